```python
import math
import jax
import jax.numpy as jnp
from jax import lax
import numpy as np

D_MODEL = 2048
BATCH = 2
SEQ = 4096
DEPTH = 2
DEC_BATCH = 8
DEC_SEQ = 4
PAST_LEN = 16384
PAGE_SIZE = 128

NSA_HEADS = 8
NSA_KV_HEADS = 2
NSA_HEAD_DIM = 128
NSA_GROUP = NSA_HEADS // NSA_KV_HEADS
ROPE_DIM = NSA_HEAD_DIM // 4
ROPE_THETA = 500000.0
CMP_BLOCK = 32
CMP_STRIDE = 16
SLC_BLOCK = 64
SLC_RATIO = SLC_BLOCK // CMP_STRIDE
N_SELECT = 16
WINDOW = 512
NSA_QBLK = 64
WIN_QBLK = 128
MLSTM_HEADS = 4
MLSTM_QK_DIM = 128
MLSTM_V_DIM = 256
MLSTM_CHUNK = 64
GDN_HEADS = 8
GDN_HEAD_DIM = 128
GDN_CONV = 4
GDN_CHUNK = 64
FFN_DIM = 5504
FFN_CONV = 3
BRANCH_WIDTH = 1024
N_BRANCH = 3
EPS = 1e-6
NEG_INF = -1e30

IN_SPLITS = (
    NSA_HEADS * NSA_HEAD_DIM,
    6 * NSA_KV_HEADS * NSA_HEAD_DIM,
    3 * NSA_HEADS,
    MLSTM_HEADS * MLSTM_QK_DIM,
    MLSTM_HEADS * MLSTM_QK_DIM,
    MLSTM_HEADS * MLSTM_V_DIM,
    MLSTM_HEADS,
    MLSTM_HEADS,
    MLSTM_HEADS * MLSTM_V_DIM,
    3 * GDN_HEADS * GDN_HEAD_DIM,
    GDN_HEADS,
    GDN_HEADS,
    GDN_HEADS * GDN_HEAD_DIM,
    N_BRANCH * D_MODEL,
)
IN_COLS = sum(IN_SPLITS)

kernel_name = 'nsa_mlstm_gdn_hybrid_step'


def rms_norm(x, g):
    xf = x.astype(jnp.float32)
    y = xf * lax.rsqrt(jnp.mean(xf * xf, axis=-1, keepdims=True) + EPS)
    return (y * g.astype(jnp.float32)).astype(x.dtype)


def l2_norm(x):
    xf = x.astype(jnp.float32)
    return xf * lax.rsqrt(jnp.sum(xf * xf, axis=-1, keepdims=True) + EPS)


def partial_rope(x, pos):
    half = ROPE_DIM // 2
    inv = ROPE_THETA ** (-jnp.arange(half, dtype=jnp.float32) / half)
    ang = pos.astype(jnp.float32)[:, None] * inv[None, :]
    shape = (1, pos.shape[0]) + (1,) * (x.ndim - 3) + (half,)
    cos = jnp.cos(ang).reshape(shape)
    sin = jnp.sin(ang).reshape(shape)
    xf = x.astype(jnp.float32)
    x1 = xf[..., :half]
    x2 = xf[..., half:ROPE_DIM]
    out = jnp.concatenate([x1 * cos - x2 * sin, x1 * sin + x2 * cos, xf[..., ROPE_DIM:]], axis=-1)
    return out.astype(x.dtype)


def causal_dwconv(x, buf, w):
    xp = jnp.concatenate([buf.astype(x.dtype), x], axis=1)
    y = lax.conv_general_dilated(xp, w[:, None, :].astype(x.dtype), (1,), 'VALID',
                                 dimension_numbers=('NWC', 'WIO', 'NWC'),
                                 feature_group_count=x.shape[-1])
    return y, xp[:, -(w.shape[0] - 1):]


def split_cols(z, sizes):
    out, off = [], 0
    for s in sizes:
        out.append(z[..., off:off + s])
        off += s
    return out


def chunked_scan(step, state, xs, chunk):
    b, s = xs[0].shape[:2]
    c = math.gcd(chunk, s)
    nc = s // c
    state = jax.tree_util.tree_map(lambda a: a.astype(jnp.float32), state)
    xs_c = tuple(jnp.moveaxis(a.astype(jnp.float32).reshape((b, nc, c) + a.shape[2:]), 1, 0) for a in xs)
    state, ys = lax.scan(step, state, xs_c)
    ys = jnp.moveaxis(ys, 0, 1).reshape((b, s) + ys.shape[3:])
    return state, ys


def mlstm_chunk(carry, xs):
    c0, n0, m0 = carry
    q, k, v, ig, lf = xs
    L = q.shape[1]
    bc = jnp.cumsum(lf, axis=1)
    incl = jnp.tril(jnp.ones((L, L), bool))
    dlog = bc[:, :, None, :] - bc[:, None, :, :] + ig[:, None, :, :]
    dlog = jnp.where(incl[None, :, :, None], dlog, -jnp.inf)
    inter = bc + m0[:, None, :]
    m_t = jnp.maximum(inter, jnp.max(dlog, axis=2))
    w_intra = jnp.exp(dlog - m_t[:, :, None, :])
    w_inter = jnp.exp(inter - m_t)
    a = jnp.einsum('bthd,bshd->btsh', q, k) * w_intra
    num = jnp.einsum('btsh,bshv->bthv', a, v) + w_inter[..., None] * jnp.einsum('bthd,bhdv->bthv', q, c0)
    den = jnp.sum(a, axis=2) + w_inter * jnp.einsum('bthd,bhd->bth', q, n0)
    h = num / jnp.maximum(jnp.abs(den), jnp.exp(-m_t))[..., None]
    m_new = m_t[:, -1]
    wk = jnp.exp(bc[:, -1:, :] - bc + ig - m_new[:, None, :])
    decay = jnp.exp(bc[:, -1] + m0 - m_new)
    c_new = decay[..., None, None] * c0 + jnp.einsum('bsh,bshd,bshv->bhdv', wk, k, v)
    n_new = decay[..., None] * n0 + jnp.einsum('bsh,bshd->bhd', wk, k)
    return (c_new, n_new, m_new), h


def gdn_chunk(s0, xs):
    q, k, v, g, beta = xs
    L = q.shape[1]
    dv = v.shape[-1]
    gc = jnp.cumsum(g, axis=1)
    gch = jnp.moveaxis(gc, 2, 1)
    incl = jnp.tril(jnp.ones((L, L), bool))
    strict = jnp.tril(jnp.ones((L, L), jnp.float32), -1)
    dec = jnp.exp(jnp.where(incl, gch[..., :, None] - gch[..., None, :], -jnp.inf))
    kb = k * beta[..., None]
    a = jnp.einsum('bthd,bshd->bhts', kb, k) * dec * strict
    rhs = jnp.concatenate([v * beta[..., None], kb * jnp.exp(gc)[..., None]], axis=-1)
    sol = lax.linalg.triangular_solve(jnp.eye(L, dtype=jnp.float32) + a, jnp.moveaxis(rhs, 1, 2),
                                      left_side=True, lower=True, unit_diagonal=True)
    u = sol[..., :dv] - jnp.einsum('bhtk,bhkv->bhtv', sol[..., dv:], s0)
    qk = jnp.einsum('bthd,bshd->bhts', q, k) * dec
    o = jnp.einsum('bhts,bhsv->bthv', qk, u) + jnp.exp(gc)[..., None] * jnp.einsum('bthk,bhkv->bthv', q, s0)
    g_last = gch[..., -1]
    wk = jnp.exp(g_last[..., None] - gch)
    s_new = jnp.exp(g_last)[..., None, None] * s0 + jnp.einsum('bhs,bshk,bhsv->bhkv', wk, k, u)
    return s_new, o


def compress_rows(rows, w1, pe, w2):
    b, tp, g, d = rows.shape
    hb = rows.reshape(b, tp // CMP_STRIDE, CMP_STRIDE, g, d)
    first = jnp.einsum('bnjgd,jde->bnge', hb, w1[:CMP_STRIDE])
    second = jnp.einsum('bnjgd,jde->bnge', hb, w1[CMP_STRIDE:])
    bias = jnp.einsum('jd,jde->e', pe, w1)
    hid = first[:, :-1] + second[:, 1:] + bias
    return jnp.einsum('bnge,ef->bngf', jax.nn.silu(hid), w2)


def nsa_cmp_slc(q, kcmp, vcmp, ks, vs, pos0):
    b, nq, h, d = q.shape
    g = NSA_KV_HEADS
    nc = kcmp.shape[1]
    ns = ks.shape[1] // SLC_BLOCK
    n_sel = min(N_SELECT, ns)
    qb = math.gcd(NSA_QBLK, nq)
    nb = nq // qb
    scale = NSA_HEAD_DIM ** -0.5
    cmp_end = jnp.arange(nc) * CMP_STRIDE + (CMP_BLOCK - 1)
    blk = jnp.arange(ns)
    tap = jnp.full((SLC_RATIO,), 2.0, jnp.float32).at[0].set(1.0)
    ks_t = jnp.swapaxes(ks, 1, 2)
    vs_t = jnp.swapaxes(vs, 1, 2)
    gi = jnp.arange(g)[None, :, None]
    offs = jnp.arange(SLC_BLOCK)

    def gather(rows, pos):
        return jax.vmap(lambda r, p: r[gi, p])(rows, pos)

    def block(args):
        i, qq = args
        t = pos0 + i * qb + jnp.arange(qb)
        qg = qq.reshape(b, qb, g, NSA_GROUP, d)
        s = jnp.einsum('bqghd,bngd->bqghn', qg, kcmp, preferred_element_type=jnp.float32) * scale
        vis = (cmp_end[None, :] <= t[:, None])[None, :, None, None, :]
        pc = jax.nn.softmax(jnp.where(vis, s, NEG_INF), axis=-1) * vis
        o_cmp = jnp.einsum('bqghn,bngd->bqghd', pc.astype(vcmp.dtype), vcmp)
        imp = jnp.pad(jnp.sum(pc, axis=3), ((0, 0), (0, 0), (0, 0), (1, 1)))
        imp_s = jnp.sum(imp[..., :SLC_RATIO * ns].reshape(b, qb, g, ns, SLC_RATIO) * tap, axis=-1) + imp[..., SLC_RATIO::SLC_RATIO]
        cur = t // SLC_BLOCK
        valid = blk[None, :] * SLC_BLOCK <= t[:, None]
        forced = (blk[None, :] == 0) | (blk[None, :] == cur[:, None]) | (blk[None, :] == cur[:, None] - 1)
        score = jnp.where(forced[None, :, None, :], jnp.inf, imp_s)
        score = jnp.where(valid[None, :, None, :], score, -jnp.inf)
        _, sel = lax.top_k(score, n_sel)
        pos = (sel[..., None] * SLC_BLOCK + offs).reshape(b, qb, g, n_sel * SLC_BLOCK)
        kk = gather(ks_t, pos)
        vv = gather(vs_t, pos)
        s2 = jnp.einsum('bqghd,bqgld->bqghl', qg, kk, preferred_element_type=jnp.float32) * scale
        m2 = (pos <= t[None, :, None, None])[:, :, :, None, :]
        p2 = jax.nn.softmax(jnp.where(m2, s2, NEG_INF), axis=-1)
        o_slc = jnp.einsum('bqghl,bqgld->bqghd', p2.astype(vv.dtype), vv)
        return o_cmp.reshape(b, qb, h, d), o_slc.reshape(b, qb, h, d)

    oc, osl = lax.map(block, (jnp.arange(nb), jnp.moveaxis(q.reshape(b, nb, qb, h, d), 1, 0)))
    return (jnp.moveaxis(oc, 0, 1).reshape(b, nq, h, d), jnp.moveaxis(osl, 0, 1).reshape(b, nq, h, d))


def window_branch(q, kw, vw, pos0):
    b, nq, h, d = q.shape
    g = NSA_KV_HEADS
    qb = math.gcd(WIN_QBLK, nq)
    nb = nq // qb
    span = WINDOW + qb
    scale = NSA_HEAD_DIM ** -0.5

    def block(args):
        i, qq = args
        kk = lax.dynamic_slice_in_dim(kw, i * qb, span, axis=1)
        vv = lax.dynamic_slice_in_dim(vw, i * qb, span, axis=1)
        qpos = pos0 + i * qb + jnp.arange(qb)
        kpos = pos0 - WINDOW + i * qb + jnp.arange(span)
        rel = qpos[:, None] - kpos[None, :]
        mask = (rel >= 0) & (rel <= WINDOW) & (kpos[None, :] >= 0)
        s = jnp.einsum('bqghd,bkgd->bqghk', qq.reshape(b, qb, g, NSA_GROUP, d), kk,
                       preferred_element_type=jnp.float32) * scale
        pr = jax.nn.softmax(jnp.where(mask[None, :, None, None, :], s, NEG_INF), axis=-1)
        o = jnp.einsum('bqghk,bkgd->bqghd', pr.astype(vv.dtype), vv)
        return o.reshape(b, qb, h, d)

    out = lax.map(block, (jnp.arange(nb), jnp.moveaxis(q.reshape(b, nb, qb, h, d), 1, 0)))
    return jnp.moveaxis(out, 0, 1).reshape(b, nq, h, d)


def hybrid_layer(x, past_kv, win_buf, m_c, m_n, m_m, g_conv, g_s, f_conv, p):
    b, nq, _ = x.shape
    H, G, DH = NSA_HEADS, NSA_KV_HEADS, NSA_HEAD_DIM
    pos0 = past_kv.shape[1]
    pos = pos0 + jnp.arange(nq, dtype=jnp.int32)
    h = rms_norm(x, p['norm1_g'])
    z = h @ p['w_in']
    (z_q, z_kv, z_ng, z_mq, z_mk, z_mv, z_mi, z_mf, z_mo,
     z_gqkv, z_ga, z_gb, z_gz, z_br) = split_cols(z, IN_SPLITS)

    q = partial_rope(rms_norm(z_q.reshape(b, nq, H, DH), p['q_norm_g']), pos)
    kv = z_kv.reshape(b, nq, 3, 2, G, DH)
    k = partial_rope(rms_norm(kv[:, :, :, 0], p['k_norm_g'][:, None, :]), pos)
    v = kv[:, :, :, 1]
    new_rows = jnp.stack([k[:, :, 0], v[:, :, 0], k[:, :, 1], v[:, :, 1]], axis=2)
    rows = jnp.concatenate([past_kv.astype(new_rows.dtype), new_rows], axis=1)
    t_all = rows.shape[1]
    tpad = -(-t_all // SLC_BLOCK) * SLC_BLOCK
    rows = jnp.pad(rows, ((0, 0), (0, tpad - t_all), (0, 0), (0, 0), (0, 0)))
    kcmp = compress_rows(rows[:, :, 0], p['w_cmp1'][0], p['cmp_pe'][0], p['w_cmp2'][0])
    vcmp = compress_rows(rows[:, :, 1], p['w_cmp1'][1], p['cmp_pe'][1], p['w_cmp2'][1])
    o_cmp, o_slc = nsa_cmp_slc(q, kcmp, vcmp, rows[:, :, 2], rows[:, :, 3], pos0)
    new_win = jnp.stack([k[:, :, 2], v[:, :, 2]], axis=2)
    wb = win_buf.shape[1]
    win_all = jnp.concatenate([jnp.zeros((b, WINDOW - wb) + new_win.shape[2:], new_win.dtype),
                               win_buf.astype(new_win.dtype), new_win], axis=1)
    o_win = window_branch(q, win_all[:, :, 0], win_all[:, :, 1], pos0)
    win_state = win_all[:, -wb:]
    ng = jax.nn.sigmoid(z_ng).reshape(b, nq, H, 3)
    o_nsa = (ng[..., 0:1] * o_cmp + ng[..., 1:2] * o_slc + ng[..., 2:3] * o_win).reshape(b, nq, H * DH)

    mq = z_mq.reshape(b, nq, MLSTM_HEADS, MLSTM_QK_DIM)
    mk = z_mk.reshape(b, nq, MLSTM_HEADS, MLSTM_QK_DIM) * (MLSTM_QK_DIM ** -0.5)
    mv = z_mv.reshape(b, nq, MLSTM_HEADS, MLSTM_V_DIM)
    ig = z_mi.astype(jnp.float32) + p['mlstm_b_i']
    lf = jax.nn.log_sigmoid(z_mf.astype(jnp.float32) + p['mlstm_b_f'])
    (m_c, m_n, m_m), hm = chunked_scan(mlstm_chunk, (m_c, m_n, m_m), (mq, mk, mv, ig, lf), MLSTM_CHUNK)
    hm = rms_norm(hm, p['mlstm_norm_g'].reshape(MLSTM_HEADS, MLSTM_V_DIM))
    o_ml = (jax.nn.sigmoid(z_mo.astype(jnp.float32)) * hm.reshape(b, nq, MLSTM_HEADS * MLSTM_V_DIM)).astype(x.dtype)

    qkv, g_conv_new = causal_dwconv(z_gqkv, g_conv, p['gdn_conv_w'])
    qkv = jax.nn.silu(qkv)
    gw = GDN_HEADS * GDN_HEAD_DIM
    gq = l2_norm(qkv[..., :gw].reshape(b, nq, GDN_HEADS, GDN_HEAD_DIM)) * (GDN_HEAD_DIM ** -0.5)
    gk = l2_norm(qkv[..., gw:2 * gw].reshape(b, nq, GDN_HEADS, GDN_HEAD_DIM))
    gv = qkv[..., 2 * gw:].reshape(b, nq, GDN_HEADS, GDN_HEAD_DIM)
    gdec = -jnp.exp(p['gdn_a_log'].astype(jnp.float32)) * jax.nn.softplus(z_ga.astype(jnp.float32) + p['gdn_dt_bias'])
    beta = jax.nn.sigmoid(z_gb.astype(jnp.float32))
    g_s_new, og = chunked_scan(gdn_chunk, g_s, (gq, gk, gv, gdec, beta), GDN_CHUNK)
    og = rms_norm(og, p['gdn_norm_g']) * jax.nn.silu(z_gz.reshape(b, nq, GDN_HEADS, GDN_HEAD_DIM).astype(jnp.float32))
    o_gdn = og.reshape(b, nq, gw).astype(x.dtype)

    ob = jnp.stack([o_nsa, o_ml, o_gdn], axis=2)
    yb = jnp.einsum('bqnc,ncd->bqnd', ob, p['w_branch'])
    gate = jax.nn.sigmoid(z_br.reshape(b, nq, N_BRANCH, D_MODEL))
    x = x + jnp.sum(gate * yb, axis=2) @ p['w_out']

    h2 = rms_norm(x, p['norm2_g'])
    u, f_conv_new = causal_dwconv(h2 @ p['w_up'], f_conv, p['ffn_conv_w'])
    x = x + (jax.nn.silu(u[..., :FFN_DIM]) * u[..., FFN_DIM:]) @ p['w_down']
    return x, (new_rows, win_state, m_c, m_n, m_m, g_conv_new, g_s_new, f_conv_new)


def stack_layers(states, i):
    return jnp.stack([s[i] for s in states], axis=0)


def setup_inputs(seed: int = 0) -> dict:
    key = jax.random.key(seed)
    ks = jax.random.split(key, 32)
    G, DH = NSA_KV_HEADS, NSA_HEAD_DIM
    n_pages = PAST_LEN // PAGE_SIZE
    n_used = DEC_BATCH * n_pages
    n_pool = n_used + max(1, n_used // 4)
    wb = min(WINDOW, PAST_LEN)

    def nrm(k, shape, scale):
        return scale * jax.random.normal(k, shape, jnp.float32)

    def gain(k, shape):
        return 1.0 + 0.01 * jax.random.normal(k, shape, jnp.float32)

    dt = jnp.exp(jax.random.uniform(ks[20], (DEPTH, GDN_HEADS), jnp.float32, math.log(1e-3), math.log(1e-1)))
    return {
        'x_prompt': nrm(ks[0], (BATCH, SEQ, D_MODEL), 1.0),
        'x_sample': nrm(ks[1], (DEC_BATCH, DEC_SEQ, D_MODEL), 1.0),
        'cache_nsa_kv': nrm(ks[2], (DEPTH, n_pool, PAGE_SIZE, 4, G, DH), 1.0),
        'page_table': jax.random.permutation(ks[3], n_pool)[:n_used].reshape(DEC_BATCH, n_pages).astype(jnp.int32),
        'cache_nsa_win': nrm(ks[4], (DEPTH, DEC_BATCH, wb, 2, G, DH), 1.0),
        'state_mlstm_c': nrm(ks[5], (DEPTH, DEC_BATCH, MLSTM_HEADS, MLSTM_QK_DIM, MLSTM_V_DIM), 0.1),
        'state_mlstm_n': nrm(ks[6], (DEPTH, DEC_BATCH, MLSTM_HEADS, MLSTM_QK_DIM), 0.1),
        'state_mlstm_m': nrm(ks[7], (DEPTH, DEC_BATCH, MLSTM_HEADS), 1.0),
        'state_gdn_conv': nrm(ks[8], (DEPTH, DEC_BATCH, GDN_CONV - 1, 3 * GDN_HEADS * GDN_HEAD_DIM), 1.0),
        'state_gdn_s': nrm(ks[9], (DEPTH, DEC_BATCH, GDN_HEADS, GDN_HEAD_DIM, GDN_HEAD_DIM), 0.1),
        'state_ffn_conv': nrm(ks[10], (DEPTH, DEC_BATCH, FFN_CONV - 1, 2 * FFN_DIM), 1.0),
        'norm1_g': gain(ks[11], (DEPTH, D_MODEL)),
        'w_in': nrm(ks[12], (DEPTH, D_MODEL, IN_COLS), D_MODEL ** -0.5),
        'q_norm_g': gain(ks[13], (DEPTH, DH)),
        'k_norm_g': gain(ks[14], (DEPTH, 3, DH)),
        'w_cmp1': nrm(ks[15], (DEPTH, 2, CMP_BLOCK, DH, DH), (CMP_BLOCK * DH) ** -0.5),
        'cmp_pe': nrm(ks[16], (DEPTH, 2, CMP_BLOCK, DH), 0.1),
        'w_cmp2': nrm(ks[17], (DEPTH, 2, DH, DH), DH ** -0.5),
        'mlstm_b_i': nrm(ks[18], (DEPTH, MLSTM_HEADS), 0.1),
        'mlstm_b_f': jnp.linspace(3.0, 6.0, MLSTM_HEADS, dtype=jnp.float32)[None, :] + nrm(ks[19], (DEPTH, MLSTM_HEADS), 0.1),
        'mlstm_norm_g': gain(ks[21], (DEPTH, MLSTM_HEADS * MLSTM_V_DIM)),
        'gdn_conv_w': nrm(ks[22], (DEPTH, GDN_CONV, 3 * GDN_HEADS * GDN_HEAD_DIM), GDN_CONV ** -0.5),
        'gdn_a_log': jnp.log(jax.random.uniform(ks[23], (DEPTH, GDN_HEADS), jnp.float32, 1.0, 16.0)),
        'gdn_dt_bias': dt + jnp.log(-jnp.expm1(-dt)),
        'gdn_norm_g': gain(ks[24], (DEPTH, GDN_HEAD_DIM)),
        'w_branch': nrm(ks[25], (DEPTH, N_BRANCH, BRANCH_WIDTH, D_MODEL), BRANCH_WIDTH ** -0.5),
        'w_out': nrm(ks[26], (DEPTH, D_MODEL, D_MODEL), D_MODEL ** -0.5),
        'norm2_g': gain(ks[27], (DEPTH, D_MODEL)),
        'w_up': nrm(ks[28], (DEPTH, D_MODEL, 2 * FFN_DIM), D_MODEL ** -0.5),
        'ffn_conv_w': nrm(ks[29], (DEPTH, FFN_CONV, 2 * FFN_DIM), FFN_CONV ** -0.5),
        'w_down': nrm(ks[30], (DEPTH, FFN_DIM, D_MODEL), FFN_DIM ** -0.5),
    }


def reference(x_prompt, x_sample, cache_nsa_kv, page_table, cache_nsa_win, state_mlstm_c, state_mlstm_n,
              state_mlstm_m, state_gdn_conv, state_gdn_s, state_ffn_conv, norm1_g, w_in, q_norm_g, k_norm_g,
              w_cmp1, cmp_pe, w_cmp2, mlstm_b_i, mlstm_b_f, mlstm_norm_g, gdn_conv_w, gdn_a_log, gdn_dt_bias,
              gdn_norm_g, w_branch, w_out, norm2_g, w_up, ffn_conv_w, w_down):
    G, DH = NSA_KV_HEADS, NSA_HEAD_DIM
    f32 = jnp.float32
    wb = cache_nsa_win.shape[2]
    n_dec, n_pages = page_table.shape
    bp = x_prompt.shape[0]
    xp, xs = x_prompt, x_sample
    st_p, st_s = [], []
    for l in range(DEPTH):
        p = dict(norm1_g=norm1_g[l], w_in=w_in[l], q_norm_g=q_norm_g[l], k_norm_g=k_norm_g[l],
                 w_cmp1=w_cmp1[l], cmp_pe=cmp_pe[l], w_cmp2=w_cmp2[l], mlstm_b_i=mlstm_b_i[l],
                 mlstm_b_f=mlstm_b_f[l], mlstm_norm_g=mlstm_norm_g[l], gdn_conv_w=gdn_conv_w[l],
                 gdn_a_log=gdn_a_log[l], gdn_dt_bias=gdn_dt_bias[l], gdn_norm_g=gdn_norm_g[l],
                 w_branch=w_branch[l], w_out=w_out[l], norm2_g=norm2_g[l], w_up=w_up[l],
                 ffn_conv_w=ffn_conv_w[l], w_down=w_down[l])
        xp, sp = hybrid_layer(
            xp,
            jnp.zeros((bp, 0, 4, G, DH), xp.dtype),
            jnp.zeros((bp, wb, 2, G, DH), xp.dtype),
            jnp.zeros((bp, MLSTM_HEADS, MLSTM_QK_DIM, MLSTM_V_DIM), f32),
            jnp.zeros((bp, MLSTM_HEADS, MLSTM_QK_DIM), f32),
            jnp.zeros((bp, MLSTM_HEADS), f32),
            jnp.zeros((bp, GDN_CONV - 1, 3 * GDN_HEADS * GDN_HEAD_DIM), xp.dtype),
            jnp.zeros((bp, GDN_HEADS, GDN_HEAD_DIM, GDN_HEAD_DIM), f32),
            jnp.zeros((bp, FFN_CONV - 1, 2 * FFN_DIM), xp.dtype),
            p)
        past = cache_nsa_kv[l][page_table].reshape((n_dec, n_pages * PAGE_SIZE) + cache_nsa_kv.shape[3:])
        xs, ss = hybrid_layer(xs, past, cache_nsa_win[l], state_mlstm_c[l], state_mlstm_n[l], state_mlstm_m[l],
                              state_gdn_conv[l], state_gdn_s[l], state_ffn_conv[l], p)
        st_p.append(sp)
        st_s.append(ss)
    return (xp, xs,
            stack_layers(st_p, 0), stack_layers(st_s, 0),
            stack_layers(st_p, 1), stack_layers(st_s, 1),
            stack_layers(st_p, 2), stack_layers(st_s, 2),
            stack_layers(st_p, 3), stack_layers(st_s, 3),
            stack_layers(st_p, 4), stack_layers(st_s, 4),
            stack_layers(st_p, 5), stack_layers(st_s, 5),
            stack_layers(st_p, 6), stack_layers(st_s, 6),
            stack_layers(st_p, 7), stack_layers(st_s, 7))
```

```python
import functools
import math

import numpy as np
import jax
import jax.numpy as jnp
from jax import lax
from jax.experimental import pallas as pl
from jax.experimental.pallas import tpu as pltpu

F32 = jnp.float32
BF16 = jnp.bfloat16
HI = lax.Precision.HIGHEST

EPS = 1e-6
NEG_INF = -1e30
LANE = 128
PAGE = 128
DH = 128
NSA_HEADS = 8
NSA_G = 2
NSA_GROUP = NSA_HEADS // NSA_G
ROPE_DIM = DH // 4
ROPE_THETA = 500000.0
CMP_BLOCK = 32
CMP_STRIDE = 16
SLC_BLOCK = 64
N_SELECT = 16
WINDOW = 512
ML_H, ML_DK, ML_DV = 4, 128, 256
GD_H, GD_D = 8, 128
GDN_CONV = 4
FFN_CONV = 3
CHUNK = 64
FFN_PAD = 5632
KC = 1024
BLK_PER_KC = KC // SLC_BLOCK
WIN_PER_LANE = LANE // BLK_PER_KC
VMEM_LIMIT = 56 * 1024 * 1024

Z_Q, Z_K, Z_V = 0, 1024, 1792
Z_MQ, Z_MK, Z_MV, Z_MO = 2560, 3072, 3584, 4608
Z_GQKV, Z_GZ, Z_BR, Z_SM = 5632, 8704, 9728, 15872
Z_COLS = 16384
SM_NG, SM_MI, SM_MF, SM_GA, SM_GB = 0, 24, 28, 32, 40


def _cp(sem):
    return pltpu.CompilerParams(dimension_semantics=sem, vmem_limit_bytes=VMEM_LIMIT)


def _iota(shape, dim):
    return lax.broadcasted_iota(jnp.int32, shape, dim)


def _dot(a, b, **kw):
    return jnp.dot(a, b, preferred_element_type=F32, **kw)


def _dot_nt(a, b, **kw):
    return lax.dot_general(a, b, (((1,), (1,)), ((), ())), preferred_element_type=F32, **kw)


def _silu(x):
    return x * jax.nn.sigmoid(x)


def _rmsnorm_body(x_ref, g_ref, o_ref):
    x = x_ref[...]
    y = x * lax.rsqrt(jnp.mean(x * x, axis=-1, keepdims=True) + EPS)
    o_ref[...] = (y * g_ref[...]).astype(o_ref.dtype)


def rmsnorm_bf16(x, g, tm):
    t, d = x.shape
    return pl.pallas_call(
        _rmsnorm_body, grid=(t // tm,),
        in_specs=[pl.BlockSpec((tm, d), lambda i: (i, 0)), pl.BlockSpec((1, d), lambda i: (0, 0))],
        out_specs=pl.BlockSpec((tm, d), lambda i: (i, 0)),
        out_shape=jax.ShapeDtypeStruct((t, d), BF16),
        compiler_params=_cp(("parallel",)), name="rmsnorm")(x, g.reshape(1, d))


def _mm_body(a_ref, w_ref, o_ref):
    o_ref[...] = _dot(a_ref[...], w_ref[...]).astype(o_ref.dtype)


def _mm_res_body(a_ref, w_ref, r_ref, o_ref):
    o_ref[...] = r_ref[...] + _dot(a_ref[...], w_ref[...])


def matmul(a, w, tm, tn, res=None, name="matmul"):
    m, k = a.shape
    n = w.shape[1]
    in_specs = [pl.BlockSpec((tm, k), lambda i, j: (i, 0)), pl.BlockSpec((k, tn), lambda i, j: (0, j))]
    args = [a, w]
    body = _mm_body
    if res is not None:
        in_specs.append(pl.BlockSpec((tm, tn), lambda i, j: (i, j)))
        args.append(res)
        body = _mm_res_body
    return pl.pallas_call(
        body, grid=(m // tm, n // tn), in_specs=in_specs,
        out_specs=pl.BlockSpec((tm, tn), lambda i, j: (i, j)),
        out_shape=jax.ShapeDtypeStruct((m, n), F32),
        compiler_params=_cp(("parallel", "parallel")), name=name)(*args)


def _nsa_prep_body(z_ref, c_ref, s1_ref, s2_ref, qg_ref, kg_ref, q_out, rows_out, win_out):
    c = c_ref[...]
    s1 = s1_ref[...]
    s2 = s2_ref[...]

    def norm_rope(x, g):
        y = x * lax.rsqrt(jnp.mean(x * x, axis=-1, keepdims=True) + EPS) * g
        return y * c + pltpu.roll(y, LANE - ROPE_DIM // 2, 1) * s1 + pltpu.roll(y, ROPE_DIM // 2, 1) * s2

    for h in range(NSA_HEADS):
        sl = slice(h * DH, (h + 1) * DH)
        q_out[:, sl] = norm_rope(z_ref[:, sl], qg_ref[...]).astype(q_out.dtype)
    for br in range(3):
        for g in range(NSA_G):
            col = Z_K + (br * NSA_G + g) * DH
            kk = norm_rope(z_ref[:, col:col + DH], kg_ref[br:br + 1, :])
            vv = z_ref[:, col + (Z_V - Z_K):col + (Z_V - Z_K) + DH]
            if br < 2:
                rows_out[:, (2 * br) * 2 * DH + g * DH:(2 * br) * 2 * DH + (g + 1) * DH] = kk
                rows_out[:, (2 * br + 1) * 2 * DH + g * DH:(2 * br + 1) * 2 * DH + (g + 1) * DH] = vv
            else:
                win_out[:, g * DH:(g + 1) * DH] = kk
                win_out[:, 2 * DH + g * DH:2 * DH + (g + 1) * DH] = vv


def nsa_prep(z, tabs, q_g, k_g, nqp, tb):
    t = z.shape[0]
    nblk = nqp // tb
    tab_spec = pl.BlockSpec((tb, LANE), lambda i: (i % nblk, 0))
    return pl.pallas_call(
        _nsa_prep_body, grid=(t // tb,),
        in_specs=[pl.BlockSpec((tb, Z_MQ), lambda i: (i, 0)), tab_spec, tab_spec, tab_spec,
                  pl.BlockSpec((1, DH), lambda i: (0, 0)), pl.BlockSpec((3, DH), lambda i: (0, 0))],
        out_specs=[pl.BlockSpec((tb, 1024), lambda i: (i, 0)), pl.BlockSpec((tb, 1024), lambda i: (i, 0)),
                   pl.BlockSpec((tb, 512), lambda i: (i, 0))],
        out_shape=[jax.ShapeDtypeStruct((t, 1024), BF16), jax.ShapeDtypeStruct((t, 1024), F32),
                   jax.ShapeDtypeStruct((t, 512), F32)],
        compiler_params=_cp(("parallel",)), name="nsa_prep")(z, *tabs, q_g.reshape(1, DH), k_g)


def rope_tables(pos0, nq, nqp):
    half = ROPE_DIM // 2
    pos = pos0 + jnp.arange(nqp, dtype=jnp.int32)
    inv = ROPE_THETA ** (-jnp.arange(half, dtype=F32) / half)
    ang = pos.astype(F32)[:, None] * inv[None, :]
    cos, sin = jnp.cos(ang), jnp.sin(ang)
    zeros = jnp.zeros((nqp, LANE - ROPE_DIM), F32)
    c = jnp.concatenate([cos, cos, jnp.ones((nqp, LANE - ROPE_DIM), F32)], axis=1)
    s1 = jnp.concatenate([-sin, jnp.zeros((nqp, half), F32), zeros], axis=1)
    s2 = jnp.concatenate([jnp.zeros((nqp, half), F32), sin, zeros], axis=1)
    return c, s1, s2


def _compress_body(*refs, pps, paged):
    if paged:
        refs = refs[1:]
    slots = refs[:pps]
    w_refs = refs[pps:pps + 2]
    o_ref = refs[pps + 2]
    nb = PAGE // CMP_STRIDE
    for kind in range(2):
        xs = []
        for g in range(NSA_G):
            lane0 = kind * 2 * DH + g * DH
            per_slot = [jnp.concatenate([s[:, j * 1024 + lane0:j * 1024 + lane0 + DH]
                                         for j in range(CMP_STRIDE)], axis=1) for s in slots]
            xs.append(jnp.concatenate(per_slot, axis=0) if pps > 1 else per_slot[0])
        x = jnp.concatenate(xs, axis=0).astype(BF16)
        fs = _dot(x, w_refs[kind][...])
        for g in range(NSA_G):
            o_ref[:, (kind * 2 + g) * 2 * DH:(kind * 2 + g + 1) * 2 * DH] = fs[g * nb * pps:(g + 1) * nb * pps]


def compress_pages(src, wcat, nb_batch, n_pages, pps, table=None, layer=None):
    nb = PAGE // CMP_STRIDE
    steps = n_pages // pps
    w_spec = pl.BlockSpec((CMP_STRIDE * DH, 2 * DH), lambda b, i, *_: (0, 0))
    out_spec = pl.BlockSpec((None, nb * pps, 1024), lambda b, i, *_: (b, i, 0))
    out_shape = jax.ShapeDtypeStruct((nb_batch, n_pages * nb, 1024), F32)
    body = functools.partial(_compress_body, pps=pps, paged=table is not None)
    src = src.reshape(src.shape[:-2] + (nb, CMP_STRIDE * 1024))
    if table is None:
        slot_specs = [pl.BlockSpec((None, nb, CMP_STRIDE * 1024), functools.partial(
            lambda b, i, s: (b * n_pages + i * pps + s, 0, 0), s=s)) for s in range(pps)]
        return pl.pallas_call(
            body, grid=(nb_batch, steps), in_specs=slot_specs + [w_spec, w_spec], out_specs=out_spec,
            out_shape=out_shape, compiler_params=_cp(("parallel", "parallel")),
            name="compress")(*([src] * pps), wcat[0], wcat[1])
    slot_specs = [pl.BlockSpec((None, None, nb, CMP_STRIDE * 1024), functools.partial(
        lambda b, i, tbl, s: (layer, tbl[b, i * pps + s], 0, 0), s=s)) for s in range(pps)]
    gs = pltpu.PrefetchScalarGridSpec(num_scalar_prefetch=1, grid=(nb_batch, steps),
                                      in_specs=slot_specs + [w_spec, w_spec], out_specs=out_spec)
    return pl.pallas_call(body, grid_spec=gs, out_shape=out_shape,
                          compiler_params=_cp(("parallel", "parallel")),
                          name="compress_paged")(table, *([src] * pps), wcat[0], wcat[1])


def _cfin_body(f_ref, s_ref, pe_ref, w1_ref, w2_ref, o_ref):
    for kind in range(2):
        bias = _dot(pe_ref[kind], w1_ref[kind])[0:1, :]
        for g in range(NSA_G):
            sl = slice((kind * 2 + g) * DH, (kind * 2 + g + 1) * DH)
            hid = f_ref[:, sl] + s_ref[:, sl] + bias
            o_ref[:, sl] = _dot(_silu(hid).astype(BF16), w2_ref[kind]).astype(o_ref.dtype)


def compress_finish(first, second_next, pe8, w1flat, w2):
    b, ncp, _ = first.shape
    spec = pl.BlockSpec((None, ncp, 512), lambda i: (i, 0, 0))
    return pl.pallas_call(
        _cfin_body, grid=(b,),
        in_specs=[spec, spec, pl.BlockSpec(pe8.shape, lambda i: (0, 0, 0)),
                  pl.BlockSpec(w1flat.shape, lambda i: (0, 0, 0)), pl.BlockSpec(w2.shape, lambda i: (0, 0, 0))],
        out_specs=spec, out_shape=jax.ShapeDtypeStruct((b, ncp, 512), BF16),
        compiler_params=_cp(("parallel",)), name="compress_finish")(first, second_next, pe8, w1flat, w2)


def _cmp_body(q_ref, kv_ref, m_ref, o_ref, sel_ref, *, qb, pos0, ncp, ns, nsp, nsel):
    i = pl.program_id(1)
    scale = DH ** -0.5
    rows = NSA_GROUP * qb
    t4 = pos0 + i * qb + (_iota((rows, 1), 0) & (qb - 1))
    vis = (_iota((1, ncp), 1) * CMP_STRIDE + (CMP_BLOCK - 1)) <= t4
    t1 = pos0 + i * qb + _iota((qb, 1), 0)
    blk = _iota((1, nsp), 1)
    cur = t1 // SLC_BLOCK
    valid = blk * SLC_BLOCK <= t1
    forced = (blk == 0) | (blk == cur) | (blk == cur - 1)
    for g in range(NSA_G):
        qg = jnp.concatenate([q_ref[:, (g * NSA_GROUP + h) * DH:(g * NSA_GROUP + h + 1) * DH]
                              for h in range(NSA_GROUP)], axis=0)
        s = _dot_nt(qg, kv_ref[:, g * DH:(g + 1) * DH]) * scale
        s = jnp.where(vis, s, NEG_INF)
        e = jnp.exp(s - jnp.max(s, axis=-1, keepdims=True))
        pc = jnp.where(vis, e / jnp.sum(e, axis=-1, keepdims=True), 0.0)
        o = _dot(pc.astype(BF16), kv_ref[:, (2 + g) * DH:(3 + g) * DH])
        for h in range(NSA_GROUP):
            o_ref[:, (g * NSA_GROUP + h) * DH:(g * NSA_GROUP + h + 1) * DH] = o[h * qb:(h + 1) * qb]
        pcs = (pc[0:qb] + pc[qb:2 * qb]) + (pc[2 * qb:3 * qb] + pc[3 * qb:4 * qb])
        imp = _dot(pcs, m_ref[...], precision=HI)
        score = jnp.where(valid, jnp.where(forced, jnp.inf, imp), -jnp.inf)
        rank = jnp.zeros((qb, nsp), F32)
        for j in range(ns):
            sj = score[:, j:j + 1]
            ahead = (sj > score) | ((sj == score) & (blk > j))
            rank = rank + jnp.where(ahead, 1.0, 0.0)
        sel_ref[:, g * nsp:(g + 1) * nsp] = jnp.where(rank < nsel, 1.0, 0.0).astype(sel_ref.dtype)


def cmp_select(q, kvc, mmat, nb_batch, nqp, qb, pos0, ns, nsel):
    ncp = kvc.shape[1]
    nsp = mmat.shape[1]
    nblk = nqp // qb
    body = functools.partial(_cmp_body, qb=qb, pos0=pos0, ncp=ncp, ns=ns, nsp=nsp, nsel=nsel)
    return pl.pallas_call(
        body, grid=(nb_batch, nblk),
        in_specs=[pl.BlockSpec((qb, 1024), lambda b, i: (b * nblk + i, 0)),
                  pl.BlockSpec((None, ncp, 512), lambda b, i: (b, 0, 0)),
                  pl.BlockSpec((ncp, nsp), lambda b, i: (0, 0))],
        out_specs=[pl.BlockSpec((qb, 1024), lambda b, i: (b * nblk + i, 0)),
                   pl.BlockSpec((qb, NSA_G * nsp), lambda b, i: (b * nblk + i, 0))],
        out_shape=[jax.ShapeDtypeStruct((nb_batch * nqp, 1024), F32),
                   jax.ShapeDtypeStruct((nb_batch * nqp, NSA_G * nsp), BF16)],
        compiler_params=_cp(("parallel", "parallel")), name="cmp_select")(q, kvc, mmat)


def importance_matrix(ncp, nsp):
    m = np.zeros((ncp, nsp), np.float32)
    for j in range(nsp):
        for n, w in ((4 * j - 1, 1.0), (4 * j, 2.0), (4 * j + 1, 2.0), (4 * j + 2, 2.0), (4 * j + 3, 1.0)):
            if 0 <= n < ncp:
                m[n, j] = w
    return jnp.asarray(m)


def expand_matrices():
    e = np.zeros((WIN_PER_LANE, LANE, KC), np.float32)
    for w in range(WIN_PER_LANE):
        for key in range(KC):
            e[w, BLK_PER_KC * w + key // SLC_BLOCK, key] = 1.0
    return jnp.asarray(e, BF16)


def _slc_body(*refs, qb, pos0, nslot, has_new, causal_skip):
    if has_new:
        refs = refs[1:]
    q_ref, sel0_ref, sel1_ref, e_ref = refs[:4]
    slots = refs[4:4 + nslot]
    k = 4 + nslot
    new_ref = refs[k] if has_new else None
    k += 1 if has_new else 0
    o_ref, m_sc, l_sc, acc_sc = refs[k:k + 4]
    i = pl.program_id(1)
    c = pl.program_id(2)
    nchunk = pl.num_programs(2)
    scale = DH ** -0.5
    rows = NSA_GROUP * qb

    @pl.when(c == 0)
    def _():
        m_sc[...] = jnp.full(m_sc.shape, NEG_INF, F32)
        l_sc[...] = jnp.zeros(l_sc.shape, F32)
        acc_sc[...] = jnp.zeros(acc_sc.shape, F32)

    def step():
        parts = [s[...] for s in slots]
        if has_new:
            parts[0] = jnp.where(c == nchunk - 1, new_ref[...], parts[0])
        kv = (jnp.concatenate(parts, axis=0) if nslot > 1 else parts[0]).astype(BF16)
        t4 = pos0 + i * qb + (_iota((rows, 1), 0) & (qb - 1))
        kpos = c * KC + _iota((1, KC), 1)
        causal = kpos <= t4
        ew = e_ref[c % WIN_PER_LANE]
        for g, sel_ref in ((0, sel0_ref), (1, sel1_ref)):
            sel1 = _dot(sel_ref[...], ew)
            selx = jnp.concatenate([sel1] * NSA_GROUP, axis=0)
            mask = causal & (selx > 0.5)
            qg = jnp.concatenate([q_ref[:, (g * NSA_GROUP + h) * DH:(g * NSA_GROUP + h + 1) * DH]
                                  for h in range(NSA_GROUP)], axis=0)
            s = _dot_nt(qg, kv[:, g * DH:(g + 1) * DH]) * scale
            s = jnp.where(mask, s, NEG_INF)
            m_old = m_sc[g]
            m_new = jnp.maximum(m_old, jnp.max(s, axis=-1, keepdims=True))
            alpha = jnp.exp(m_old - m_new)
            p = jnp.where(mask, jnp.exp(s - m_new), 0.0)
            l_sc[g] = alpha * l_sc[g] + jnp.sum(p, axis=-1, keepdims=True)
            acc_sc[g] = alpha * acc_sc[g] + _dot(p.astype(BF16), kv[:, (2 + g) * DH:(3 + g) * DH])
            m_sc[g] = m_new

    if causal_skip:
        pl.when(c * KC <= pos0 + i * qb + qb - 1)(step)
    else:
        step()

    @pl.when(c == nchunk - 1)
    def _():
        for g in range(NSA_G):
            o = acc_sc[g] / l_sc[g]
            for h in range(NSA_GROUP):
                o_ref[:, (g * NSA_GROUP + h) * DH:(g * NSA_GROUP + h + 1) * DH] = o[h * qb:(h + 1) * qb]


def slc_attention(q, sel, emat, nb_batch, nqp, qb, pos0, nsp, rows=None, cache=None, table=None,
                  layer=None, newpage=None):
    nblk = nqp // qb
    lanes_g = nsp // LANE
    has_new = cache is not None
    scratch = [pltpu.VMEM((NSA_G, NSA_GROUP * qb, 1), F32), pltpu.VMEM((NSA_G, NSA_GROUP * qb, 1), F32),
               pltpu.VMEM((NSA_G, NSA_GROUP * qb, DH), F32)]
    out_shape = jax.ShapeDtypeStruct((nb_batch * nqp, 1024), F32)
    q_map = lambda b, i, c, *_: (b * nblk + i, 0)
    e_spec = pl.BlockSpec((WIN_PER_LANE, LANE, KC), lambda b, i, c, *_: (0, 0, 0))
    if not has_new:
        nq_keys = rows.shape[1]
        nchunk = nq_keys // KC
        last = lambda i: (pos0 + i * qb + qb - 1) // KC
        sel_specs = [pl.BlockSpec((qb, LANE), functools.partial(
            lambda b, i, c, g: (b * nblk + i, g * lanes_g + jnp.minimum(c, last(i)) // WIN_PER_LANE), g=g))
            for g in range(NSA_G)]
        kv_spec = pl.BlockSpec((None, KC, 512), lambda b, i, c: (b, jnp.minimum(c, last(i)), 1))
        body = functools.partial(_slc_body, qb=qb, pos0=pos0, nslot=1, has_new=False, causal_skip=True)
        return pl.pallas_call(
            body, grid=(nb_batch, nblk, nchunk),
            in_specs=[pl.BlockSpec((qb, 1024), q_map)] + sel_specs + [e_spec, kv_spec],
            out_specs=pl.BlockSpec((qb, 1024), q_map), out_shape=out_shape, scratch_shapes=scratch,
            compiler_params=_cp(("parallel", "parallel", "arbitrary")), name="slc_attention")(q, sel, sel, emat, rows)
    n_pages = table.shape[1]
    nslot = KC // PAGE
    nchunk = n_pages // nslot + 1
    sel_specs = [pl.BlockSpec((qb, LANE), functools.partial(
        lambda b, i, c, tbl, g: (b * nblk + i, g * lanes_g + c // WIN_PER_LANE), g=g)) for g in range(NSA_G)]
    slot_specs = [pl.BlockSpec((None, None, PAGE, 512), functools.partial(
        lambda b, i, c, tbl, s: (layer, tbl[b, jnp.minimum(c * nslot + s, n_pages - 1)], 0, 1), s=s))
        for s in range(nslot)]
    new_spec = pl.BlockSpec((None, PAGE, 512), lambda b, i, c, tbl: (b, 0, 1))
    body = functools.partial(_slc_body, qb=qb, pos0=pos0, nslot=nslot, has_new=True, causal_skip=False)
    gs = pltpu.PrefetchScalarGridSpec(
        num_scalar_prefetch=1, grid=(nb_batch, nblk, nchunk),
        in_specs=[pl.BlockSpec((qb, 1024), q_map)] + sel_specs + [e_spec] + slot_specs + [new_spec],
        out_specs=pl.BlockSpec((qb, 1024), q_map), scratch_shapes=scratch)
    return pl.pallas_call(body, grid_spec=gs, out_shape=out_shape,
                          compiler_params=_cp(("parallel", "parallel", "arbitrary")),
                          name="slc_attention_paged")(table, q, sel, sel, emat, *([cache] * nslot), newpage)


def _win_body(q_ref, w_ref, o_ref, *, qb, pos0, span):
    i = pl.program_id(1)
    scale = DH ** -0.5
    rows = NSA_GROUP * qb
    kv = w_ref[pl.ds(pl.multiple_of(i * qb, qb), span), :].astype(BF16)
    qpos = pos0 + i * qb + (_iota((rows, 1), 0) & (qb - 1))
    kpos = pos0 - WINDOW + i * qb + _iota((1, span), 1)
    rel = qpos - kpos
    mask = (rel >= 0) & (rel <= WINDOW) & (kpos >= 0)
    for g in range(NSA_G):
        qg = jnp.concatenate([q_ref[:, (g * NSA_GROUP + h) * DH:(g * NSA_GROUP + h + 1) * DH]
                              for h in range(NSA_GROUP)], axis=0)
        s = jnp.where(mask, _dot_nt(qg, kv[:, g * DH:(g + 1) * DH]) * scale, NEG_INF)
        e = jnp.exp(s - jnp.max(s, axis=-1, keepdims=True))
        p = e / jnp.sum(e, axis=-1, keepdims=True)
        o = _dot(p.astype(BF16), kv[:, (2 + g) * DH:(3 + g) * DH])
        for h in range(NSA_GROUP):
            o_ref[:, (g * NSA_GROUP + h) * DH:(g * NSA_GROUP + h + 1) * DH] = o[h * qb:(h + 1) * qb]


def win_attention(q, win_all, nb_batch, nqp, qb, pos0):
    nblk = nqp // qb
    wtot = win_all.shape[1]
    body = functools.partial(_win_body, qb=qb, pos0=pos0, span=WINDOW + qb)
    return pl.pallas_call(
        body, grid=(nb_batch, nblk),
        in_specs=[pl.BlockSpec((qb, 1024), lambda b, i: (b * nblk + i, 0)),
                  pl.BlockSpec((None, wtot, 512), lambda b, i: (b, 0, 0))],
        out_specs=pl.BlockSpec((qb, 1024), lambda b, i: (b * nblk + i, 0)),
        out_shape=jax.ShapeDtypeStruct((nb_batch * nqp, 1024), F32),
        compiler_params=_cp(("parallel", "parallel")), name="win_attention")(q, win_all)


def _combine_body(oc_ref, os_ref, ow_ref, zs_ref, o_ref):
    gate = jax.nn.sigmoid(zs_ref[...])
    for h in range(NSA_HEADS):
        sl = slice(h * DH, (h + 1) * DH)
        c0 = SM_NG + 3 * h
        o = (gate[:, c0:c0 + 1] * oc_ref[:, sl] + gate[:, c0 + 1:c0 + 2] * os_ref[:, sl]
             + gate[:, c0 + 2:c0 + 3] * ow_ref[:, sl])
        o_ref[:, sl] = o.astype(o_ref.dtype)


def nsa_combine(oc, osl, ow, z, tb):
    t = oc.shape[0]
    spec = pl.BlockSpec((tb, 1024), lambda i: (i, 0))
    return pl.pallas_call(
        _combine_body, grid=(t // tb,),
        in_specs=[spec, spec, spec, pl.BlockSpec((tb, LANE), lambda i: (i, Z_SM // LANE))],
        out_specs=spec, out_shape=jax.ShapeDtypeStruct((t, 1024), BF16),
        compiler_params=_cp(("parallel",)), name="nsa_combine")(oc, osl, ow, z)


def _pad_rows(x, n):
    if x.shape[0] == n:
        return x
    return jnp.concatenate([x, jnp.zeros((n - x.shape[0], x.shape[1]), x.dtype)], axis=0)


def _lane_pick(x, lane):
    return jnp.sum(jnp.where(_iota((1, x.shape[1]), 1) == lane, x, 0.0), axis=1, keepdims=True)


def _row_bcast(col):
    n = col.shape[0]
    eye = (_iota((n, n), 0) == _iota((n, n), 1))
    return _dot(jnp.ones((n, n), F32), jnp.where(eye, col, 0.0), precision=HI)


def _cumsum_col(col):
    n = col.shape[0]
    tri = jnp.where(_iota((n, n), 0) >= _iota((n, n), 1), 1.0, 0.0)
    return _dot(tri, jnp.broadcast_to(col, (n, LANE)), precision=HI)[:, 0:1]


def _mlstm_body(q_ref, k_ref, v_ref, og_ref, zs_ref, b_ref, ng_ref, c0_ref, n0_ref, m0_ref,
                o_ref, c_out, n_out, m_out, c_sc, n_sc, m_sc, *, nvalid):
    h = pl.program_id(1)
    ck = pl.program_id(2)
    n = CHUNK
    lin = q_ref.shape[0]

    @pl.when(ck == 0)
    def _():
        c_sc[...] = c0_ref[...]
        n_sc[...] = n0_ref[...]
        m_sc[...] = m0_ref[...]

    q = _pad_rows(q_ref[...], n)
    k = _pad_rows(k_ref[...], n) * (ML_DK ** -0.5)
    v = _pad_rows(v_ref[...], n)
    gates = _pad_rows(zs_ref[...], n) + b_ref[...]
    ig = _lane_pick(gates, SM_MI + h)
    fp = _lane_pick(gates, SM_MF + h)
    lf = jnp.minimum(fp, 0.0) - jnp.log1p(jnp.exp(-jnp.abs(fp)))
    if nvalid < n:
        live = _iota((n, 1), 0) < nvalid
        ig = jnp.where(live, ig, NEG_INF)
        lf = jnp.where(live, lf, 0.0)
    bc = _cumsum_col(lf)
    incl = _iota((n, n), 0) >= _iota((n, n), 1)
    dlog = jnp.where(incl, bc + _row_bcast(ig - bc), -jnp.inf)
    m0 = m_sc[...]
    c0 = c_sc[...]
    n0 = n_sc[...]
    inter = bc + m0
    m_t = jnp.maximum(inter, jnp.max(dlog, axis=1, keepdims=True))
    w_intra = jnp.exp(dlog - m_t)
    w_inter = jnp.exp(inter - m_t)
    qb16 = q.astype(BF16)
    a = _dot_nt(qb16, k.astype(BF16)) * w_intra
    num = _dot(a.astype(BF16), v.astype(BF16)) + w_inter * _dot(qb16, c0.astype(BF16))
    den = jnp.sum(a, axis=1, keepdims=True) + w_inter * jnp.sum(q * n0, axis=1, keepdims=True)
    hh = num / jnp.maximum(jnp.abs(den), jnp.exp(-m_t))
    m_new = m_t[n - 1:n]
    bc_last = bc[n - 1:n]
    wk = jnp.exp(bc_last - bc + ig - m_new)
    decay = jnp.exp(bc_last + m0 - m_new)
    kw = k * wk
    c_new = decay * c0 + _dot(kw.T.astype(BF16), v.astype(BF16))
    n_new = decay * n0 + jnp.sum(kw, axis=0, keepdims=True)
    c_sc[...] = c_new
    n_sc[...] = n_new
    m_sc[...] = m_new
    c_out[...] = c_new
    n_out[...] = n_new
    m_out[...] = m_new
    hn = hh * lax.rsqrt(jnp.mean(hh * hh, axis=-1, keepdims=True) + EPS) * ng_ref[...]
    o = jax.nn.sigmoid(_pad_rows(og_ref[...], n)) * hn
    o_ref[...] = o[:lin].astype(o_ref.dtype)


def mlstm_scan(z, bias_row, norm_g, c0, n0, m0, nb_batch, nqp, nvalid):
    lin = min(CHUNK, nqp)
    nc = nqp // lin
    tok = lambda w, off: pl.BlockSpec((lin, w), lambda b, h, c: (b * nc + c, off // w + h))
    body = functools.partial(_mlstm_body, nvalid=min(nvalid, CHUNK))
    st = lambda shape: pl.BlockSpec((None, None) + shape, lambda b, h, c: (b, h, 0, 0))
    return pl.pallas_call(
        body, grid=(nb_batch, ML_H, nc),
        in_specs=[tok(ML_DK, Z_MQ), tok(ML_DK, Z_MK), tok(ML_DV, Z_MV), tok(ML_DV, Z_MO),
                  pl.BlockSpec((lin, LANE), lambda b, h, c: (b * nc + c, Z_SM // LANE)),
                  pl.BlockSpec((1, LANE), lambda b, h, c: (0, 0)),
                  pl.BlockSpec((None, 1, ML_DV), lambda b, h, c: (h, 0, 0)),
                  st((ML_DK, ML_DV)), st((1, ML_DK)), st((1, 1))],
        out_specs=[pl.BlockSpec((lin, ML_DV), lambda b, h, c: (b * nc + c, h)),
                   st((ML_DK, ML_DV)), st((1, ML_DK)), st((1, 1))],
        out_shape=[jax.ShapeDtypeStruct((nb_batch * nqp, ML_H * ML_DV), BF16),
                   jax.ShapeDtypeStruct((nb_batch, ML_H, ML_DK, ML_DV), F32),
                   jax.ShapeDtypeStruct((nb_batch, ML_H, 1, ML_DK), F32),
                   jax.ShapeDtypeStruct((nb_batch, ML_H, 1, 1), F32)],
        scratch_shapes=[pltpu.VMEM((ML_DK, ML_DV), F32), pltpu.VMEM((1, ML_DK), F32), pltpu.VMEM((1, 1), F32)],
        compiler_params=_cp(("parallel", "parallel", "arbitrary")),
        name="mlstm_scan")(z, z, z, z, z, bias_row, norm_g.reshape(ML_H, 1, ML_DV), c0, n0, m0)


def _gdn_body(q_ref, k_ref, v_ref, gz_ref, zs_ref, b_ref, a_ref, ng_ref, s0_ref,
              o_ref, s_out, s_sc, *, nvalid):
    h = pl.program_id(1)
    ck = pl.program_id(2)
    n = CHUNK
    lin = q_ref.shape[0]

    @pl.when(ck == 0)
    def _():
        s_sc[...] = s0_ref[...]

    q = _pad_rows(q_ref[...], n)
    k = _pad_rows(k_ref[...], n)
    v = _pad_rows(v_ref[...], n)
    zs = _pad_rows(zs_ref[...], n)
    a_pre = _lane_pick(zs + b_ref[...], SM_GA + h)
    neg_a = _lane_pick(a_ref[...], SM_GA + h)
    gdec = neg_a * (jnp.maximum(a_pre, 0.0) + jnp.log1p(jnp.exp(-jnp.abs(a_pre))))
    beta = jax.nn.sigmoid(_lane_pick(zs, SM_GB + h))
    if nvalid < n:
        live = _iota((n, 1), 0) < nvalid
        gdec = jnp.where(live, gdec, 0.0)
        beta = jnp.where(live, beta, 0.0)
    gc = _cumsum_col(gdec)
    incl = _iota((n, n), 0) >= _iota((n, n), 1)
    strict = _iota((n, n), 0) > _iota((n, n), 1)
    dec = jnp.where(incl, jnp.exp(jnp.where(incl, gc - _row_bcast(gc), 0.0)), 0.0)
    egc = jnp.exp(gc)
    kb = k * beta
    k16 = k.astype(BF16)
    a = jnp.where(strict, _dot_nt(kb.astype(BF16), k16) * dec, 0.0)
    eye = jnp.where(_iota((n, n), 0) == _iota((n, n), 1), 1.0, 0.0)
    xp = -a
    tinv = eye + xp
    for _ in range(int(math.log2(n)) - 1):
        xp = _dot(xp, xp, precision=HI)
        tinv = tinv + _dot(tinv, xp, precision=HI)
    rhs = jnp.concatenate([v * beta, kb * egc], axis=1)
    sol = _dot(tinv, rhs, precision=HI)
    s0 = s_sc[...]
    s016 = s0.astype(BF16)
    u = sol[:, :GD_D] - _dot(sol[:, GD_D:].astype(BF16), s016)
    q16 = q.astype(BF16)
    qk = _dot_nt(q16, k16) * dec
    o = _dot(qk.astype(BF16), u.astype(BF16)) + egc * _dot(q16, s016)
    g_last = gc[n - 1:n]
    kw = k * jnp.exp(g_last - gc)
    s_new = jnp.exp(g_last) * s0 + _dot(kw.T.astype(BF16), u.astype(BF16))
    s_sc[...] = s_new
    s_out[...] = s_new
    on = o * lax.rsqrt(jnp.mean(o * o, axis=-1, keepdims=True) + EPS) * ng_ref[...]
    o_ref[...] = (on * _silu(_pad_rows(gz_ref[...], n)))[:lin].astype(o_ref.dtype)


def gdn_scan(qkv, z, dt_row, nega_row, norm_g, s0, nb_batch, nqp, nvalid):
    lin = min(CHUNK, nqp)
    nc = nqp // lin
    body = functools.partial(_gdn_body, nvalid=min(nvalid, CHUNK))
    tokq = lambda part: pl.BlockSpec((lin, GD_D), lambda b, h, c: (b * nc + c, part * GD_H + h))
    st = pl.BlockSpec((None, None, GD_D, GD_D), lambda b, h, c: (b, h, 0, 0))
    row = pl.BlockSpec((1, LANE), lambda b, h, c: (0, 0))
    return pl.pallas_call(
        body, grid=(nb_batch, GD_H, nc),
        in_specs=[tokq(0), tokq(1), tokq(2),
                  pl.BlockSpec((lin, GD_D), lambda b, h, c: (b * nc + c, Z_GZ // GD_D + h)),
                  pl.BlockSpec((lin, LANE), lambda b, h, c: (b * nc + c, Z_SM // LANE)),
                  row, row, row, st],
        out_specs=[pl.BlockSpec((lin, GD_D), lambda b, h, c: (b * nc + c, h)), st],
        out_shape=[jax.ShapeDtypeStruct((nb_batch * nqp, GD_H * GD_D), BF16),
                   jax.ShapeDtypeStruct((nb_batch, GD_H, GD_D, GD_D), F32)],
        scratch_shapes=[pltpu.VMEM((GD_D, GD_D), F32)],
        compiler_params=_cp(("parallel", "parallel", "arbitrary")),
        name="gdn_scan")(qkv, qkv, qkv, z, z, dt_row, nega_row, norm_g.reshape(1, GD_D), s0)


HALO = 8


def _conv_rows(buf, x, hist_ref, w_ref, width, first):
    tb = x.shape[0]

    @pl.when(first)
    def _():
        buf[0:HALO, :] = hist_ref[...]

    buf[HALO:HALO + tb, :] = x
    y = w_ref[width - 1:width, :] * x
    for kk in range(1, width):
        y = y + w_ref[width - 1 - kk:width - kk, :] * buf[HALO - kk:HALO - kk + tb, :]
    return y


def _gdn_prep_body(z_ref, hist_ref, w_ref, o_ref, buf):
    ct = pl.program_id(1)
    tb = z_ref.shape[0]
    x = z_ref[...]
    y = _silu(_conv_rows(buf, x, hist_ref, w_ref, GDN_CONV, pl.program_id(2) == 0))
    buf[0:HALO, :] = x[tb - HALO:tb, :]
    part = ct // (GD_H * GD_D // y.shape[1])
    qscale = jnp.where(part == 0, GD_D ** -0.5, 1.0)
    for hh in range(y.shape[1] // GD_D):
        sl = slice(hh * GD_D, (hh + 1) * GD_D)
        yh = y[:, sl]
        nrm = yh * lax.rsqrt(jnp.sum(yh * yh, axis=-1, keepdims=True) + EPS) * qscale
        o_ref[:, sl] = jnp.where(part == 2, yh, nrm)


def gdn_prep(z, hist, w, nb_batch, nqp, tb, tc=512):
    nblk = nqp // tb
    ncol = 3 * GD_H * GD_D
    return pl.pallas_call(
        _gdn_prep_body, grid=(nb_batch, ncol // tc, nblk),
        in_specs=[pl.BlockSpec((tb, tc), lambda b, c, i: (b * nblk + i, Z_GQKV // tc + c)),
                  pl.BlockSpec((None, HALO, tc), lambda b, c, i: (b, 0, c)),
                  pl.BlockSpec((GDN_CONV, tc), lambda b, c, i: (0, c))],
        out_specs=pl.BlockSpec((tb, tc), lambda b, c, i: (b * nblk + i, c)),
        out_shape=jax.ShapeDtypeStruct((nb_batch * nqp, ncol), F32),
        scratch_shapes=[pltpu.VMEM((HALO + tb, tc), F32)],
        compiler_params=_cp(("parallel", "parallel", "arbitrary")), name="gdn_prep")(z, hist, w)


def _ffn_act_body(ua_ref, ub_ref, ha_ref, hb_ref, wa_ref, wb_ref, o_ref, bufa, bufb):
    first = pl.program_id(1) == 0
    tb = ua_ref.shape[0]
    xa = ua_ref[...]
    xb = ub_ref[...]
    ya = _conv_rows(bufa, xa, ha_ref, wa_ref, FFN_CONV, first)
    yb = _conv_rows(bufb, xb, hb_ref, wb_ref, FFN_CONV, first)
    bufa[0:HALO, :] = xa[tb - HALO:tb, :]
    bufb[0:HALO, :] = xb[tb - HALO:tb, :]
    o_ref[...] = (_silu(ya) * yb).astype(o_ref.dtype)


def ffn_act(u, hist, w, nb_batch, nqp, tb):
    nblk = nqp // tb
    f = u.shape[1] // 2
    tok = lambda half: pl.BlockSpec((tb, f), lambda b, i: (b * nblk + i, half))
    hs = lambda half: pl.BlockSpec((None, HALO, f), lambda b, i: (b, 0, half))
    ws = lambda half: pl.BlockSpec((FFN_CONV, f), lambda b, i: (0, half))
    return pl.pallas_call(
        _ffn_act_body, grid=(nb_batch, nblk),
        in_specs=[tok(0), tok(1), hs(0), hs(1), ws(0), ws(1)],
        out_specs=pl.BlockSpec((tb, f), lambda b, i: (b * nblk + i, 0)),
        out_shape=jax.ShapeDtypeStruct((nb_batch * nqp, f), BF16),
        scratch_shapes=[pltpu.VMEM((HALO + tb, f), F32), pltpu.VMEM((HALO + tb, f), F32)],
        compiler_params=_cp(("parallel", "arbitrary")), name="ffn_act")(u, u, hist, hist, w, w)


def _merge_body(oa_ref, ob_ref, oc_ref, w_ref, g0_ref, g1_ref, g2_ref, o_ref):
    acc = jax.nn.sigmoid(g0_ref[...]) * _dot(oa_ref[...], w_ref[0])
    acc = acc + jax.nn.sigmoid(g1_ref[...]) * _dot(ob_ref[...], w_ref[1])
    acc = acc + jax.nn.sigmoid(g2_ref[...]) * _dot(oc_ref[...], w_ref[2])
    o_ref[...] = acc.astype(o_ref.dtype)


def merge_branches(oa, ob, oc, wb, z, tm, tn):
    t = oa.shape[0]
    d = wb.shape[2]
    tok = pl.BlockSpec((tm, 1024), lambda i, j: (i, 0))
    gate = lambda nbr: pl.BlockSpec((tm, tn), lambda i, j: (i, (Z_BR + nbr * d) // tn + j))
    return pl.pallas_call(
        _merge_body, grid=(t // tm, d // tn),
        in_specs=[tok, tok, tok, pl.BlockSpec((3, 1024, tn), lambda i, j: (0, 0, j)), gate(0), gate(1), gate(2)],
        out_specs=pl.BlockSpec((tm, tn), lambda i, j: (i, j)),
        out_shape=jax.ShapeDtypeStruct((t, d), BF16),
        compiler_params=_cp(("parallel", "parallel")), name="merge_branches")(oa, ob, oc, wb, z, z, z)


def _layer_weights(p):
    w_in = p['w_in']
    offs = np.cumsum([0, 1024, 1536, 24, 512, 512, 1024, 4, 4, 1024, 3072, 8, 8, 1024, 6144])
    seg = lambda i: w_in[:, offs[i]:offs[i + 1]]
    kv = seg(1).reshape(-1, 3, 2, NSA_G * DH)
    small = jnp.concatenate([seg(2), seg(6), seg(7), seg(10), seg(11)], axis=1)
    small = jnp.pad(small, ((0, 0), (0, Z_COLS - Z_SM - small.shape[1])))
    w_all = jnp.concatenate([seg(0), kv[:, :, 0].reshape(-1, 768), kv[:, :, 1].reshape(-1, 768),
                             seg(3), seg(4), seg(5), seg(8), seg(9), seg(12), seg(13), small], axis=1)
    w1 = p['w_cmp1']
    wcat = jnp.concatenate([w1[:, :CMP_STRIDE].reshape(2, CMP_STRIDE * DH, DH),
                            w1[:, CMP_STRIDE:].reshape(2, CMP_STRIDE * DH, DH)], axis=2).astype(BF16)
    def lane_row(pairs):
        row = jnp.zeros((LANE,), F32)
        for off, v in pairs:
            row = row.at[off:off + v.shape[0]].set(v.astype(F32))
        return row.reshape(1, LANE)

    f = p['w_down'].shape[0]
    fpad = FFN_PAD - f
    pad_halves = lambda a: jnp.concatenate(
        [jnp.pad(a[..., :f], ((0, 0),) * (a.ndim - 1) + ((0, fpad),)),
         jnp.pad(a[..., f:], ((0, 0),) * (a.ndim - 1) + ((0, fpad),))], axis=-1)
    return dict(
        w_up=pad_halves(p['w_up']).astype(BF16), ffn_conv_w=pad_halves(p['ffn_conv_w']),
        w_down=jnp.pad(p['w_down'], ((0, fpad), (0, 0))).astype(BF16), pad_halves=pad_halves,
        w_all=w_all.astype(BF16), wcat=wcat,
        pe8=jnp.pad(p['cmp_pe'].reshape(2, 1, CMP_BLOCK * DH), ((0, 0), (0, 7), (0, 0))).astype(BF16),
        w1flat=w1.reshape(2, CMP_BLOCK * DH, DH).astype(BF16), w2=p['w_cmp2'].astype(BF16),
        ml_bias=lane_row([(SM_MI, p['mlstm_b_i']), (SM_MF, p['mlstm_b_f'])]),
        gd_dt=lane_row([(SM_GA, p['gdn_dt_bias'])]),
        gd_nega=lane_row([(SM_GA, -jnp.exp(p['gdn_a_log'].astype(F32)))]),
        w_branch=p['w_branch'].astype(BF16), w_out=p['w_out'].astype(BF16))


def _hist_rows(state):
    return jnp.pad(state, ((0, 0), (HALO - state.shape[1], 0), (0, 0)))


def _group_layer(x, grp, st, p, w, layer):
    nb, nq, nqp, pos0 = grp['nb'], grp['nq'], grp['nqp'], grp['pos0']
    t = nb * nqp
    big = nqp >= 512
    tm = 512 if big else t
    h = rmsnorm_bf16(x, p['norm1_g'], min(tm, 256))
    z = matmul(h, w['w_all'], tm, 1024 if big else 2048, name="in_proj")

    tb = 256 if big else nqp
    q_rot, rows, win_new = nsa_prep(z, grp['rope'], p['q_norm_g'], p['k_norm_g'], nqp, tb)
    rows3 = rows.reshape(nb, nqp, 1024)
    if pos0 == 0:
        n_pages = nqp // PAGE
        fs = compress_pages(rows.reshape(nb * n_pages, PAGE, 1024), w['wcat'], nb, n_pages, 8)
        newpage = None
    else:
        live = (jnp.arange(nqp) < nq)[None, :, None]
        newpage = jnp.pad(jnp.where(live, rows3, 0.0), ((0, 0), (0, PAGE - nqp), (0, 0)))
        fs_past = compress_pages(st['cache'], w['wcat'], nb, st['table'].shape[1], 8,
                                 table=st['table'], layer=layer)
        fs_new = compress_pages(newpage, w['wcat'], nb, 1, 1)
        fs = jnp.concatenate([fs_past, fs_new], axis=1)
    nblk_c = fs.shape[1]
    ncp = -(-nblk_c // LANE) * LANE
    fs = fs.reshape(nb, nblk_c, 4, 2, DH)
    first = jnp.pad(fs[:, :, :, 0].reshape(nb, nblk_c, 512), ((0, 0), (0, ncp - nblk_c), (0, 0)))
    second = jnp.pad(fs[:, 1:, :, 1].reshape(nb, nblk_c - 1, 512), ((0, 0), (0, ncp - nblk_c + 1), (0, 0)))
    kvc = compress_finish(first, second, w['pe8'], w['w1flat'], w['w2'])
    ns = nblk_c * CMP_STRIDE // SLC_BLOCK
    nsp = -(-ns // LANE) * LANE
    t_all = pos0 + nq
    nsel = min(N_SELECT, -(-t_all // SLC_BLOCK))
    qb = 64 if big else nqp
    o_cmp, sel = cmp_select(q_rot, kvc, importance_matrix(ncp, nsp), nb, nqp, qb, pos0, ns, nsel)
    emat = expand_matrices()
    if pos0 == 0:
        o_slc = slc_attention(q_rot, sel, emat, nb, nqp, qb, pos0, nsp, rows=rows3)
    else:
        o_slc = slc_attention(q_rot, sel, emat, nb, nqp, qb, pos0, nsp, cache=st['cache'], table=st['table'],
                              layer=layer, newpage=newpage)
    win3 = win_new.reshape(nb, nqp, 512)
    win_all = jnp.concatenate([st['win'], win3], axis=1)
    o_win = win_attention(q_rot, win_all, nb, nqp, qb, pos0)
    o_nsa = nsa_combine(o_cmp, o_slc, o_win, z, tb)
    win_state = jnp.concatenate([st['win'], win3[:, :nq]], axis=1)[:, -WINDOW:]

    o_ml, m_c, m_n, m_m = mlstm_scan(z, w['ml_bias'], p['mlstm_norm_g'], st['m_c'], st['m_n'], st['m_m'],
                                     nb, nqp, nq)

    tbc = 512 if big else nqp
    qkv = gdn_prep(z, _hist_rows(st['g_conv']), p['gdn_conv_w'], nb, nqp, tbc)
    o_gd, g_s = gdn_scan(qkv, z, w['gd_dt'], w['gd_nega'], p['gdn_norm_g'], st['g_s'], nb, nqp, nq)
    z3 = z.reshape(nb, nqp, Z_COLS)
    g_conv_new = z3[:, nq - (GDN_CONV - 1):nq, Z_GQKV:Z_GQKV + 3 * GD_H * GD_D]

    mrg = merge_branches(o_nsa, o_ml, o_gd, w['w_branch'], z, tm, 512)
    x1 = matmul(mrg, w['w_out'], tm, 512 if big else 2048, res=x, name="out_proj")

    h2 = rmsnorm_bf16(x1, p['norm2_g'], min(tm, 256))
    u = matmul(h2, w['w_up'], tm, 1024, name="up_proj")
    act = ffn_act(u, _hist_rows(w['pad_halves'](st['f_conv'])), w['ffn_conv_w'], nb, nqp, 128 if big else nqp)
    x2 = matmul(act, w['w_down'], tm, 512, res=x1, name="down_proj")
    f = p['w_down'].shape[0]
    u_last = u.reshape(nb, nqp, -1)[:, nq - (FFN_CONV - 1):nq]
    f_conv_new = jnp.concatenate([u_last[..., :f], u_last[..., FFN_PAD:FFN_PAD + f]], axis=-1)

    new_rows = rows3[:, :nq].reshape(nb, nq, 4, NSA_G, DH)
    states = (new_rows, win_state.reshape(nb, WINDOW, 2, NSA_G, DH), m_c, m_n.reshape(nb, ML_H, ML_DK),
              m_m.reshape(nb, ML_H), g_conv_new, g_s, f_conv_new)
    return x2, states


def kernel(x_prompt, x_sample, cache_nsa_kv, page_table, cache_nsa_win, state_mlstm_c, state_mlstm_n,
           state_mlstm_m, state_gdn_conv, state_gdn_s, state_ffn_conv, norm1_g, w_in, q_norm_g, k_norm_g,
           w_cmp1, cmp_pe, w_cmp2, mlstm_b_i, mlstm_b_f, mlstm_norm_g, gdn_conv_w, gdn_a_log, gdn_dt_bias,
           gdn_norm_g, w_branch, w_out, norm2_g, w_up, ffn_conv_w, w_down):
    depth = w_in.shape[0]
    bp, sp, d = x_prompt.shape
    bs, ss, _ = x_sample.shape
    n_pages = page_table.shape[1]
    past = n_pages * PAGE
    ssp = -(-ss // 16) * 16
    assert sp % KC == 0 and ss >= GDN_CONV - 1 and cache_nsa_win.shape[2] == WINDOW and n_pages % (KC // PAGE) == 0
    grp_p = dict(nb=bp, nq=sp, nqp=sp, pos0=0, rope=rope_tables(0, sp, sp))
    grp_s = dict(nb=bs, nq=ss, nqp=ssp, pos0=past, rope=rope_tables(past, ss, ssp))
    cache = cache_nsa_kv.reshape(depth, cache_nsa_kv.shape[1], PAGE, 4 * NSA_G * DH)
    xp = x_prompt.reshape(bp * sp, d)
    xs = jnp.pad(x_sample, ((0, 0), (0, ssp - ss), (0, 0))).reshape(bs * ssp, d)
    st_p, st_s = [], []
    for l in range(depth):
        p = dict(norm1_g=norm1_g[l], w_in=w_in[l], q_norm_g=q_norm_g[l], k_norm_g=k_norm_g[l],
                 w_cmp1=w_cmp1[l], cmp_pe=cmp_pe[l], w_cmp2=w_cmp2[l], mlstm_b_i=mlstm_b_i[l],
                 mlstm_b_f=mlstm_b_f[l], mlstm_norm_g=mlstm_norm_g[l], gdn_conv_w=gdn_conv_w[l],
                 gdn_a_log=gdn_a_log[l], gdn_dt_bias=gdn_dt_bias[l], gdn_norm_g=gdn_norm_g[l],
                 w_branch=w_branch[l], w_out=w_out[l], norm2_g=norm2_g[l], w_up=w_up[l],
                 ffn_conv_w=ffn_conv_w[l], w_down=w_down[l])
        w = _layer_weights(p)
        zero_p = dict(win=jnp.zeros((bp, WINDOW, 512), F32),
                      m_c=jnp.zeros((bp, ML_H, ML_DK, ML_DV), F32), m_n=jnp.zeros((bp, ML_H, 1, ML_DK), F32),
                      m_m=jnp.zeros((bp, ML_H, 1, 1), F32),
                      g_conv=jnp.zeros((bp, GDN_CONV - 1, 3 * GD_H * GD_D), F32),
                      g_s=jnp.zeros((bp, GD_H, GD_D, GD_D), F32),
                      f_conv=jnp.zeros((bp, FFN_CONV - 1, state_ffn_conv.shape[-1]), F32))
        xp, s_p = _group_layer(xp, grp_p, zero_p, p, w, l)
        st = dict(cache=cache, table=page_table, win=cache_nsa_win[l].reshape(bs, WINDOW, 512),
                  m_c=state_mlstm_c[l], m_n=state_mlstm_n[l].reshape(bs, ML_H, 1, ML_DK),
                  m_m=state_mlstm_m[l].reshape(bs, ML_H, 1, 1), g_conv=state_gdn_conv[l],
                  g_s=state_gdn_s[l], f_conv=state_ffn_conv[l])
        xs, s_s = _group_layer(xs, grp_s, st, p, w, l)
        st_p.append(s_p)
        st_s.append(s_s)
    outs = [xp.reshape(bp, sp, d), xs.reshape(bs, ssp, d)[:, :ss]]
    for i in range(8):
        outs.append(jnp.stack([s[i] for s in st_p], axis=0))
        outs.append(jnp.stack([s[i] for s in st_s], axis=0))
    return tuple(outs)
```

```python
import functools
import math

import numpy as np
import jax
import jax.numpy as jnp
from jax import lax
from jax.experimental import pallas as pl
from jax.experimental.pallas import tpu as pltpu

F32 = jnp.float32
BF16 = jnp.bfloat16
HI = lax.Precision.HIGHEST

EPS = 1e-6
NEG_INF = -1e30
LANE = 128
PAGE = 128
DH = 128
NSA_HEADS = 8
NSA_G = 2
NSA_GROUP = NSA_HEADS // NSA_G
ROPE_DIM = DH // 4
ROPE_THETA = 500000.0
CMP_BLOCK = 32
CMP_STRIDE = 16
SLC_BLOCK = 64
N_SELECT = 16
WINDOW = 512
ML_H, ML_DK, ML_DV = 4, 128, 256
GD_H, GD_D = 8, 128
GDN_CONV = 4
FFN_CONV = 3
CHUNK = 64
FFN_PAD = 5632
KC = 1024
BLK_PER_KC = KC // SLC_BLOCK
WIN_PER_LANE = LANE // BLK_PER_KC
VMEM_LIMIT = 56 * 1024 * 1024

Z_Q, Z_MV, Z_MO, Z_GZ = 0, 1024, 2048, 3072
Z_GQKV, Z_BR = 4096, 7168
Z_MQ, Z_KV, Z_MK, Z_SM = 13312, 13824, 15360, 15872
Z_COLS = 16384
KV_W = 1536
ROW_W = 4 * NSA_G * DH
ROW_SUB = ROW_W // LANE
PAGE_SUB = PAGE * ROW_SUB
SM_NG, SM_MI, SM_MF, SM_GA, SM_GB = 0, 24, 28, 32, 40


def _cp(sem):
    return pltpu.CompilerParams(dimension_semantics=sem, vmem_limit_bytes=VMEM_LIMIT)


def _iota(shape, dim):
    return lax.broadcasted_iota(jnp.int32, shape, dim)


def _dot(a, b, **kw):
    return jnp.dot(a, b, preferred_element_type=F32, **kw)


def _dot_nt(a, b, **kw):
    return lax.dot_general(a, b, (((1,), (1,)), ((), ())), preferred_element_type=F32, **kw)


def _silu(x):
    return x * jax.nn.sigmoid(x)


def _split(x):
    hi = x.astype(BF16)
    return hi, (x - hi.astype(F32)).astype(BF16)


def _dot3(a, b):
    ah, al = _split(a)
    bh, bl = _split(b)
    return _dot(ah, bh) + (_dot(ah, bl) + _dot(al, bh))


def _rmsnorm_body(x_ref, g_ref, o_ref):
    x = x_ref[...]
    y = x * lax.rsqrt(jnp.mean(x * x, axis=-1, keepdims=True) + EPS)
    o_ref[...] = (y * g_ref[...]).astype(o_ref.dtype)


def rmsnorm_bf16(x, g, tm):
    t, d = x.shape
    return pl.pallas_call(
        _rmsnorm_body, grid=(t // tm,),
        in_specs=[pl.BlockSpec((tm, d), lambda i: (i, 0)), pl.BlockSpec((1, d), lambda i: (0, 0))],
        out_specs=pl.BlockSpec((tm, d), lambda i: (i, 0)),
        out_shape=jax.ShapeDtypeStruct((t, d), BF16),
        compiler_params=_cp(("parallel",)), name="rmsnorm")(x, g.reshape(1, d))


def _mm_body(a_ref, w_ref, o_ref):
    o_ref[...] = _dot(a_ref[...], w_ref[...]).astype(o_ref.dtype)


def _mm_res_body(a_ref, w_ref, r_ref, o_ref):
    o_ref[...] = r_ref[...] + _dot(a_ref[...], w_ref[...])


def matmul(a, w, tm, tn, res=None, name="matmul"):
    m, k = a.shape
    n = w.shape[1]
    in_specs = [pl.BlockSpec((tm, k), lambda i, j: (i, 0)), pl.BlockSpec((k, tn), lambda i, j: (0, j))]
    args = [a, w]
    body = _mm_body
    if res is not None:
        in_specs.append(pl.BlockSpec((tm, tn), lambda i, j: (i, j)))
        args.append(res)
        body = _mm_res_body
    return pl.pallas_call(
        body, grid=(m // tm, n // tn), in_specs=in_specs,
        out_specs=pl.BlockSpec((tm, tn), lambda i, j: (i, j)),
        out_shape=jax.ShapeDtypeStruct((m, n), F32),
        compiler_params=_cp(("parallel", "parallel")), name=name)(*args)


def _nsa_prep_body(zq_ref, zkv_ref, c_ref, s1_ref, s2_ref, qg_ref, kg_ref, q_out, rows_out, kvs_out, win_out):
    c = c_ref[...]
    s1 = s1_ref[...]
    s2 = s2_ref[...]
    tb = zq_ref.shape[0]

    def norm_rope(x, g):
        y = x * lax.rsqrt(jnp.mean(x * x, axis=-1, keepdims=True) + EPS) * g
        return y * c + pltpu.roll(y, LANE - ROPE_DIM // 2, 1) * s1 + pltpu.roll(y, ROPE_DIM // 2, 1) * s2

    for h in range(NSA_HEADS):
        sl = slice(h * DH, (h + 1) * DH)
        q_out[:, sl] = norm_rope(zq_ref[:, sl], qg_ref[...]).astype(q_out.dtype)
    for br in range(3):
        for g in range(NSA_G):
            col = (br * NSA_G + g) * DH
            kk = norm_rope(zkv_ref[:, col:col + DH], kg_ref[br:br + 1, :])
            vv = zkv_ref[:, KV_W // 2 + col:KV_W // 2 + col + DH]
            if br < 2:
                rows_out[pl.ds(4 * br + g, tb, stride=ROW_SUB), :] = kk
                rows_out[pl.ds(4 * br + 2 + g, tb, stride=ROW_SUB), :] = vv
            else:
                win_out[:, g * DH:(g + 1) * DH] = kk
                win_out[:, 2 * DH + g * DH:2 * DH + (g + 1) * DH] = vv
            if br == 1:
                kvs_out[:, g * DH:(g + 1) * DH] = kk.astype(kvs_out.dtype)
                kvs_out[:, (2 + g) * DH:(3 + g) * DH] = vv.astype(kvs_out.dtype)


def nsa_prep(z, tabs, q_g, k_g, nqp, tb):
    t = z.shape[0]
    nblk = nqp // tb
    tab_spec = pl.BlockSpec((tb, LANE), lambda i: (i % nblk, 0))
    return pl.pallas_call(
        _nsa_prep_body, grid=(t // tb,),
        in_specs=[pl.BlockSpec((tb, 1024), lambda i: (i, Z_Q // 1024)),
                  pl.BlockSpec((tb, KV_W), lambda i: (i, Z_KV // KV_W)), tab_spec, tab_spec, tab_spec,
                  pl.BlockSpec((1, DH), lambda i: (0, 0)), pl.BlockSpec((3, DH), lambda i: (0, 0))],
        out_specs=[pl.BlockSpec((tb, 1024), lambda i: (i, 0)), pl.BlockSpec((tb * ROW_SUB, LANE), lambda i: (i, 0)),
                   pl.BlockSpec((tb, 512), lambda i: (i, 0)), pl.BlockSpec((tb, 512), lambda i: (i, 0))],
        out_shape=[jax.ShapeDtypeStruct((t, 1024), BF16), jax.ShapeDtypeStruct((t * ROW_SUB, LANE), F32),
                   jax.ShapeDtypeStruct((t, 512), BF16), jax.ShapeDtypeStruct((t, 512), F32)],
        compiler_params=_cp(("parallel",)), name="nsa_prep")(z, z, *tabs, q_g.reshape(1, DH), k_g)


def rope_tables(pos0, nqp):
    half = ROPE_DIM // 2
    pos = pos0 + jnp.arange(nqp, dtype=jnp.int32)
    inv = ROPE_THETA ** (-jnp.arange(half, dtype=F32) / half)
    ang = pos.astype(F32)[:, None] * inv[None, :]
    cos, sin = jnp.cos(ang), jnp.sin(ang)
    zeros = jnp.zeros((nqp, LANE - ROPE_DIM), F32)
    c = jnp.concatenate([cos, cos, jnp.ones((nqp, LANE - ROPE_DIM), F32)], axis=1)
    s1 = jnp.concatenate([-sin, jnp.zeros((nqp, half), F32), zeros], axis=1)
    s2 = jnp.concatenate([jnp.zeros((nqp, half), F32), sin, zeros], axis=1)
    return c, s1, s2


def _compress_body(*refs, pps, paged):
    if paged:
        refs = refs[1:]
    slots = refs[:pps]
    w_refs = refs[pps:pps + 2]
    o_ref = refs[pps + 2]
    nb = PAGE // CMP_STRIDE
    for kind in range(2):
        xs = []
        for g in range(NSA_G):
            part = kind * 2 + g
            per_slot = [jnp.concatenate([s[pl.ds(ROW_SUB * j + part, nb, stride=CMP_STRIDE * ROW_SUB), :]
                                         for j in range(CMP_STRIDE)], axis=1) for s in slots]
            xs.append(jnp.concatenate(per_slot, axis=0) if pps > 1 else per_slot[0])
        x = jnp.concatenate(xs, axis=0).astype(BF16)
        fs = _dot(x, w_refs[kind][...])
        for g in range(NSA_G):
            o_ref[:, (kind * 2 + g) * 2 * DH:(kind * 2 + g + 1) * 2 * DH] = fs[g * nb * pps:(g + 1) * nb * pps]


def compress_pages(src, wcat, nb_batch, n_pages, pps, table=None, layer=None):
    nb = PAGE // CMP_STRIDE
    steps = n_pages // pps
    w_spec = pl.BlockSpec((CMP_STRIDE * DH, 2 * DH), lambda b, i, *_: (0, 0))
    out_spec = pl.BlockSpec((None, nb * pps, 1024), lambda b, i, *_: (b, i, 0))
    out_shape = jax.ShapeDtypeStruct((nb_batch, n_pages * nb, 1024), F32)
    body = functools.partial(_compress_body, pps=pps, paged=table is not None)
    if table is None:
        slot_specs = [pl.BlockSpec((None, PAGE_SUB, LANE), functools.partial(
            lambda b, i, s: (b * n_pages + i * pps + s, 0, 0), s=s)) for s in range(pps)]
        return pl.pallas_call(
            body, grid=(nb_batch, steps), in_specs=slot_specs + [w_spec, w_spec], out_specs=out_spec,
            out_shape=out_shape, compiler_params=_cp(("parallel", "parallel")),
            name="compress")(*([src] * pps), wcat[0], wcat[1])
    slot_specs = [pl.BlockSpec((None, None, PAGE_SUB, LANE), functools.partial(
        lambda b, i, tbl, s: (layer, tbl[b, i * pps + s], 0, 0), s=s)) for s in range(pps)]
    gs = pltpu.PrefetchScalarGridSpec(num_scalar_prefetch=1, grid=(nb_batch, steps),
                                      in_specs=slot_specs + [w_spec, w_spec], out_specs=out_spec)
    return pl.pallas_call(body, grid_spec=gs, out_shape=out_shape,
                          compiler_params=_cp(("parallel", "parallel")),
                          name="compress_paged")(table, *([src] * pps), wcat[0], wcat[1])


def _cfin_body(f_ref, s_ref, pe_ref, w1_ref, w2_ref, o_ref):
    for kind in range(2):
        bias = _dot(pe_ref[kind], w1_ref[kind])[0:1, :]
        for g in range(NSA_G):
            sl = slice((kind * 2 + g) * DH, (kind * 2 + g + 1) * DH)
            hid = f_ref[:, sl] + s_ref[:, sl] + bias
            o_ref[:, sl] = _dot(_silu(hid).astype(BF16), w2_ref[kind]).astype(o_ref.dtype)


def compress_finish(first, second_next, pe8, w1flat, w2):
    b, ncp, _ = first.shape
    spec = pl.BlockSpec((None, ncp, 512), lambda i: (i, 0, 0))
    return pl.pallas_call(
        _cfin_body, grid=(b,),
        in_specs=[spec, spec, pl.BlockSpec(pe8.shape, lambda i: (0, 0, 0)),
                  pl.BlockSpec(w1flat.shape, lambda i: (0, 0, 0)), pl.BlockSpec(w2.shape, lambda i: (0, 0, 0))],
        out_specs=spec, out_shape=jax.ShapeDtypeStruct((b, ncp, 512), BF16),
        compiler_params=_cp(("parallel",)), name="compress_finish")(first, second_next, pe8, w1flat, w2)


def _cmp_body(q_ref, kv_ref, m_ref, o_ref, sel_ref, *, qb, pos0, ncp, ns, nsp, nsel):
    i = pl.program_id(1)
    scale = DH ** -0.5
    rows = NSA_GROUP * qb
    t4 = pos0 + i * qb + (_iota((rows, 1), 0) & (qb - 1))
    vis = (_iota((1, ncp), 1) * CMP_STRIDE + (CMP_BLOCK - 1)) <= t4
    t1 = pos0 + i * qb + _iota((qb, 1), 0)
    blk = _iota((1, nsp), 1)
    cur = t1 // SLC_BLOCK
    valid = blk * SLC_BLOCK <= t1
    forced = (blk == 0) | (blk == cur) | (blk == cur - 1)
    for g in range(NSA_G):
        qg = jnp.concatenate([q_ref[:, (g * NSA_GROUP + h) * DH:(g * NSA_GROUP + h + 1) * DH]
                              for h in range(NSA_GROUP)], axis=0)
        s = _dot_nt(qg, kv_ref[:, g * DH:(g + 1) * DH]) * scale
        s = jnp.where(vis, s, NEG_INF)
        e = jnp.exp(s - jnp.max(s, axis=-1, keepdims=True))
        pc = jnp.where(vis, e / jnp.sum(e, axis=-1, keepdims=True), 0.0)
        o = _dot(pc.astype(BF16), kv_ref[:, (2 + g) * DH:(3 + g) * DH])
        for h in range(NSA_GROUP):
            o_ref[:, (g * NSA_GROUP + h) * DH:(g * NSA_GROUP + h + 1) * DH] = o[h * qb:(h + 1) * qb]
        pcs = (pc[0:qb] + pc[qb:2 * qb]) + (pc[2 * qb:3 * qb] + pc[3 * qb:4 * qb])
        imp = _dot(pcs, m_ref[...], precision=HI)
        score = jnp.where(valid, jnp.where(forced, jnp.inf, imp), -jnp.inf)
        rank = jnp.zeros((qb, nsp), F32)
        for j in range(ns):
            sj = score[:, j:j + 1]
            ahead = (sj > score) | ((sj == score) & (blk > j))
            rank = rank + jnp.where(ahead, 1.0, 0.0)
        sel_ref[:, g * nsp:(g + 1) * nsp] = jnp.where(rank < nsel, 1.0, 0.0).astype(sel_ref.dtype)


def cmp_select(q, kvc, mmat, nb_batch, nqp, qb, pos0, ns, nsel):
    ncp = kvc.shape[1]
    nsp = mmat.shape[1]
    nblk = nqp // qb
    body = functools.partial(_cmp_body, qb=qb, pos0=pos0, ncp=ncp, ns=ns, nsp=nsp, nsel=nsel)
    return pl.pallas_call(
        body, grid=(nb_batch, nblk),
        in_specs=[pl.BlockSpec((qb, 1024), lambda b, i: (b * nblk + i, 0)),
                  pl.BlockSpec((None, ncp, 512), lambda b, i: (b, 0, 0)),
                  pl.BlockSpec((ncp, nsp), lambda b, i: (0, 0))],
        out_specs=[pl.BlockSpec((qb, 1024), lambda b, i: (b * nblk + i, 0)),
                   pl.BlockSpec((qb, NSA_G * nsp), lambda b, i: (b * nblk + i, 0))],
        out_shape=[jax.ShapeDtypeStruct((nb_batch * nqp, 1024), F32),
                   jax.ShapeDtypeStruct((nb_batch * nqp, NSA_G * nsp), BF16)],
        compiler_params=_cp(("parallel", "parallel")), name="cmp_select")(q, kvc, mmat)


def importance_matrix(ncp, nsp):
    m = np.zeros((ncp, nsp), np.float32)
    for j in range(nsp):
        for n, w in ((4 * j - 1, 1.0), (4 * j, 2.0), (4 * j + 1, 2.0), (4 * j + 2, 2.0), (4 * j + 3, 1.0)):
            if 0 <= n < ncp:
                m[n, j] = w
    return jnp.asarray(m)


def expand_matrices():
    e = np.zeros((WIN_PER_LANE, LANE, KC), np.float32)
    for w in range(WIN_PER_LANE):
        for key in range(KC):
            e[w, BLK_PER_KC * w + key // SLC_BLOCK, key] = 1.0
    return jnp.asarray(e, BF16)


def _slc_body(*refs, qb, pos0, nslot, paged, causal_skip):
    if paged:
        refs = refs[1:]
    q_ref, sel0_ref, sel1_ref, e_ref = refs[:4]
    slots = refs[4:4 + nslot]
    k = 4 + nslot
    new_ref = refs[k] if paged else None
    k += 1 if paged else 0
    o_ref, m_sc, l_sc, acc_sc = refs[k:k + 4]
    i = pl.program_id(1)
    c = pl.program_id(2)
    nchunk = pl.num_programs(2)
    scale = DH ** -0.5
    rows = NSA_GROUP * qb

    @pl.when(c == 0)
    def _():
        m_sc[...] = jnp.full(m_sc.shape, NEG_INF, F32)
        l_sc[...] = jnp.zeros(l_sc.shape, F32)
        acc_sc[...] = jnp.zeros(acc_sc.shape, F32)

    def keys_values(g):
        if not paged:
            kv = slots[0]
            return kv[:, g * DH:(g + 1) * DH], kv[:, (2 + g) * DH:(3 + g) * DH]
        out = []
        for part in (4 + g, 6 + g):
            pieces = [s[pl.ds(part, PAGE, stride=ROW_SUB), :] for s in slots]
            new = new_ref[pl.ds(part, PAGE, stride=ROW_SUB), :]
            pieces[0] = jnp.where(c == nchunk - 1, new, pieces[0])
            out.append(jnp.concatenate(pieces, axis=0).astype(BF16))
        return out

    def step():
        t4 = pos0 + i * qb + (_iota((rows, 1), 0) & (qb - 1))
        kpos = c * KC + _iota((1, KC), 1)
        causal = kpos <= t4
        ew = e_ref[c % WIN_PER_LANE]
        for g, sel_ref in ((0, sel0_ref), (1, sel1_ref)):
            kk, vv = keys_values(g)
            sel1 = _dot(sel_ref[...], ew)
            selx = jnp.concatenate([sel1] * NSA_GROUP, axis=0)
            mask = causal & (selx > 0.5)
            qg = jnp.concatenate([q_ref[:, (g * NSA_GROUP + h) * DH:(g * NSA_GROUP + h + 1) * DH]
                                  for h in range(NSA_GROUP)], axis=0)
            s = _dot_nt(qg, kk) * scale
            s = jnp.where(mask, s, NEG_INF)
            m_old = m_sc[g]
            m_new = jnp.maximum(m_old, jnp.max(s, axis=-1, keepdims=True))
            alpha = jnp.exp(m_old - m_new)
            p = jnp.where(mask, jnp.exp(s - m_new), 0.0)
            l_sc[g] = alpha * l_sc[g] + jnp.sum(p, axis=-1, keepdims=True)
            acc_sc[g] = alpha * acc_sc[g] + _dot(p.astype(BF16), vv)
            m_sc[g] = m_new

    if causal_skip:
        pl.when(c * KC <= pos0 + i * qb + qb - 1)(step)
    else:
        step()

    @pl.when(c == nchunk - 1)
    def _():
        for g in range(NSA_G):
            o = acc_sc[g] / l_sc[g]
            for h in range(NSA_GROUP):
                o_ref[:, (g * NSA_GROUP + h) * DH:(g * NSA_GROUP + h + 1) * DH] = o[h * qb:(h + 1) * qb]


def slc_attention(q, sel, emat, nb_batch, nqp, qb, pos0, nsp, kvs=None, cache=None, table=None,
                  layer=None, newpage=None):
    nblk = nqp // qb
    lanes_g = nsp // LANE
    paged = cache is not None
    scratch = [pltpu.VMEM((NSA_G, NSA_GROUP * qb, 1), F32), pltpu.VMEM((NSA_G, NSA_GROUP * qb, 1), F32),
               pltpu.VMEM((NSA_G, NSA_GROUP * qb, DH), F32)]
    out_shape = jax.ShapeDtypeStruct((nb_batch * nqp, 1024), F32)
    q_map = lambda b, i, c, *_: (b * nblk + i, 0)
    e_spec = pl.BlockSpec((WIN_PER_LANE, LANE, KC), lambda b, i, c, *_: (0, 0, 0))
    if not paged:
        nchunk = nqp // KC
        last = lambda i: (pos0 + i * qb + qb - 1) // KC
        sel_specs = [pl.BlockSpec((qb, LANE), functools.partial(
            lambda b, i, c, g: (b * nblk + i, g * lanes_g + jnp.minimum(c, last(i)) // WIN_PER_LANE), g=g))
            for g in range(NSA_G)]
        kv_spec = pl.BlockSpec((KC, 512), lambda b, i, c: (b * nchunk + jnp.minimum(c, last(i)), 0))
        body = functools.partial(_slc_body, qb=qb, pos0=pos0, nslot=1, paged=False, causal_skip=True)
        return pl.pallas_call(
            body, grid=(nb_batch, nblk, nchunk),
            in_specs=[pl.BlockSpec((qb, 1024), q_map)] + sel_specs + [e_spec, kv_spec],
            out_specs=pl.BlockSpec((qb, 1024), q_map), out_shape=out_shape, scratch_shapes=scratch,
            compiler_params=_cp(("parallel", "parallel", "arbitrary")), name="slc_attention")(q, sel, sel, emat, kvs)
    n_pages = table.shape[1]
    nslot = KC // PAGE
    nchunk = n_pages // nslot + 1
    sel_specs = [pl.BlockSpec((qb, LANE), functools.partial(
        lambda b, i, c, tbl, g: (b * nblk + i, g * lanes_g + c // WIN_PER_LANE), g=g)) for g in range(NSA_G)]
    slot_specs = [pl.BlockSpec((None, None, PAGE_SUB, LANE), functools.partial(
        lambda b, i, c, tbl, s: (layer, tbl[b, jnp.minimum(c * nslot + s, n_pages - 1)], 0, 0), s=s))
        for s in range(nslot)]
    new_spec = pl.BlockSpec((None, PAGE_SUB, LANE), lambda b, i, c, tbl: (b, 0, 0))
    body = functools.partial(_slc_body, qb=qb, pos0=pos0, nslot=nslot, paged=True, causal_skip=False)
    gs = pltpu.PrefetchScalarGridSpec(
        num_scalar_prefetch=1, grid=(nb_batch, nblk, nchunk),
        in_specs=[pl.BlockSpec((qb, 1024), q_map)] + sel_specs + [e_spec] + slot_specs + [new_spec],
        out_specs=pl.BlockSpec((qb, 1024), q_map), scratch_shapes=scratch)
    return pl.pallas_call(body, grid_spec=gs, out_shape=out_shape,
                          compiler_params=_cp(("parallel", "parallel", "arbitrary")),
                          name="slc_attention_paged")(table, q, sel, sel, emat, *([cache] * nslot), newpage)


def _win_body(q_ref, w_ref, o_ref, *, qb, pos0, span):
    i = pl.program_id(1)
    scale = DH ** -0.5
    rows = NSA_GROUP * qb
    kv = w_ref[pl.ds(pl.multiple_of(i * qb, qb), span), :].astype(BF16)
    qpos = pos0 + i * qb + (_iota((rows, 1), 0) & (qb - 1))
    kpos = pos0 - WINDOW + i * qb + _iota((1, span), 1)
    rel = qpos - kpos
    mask = (rel >= 0) & (rel <= WINDOW) & (kpos >= 0)
    for g in range(NSA_G):
        qg = jnp.concatenate([q_ref[:, (g * NSA_GROUP + h) * DH:(g * NSA_GROUP + h + 1) * DH]
                              for h in range(NSA_GROUP)], axis=0)
        s = jnp.where(mask, _dot_nt(qg, kv[:, g * DH:(g + 1) * DH]) * scale, NEG_INF)
        e = jnp.exp(s - jnp.max(s, axis=-1, keepdims=True))
        p = e / jnp.sum(e, axis=-1, keepdims=True)
        o = _dot(p.astype(BF16), kv[:, (2 + g) * DH:(3 + g) * DH])
        for h in range(NSA_GROUP):
            o_ref[:, (g * NSA_GROUP + h) * DH:(g * NSA_GROUP + h + 1) * DH] = o[h * qb:(h + 1) * qb]


def win_attention(q, win_all, nb_batch, nqp, qb, pos0):
    nblk = nqp // qb
    wtot = win_all.shape[1]
    body = functools.partial(_win_body, qb=qb, pos0=pos0, span=WINDOW + qb)
    return pl.pallas_call(
        body, grid=(nb_batch, nblk),
        in_specs=[pl.BlockSpec((qb, 1024), lambda b, i: (b * nblk + i, 0)),
                  pl.BlockSpec((None, wtot, 512), lambda b, i: (b, 0, 0))],
        out_specs=pl.BlockSpec((qb, 1024), lambda b, i: (b * nblk + i, 0)),
        out_shape=jax.ShapeDtypeStruct((nb_batch * nqp, 1024), F32),
        compiler_params=_cp(("parallel", "parallel")), name="win_attention")(q, win_all)


def _combine_body(oc_ref, os_ref, ow_ref, zs_ref, o_ref):
    gate = jax.nn.sigmoid(zs_ref[...])
    for h in range(NSA_HEADS):
        sl = slice(h * DH, (h + 1) * DH)
        c0 = SM_NG + 3 * h
        o = (gate[:, c0:c0 + 1] * oc_ref[:, sl] + gate[:, c0 + 1:c0 + 2] * os_ref[:, sl]
             + gate[:, c0 + 2:c0 + 3] * ow_ref[:, sl])
        o_ref[:, sl] = o.astype(o_ref.dtype)


def nsa_combine(oc, osl, ow, z, tb):
    t = oc.shape[0]
    spec = pl.BlockSpec((tb, 1024), lambda i: (i, 0))
    return pl.pallas_call(
        _combine_body, grid=(t // tb,),
        in_specs=[spec, spec, spec, pl.BlockSpec((tb, LANE), lambda i: (i, Z_SM // LANE))],
        out_specs=spec, out_shape=jax.ShapeDtypeStruct((t, 1024), BF16),
        compiler_params=_cp(("parallel",)), name="nsa_combine")(oc, osl, ow, z)


def _pad_rows(x, n):
    if x.shape[0] == n:
        return x
    return jnp.concatenate([x, jnp.zeros((n - x.shape[0], x.shape[1]), x.dtype)], axis=0)


def _cumsum_rows(x):
    n = x.shape[0]
    tri = jnp.where(_iota((n, n), 0) >= _iota((n, n), 1), 1.0, 0.0).astype(BF16)
    hi, lo = _split(x)
    lo2 = (x - hi.astype(F32) - lo.astype(F32)).astype(BF16)
    return _dot(tri, hi) + (_dot(tri, lo) + _dot(tri, lo2))


def _log_sigmoid(x):
    return jnp.minimum(x, 0.0) - jnp.log1p(jnp.exp(-jnp.abs(x)))


def _softplus(x):
    return jnp.maximum(x, 0.0) + jnp.log1p(jnp.exp(-jnp.abs(x)))


def _mlstm_body(q_ref, k_ref, v_ref, og_ref, zs_ref, b_ref, ng_ref, c0_ref, n0_ref, m0_ref,
                o_ref, c_out, n_out, m_out, c_sc, n_sc, m_sc, *, nvalid):
    ck = pl.program_id(1)
    n = CHUNK
    lin = q_ref.shape[0]

    @pl.when(ck == 0)
    def _():
        c_sc[...] = c0_ref[...]
        n_sc[...] = n0_ref[...]
        m_sc[...] = m0_ref[...]

    gates = _pad_rows(zs_ref[...], n) + b_ref[...]
    lane = _iota((1, LANE), 1)
    is_f = (lane >= SM_MF) & (lane < SM_MF + ML_H)
    lf_all = jnp.where(is_f, _log_sigmoid(gates), 0.0)
    ig_all = gates
    if nvalid < n:
        live = _iota((n, 1), 0) < nvalid
        lf_all = jnp.where(live, lf_all, 0.0)
        ig_all = jnp.where(live, gates, NEG_INF)
    comb = jnp.where(is_f, _cumsum_rows(lf_all), ig_all)
    comb_t = comb.T
    incl = _iota((n, n), 0) >= _iota((n, n), 1)
    hs = range(ML_H)
    sk = [slice(h * ML_DK, (h + 1) * ML_DK) for h in hs]
    sv = [slice(h * ML_DV, (h + 1) * ML_DV) for h in hs]
    c0 = [c_sc[h] for h in hs]
    n0 = [n_sc[h] for h in hs]
    m0 = [m_sc[h] for h in hs]
    q = [_pad_rows(q_ref[:, sk[h]], n) for h in hs]
    k = [_pad_rows(k_ref[:, sk[h]], n) * (ML_DK ** -0.5) for h in hs]
    q16 = [q[h].astype(BF16) for h in hs]
    v16 = [_pad_rows(v_ref[:, sv[h]], n).astype(BF16) for h in hs]
    ig = [comb[:, SM_MI + h:SM_MI + h + 1] for h in hs]
    bc = [comb[:, SM_MF + h:SM_MF + h + 1] for h in hs]
    dlog = [jnp.where(incl, bc[h] + (comb_t[SM_MI + h:SM_MI + h + 1, :] - comb_t[SM_MF + h:SM_MF + h + 1, :]),
                      -jnp.inf) for h in hs]
    qk = [_dot_nt(q16[h], k[h].astype(BF16)) for h in hs]
    qc = [_dot(q16[h], c0[h].astype(BF16)) for h in hs]
    inter = [bc[h] + m0[h] for h in hs]
    m_t = [jnp.maximum(inter[h], jnp.max(dlog[h], axis=1, keepdims=True)) for h in hs]
    a = [qk[h] * jnp.exp(dlog[h] - m_t[h]) for h in hs]
    w_inter = [jnp.exp(inter[h] - m_t[h]) for h in hs]
    num = [_dot(a[h].astype(BF16), v16[h]) + w_inter[h] * qc[h] for h in hs]
    den = [jnp.sum(a[h], axis=1, keepdims=True) + w_inter[h] * jnp.sum(q[h] * n0[h], axis=1, keepdims=True)
           for h in hs]
    hh = [num[h] / jnp.maximum(jnp.abs(den[h]), jnp.exp(-m_t[h])) for h in hs]
    m_new = [m_t[h][n - 1:n] for h in hs]
    kw = [k[h] * jnp.exp(bc[h][n - 1:n] - bc[h] + ig[h] - m_new[h]) for h in hs]
    decay = [jnp.exp(bc[h][n - 1:n] + m0[h] - m_new[h]) for h in hs]
    c_new = [decay[h] * c0[h] + _dot(kw[h].T.astype(BF16), v16[h]) for h in hs]
    n_new = [decay[h] * n0[h] + jnp.sum(kw[h], axis=0, keepdims=True) for h in hs]
    for h in hs:
        c_sc[h] = c_new[h]
        n_sc[h] = n_new[h]
        m_sc[h] = m_new[h]
        c_out[h] = c_new[h]
        n_out[h] = n_new[h]
        m_out[h] = m_new[h]
        hn = hh[h] * lax.rsqrt(jnp.mean(hh[h] * hh[h], axis=-1, keepdims=True) + EPS) * ng_ref[:, sv[h]]
        o = jax.nn.sigmoid(_pad_rows(og_ref[:, sv[h]], n)) * hn
        o_ref[:, sv[h]] = o[:lin].astype(o_ref.dtype)


def mlstm_scan(z, bias_row, norm_g, c0, n0, m0, nb_batch, nqp, nvalid):
    lin = min(CHUNK, nqp)
    nc = nqp // lin
    tok = lambda w, off: pl.BlockSpec((lin, w), lambda b, c: (b * nc + c, off // w))
    body = functools.partial(_mlstm_body, nvalid=min(nvalid, CHUNK))
    st = lambda shape: pl.BlockSpec((None, ML_H) + shape, lambda b, c: (b, 0, 0, 0))
    wq, wv = ML_H * ML_DK, ML_H * ML_DV
    return pl.pallas_call(
        body, grid=(nb_batch, nc),
        in_specs=[tok(wq, Z_MQ), tok(wq, Z_MK), tok(wv, Z_MV), tok(wv, Z_MO), tok(LANE, Z_SM),
                  pl.BlockSpec((1, LANE), lambda b, c: (0, 0)), pl.BlockSpec((1, wv), lambda b, c: (0, 0)),
                  st((ML_DK, ML_DV)), st((1, ML_DK)), st((1, 1))],
        out_specs=[pl.BlockSpec((lin, wv), lambda b, c: (b * nc + c, 0)),
                   st((ML_DK, ML_DV)), st((1, ML_DK)), st((1, 1))],
        out_shape=[jax.ShapeDtypeStruct((nb_batch * nqp, wv), BF16),
                   jax.ShapeDtypeStruct((nb_batch, ML_H, ML_DK, ML_DV), F32),
                   jax.ShapeDtypeStruct((nb_batch, ML_H, 1, ML_DK), F32),
                   jax.ShapeDtypeStruct((nb_batch, ML_H, 1, 1), F32)],
        scratch_shapes=[pltpu.VMEM((ML_H, ML_DK, ML_DV), F32), pltpu.VMEM((ML_H, 1, ML_DK), F32),
                        pltpu.VMEM((ML_H, 1, 1), F32)],
        compiler_params=_cp(("parallel", "arbitrary")),
        name="mlstm_scan")(z, z, z, z, z, bias_row, norm_g.reshape(1, wv), c0, n0, m0)


def _gdn_body(q_ref, k_ref, v_ref, gz_ref, zs_ref, b_ref, a_ref, ng_ref, s0_ref,
              o_ref, s_out, s_sc, *, nvalid):
    ck = pl.program_id(1)
    n = CHUNK
    lin = q_ref.shape[0]

    @pl.when(ck == 0)
    def _():
        s_sc[...] = s0_ref[...]

    zs = _pad_rows(zs_ref[...], n)
    lane = _iota((1, LANE), 1)
    is_a = (lane >= SM_GA) & (lane < SM_GA + GD_H)
    g_all = jnp.where(is_a, a_ref[...] * _softplus(zs + b_ref[...]), 0.0)
    beta_all = jax.nn.sigmoid(zs)
    if nvalid < n:
        live = _iota((n, 1), 0) < nvalid
        g_all = jnp.where(live, g_all, 0.0)
        beta_all = jnp.where(live, beta_all, 0.0)
    gc_all = _cumsum_rows(g_all)
    gc_t = gc_all.T
    incl = _iota((n, n), 0) >= _iota((n, n), 1)
    strict = _iota((n, n), 0) > _iota((n, n), 1)
    eye = jnp.where(_iota((n, n), 0) == _iota((n, n), 1), 1.0, 0.0)
    hs = range(GD_H)
    sl = [slice(h * GD_D, (h + 1) * GD_D) for h in hs]
    s0 = [s_sc[h] for h in hs]
    q16 = [_pad_rows(q_ref[:, sl[h]], n).astype(BF16) for h in hs]
    k = [_pad_rows(k_ref[:, sl[h]], n) for h in hs]
    k16 = [k[h].astype(BF16) for h in hs]
    gc = [gc_all[:, SM_GA + h:SM_GA + h + 1] for h in hs]
    beta = [beta_all[:, SM_GB + h:SM_GB + h + 1] for h in hs]
    dec = [jnp.where(incl, jnp.exp(jnp.where(incl, gc[h] - gc_t[SM_GA + h:SM_GA + h + 1, :], 0.0)), 0.0) for h in hs]
    egc = [jnp.exp(gc[h]) for h in hs]
    kb = [k[h] * beta[h] for h in hs]
    xp = [-jnp.where(strict, _dot_nt(kb[h].astype(BF16), k16[h]) * dec[h], 0.0) for h in hs]
    tinv = [eye + xp[h] for h in hs]
    for _ in range(int(math.log2(n)) - 1):
        xp = [_dot3(xp[h], xp[h]) for h in hs]
        tinv = [tinv[h] + _dot3(tinv[h], xp[h]) for h in hs]
    rhs = [jnp.concatenate([_pad_rows(v_ref[:, sl[h]], n) * beta[h], kb[h] * egc[h]], axis=1) for h in hs]
    sol = [_dot3(tinv[h], rhs[h]) for h in hs]
    s016 = [s0[h].astype(BF16) for h in hs]
    u16 = [(sol[h][:, :GD_D] - _dot(sol[h][:, GD_D:].astype(BF16), s016[h])).astype(BF16) for h in hs]
    qk = [(_dot_nt(q16[h], k16[h]) * dec[h]).astype(BF16) for h in hs]
    o = [_dot(qk[h], u16[h]) + egc[h] * _dot(q16[h], s016[h]) for h in hs]
    kw = [(k[h] * jnp.exp(gc[h][n - 1:n] - gc[h])).T.astype(BF16) for h in hs]
    s_new = [jnp.exp(gc[h][n - 1:n]) * s0[h] + _dot(kw[h], u16[h]) for h in hs]
    for h in hs:
        s_sc[h] = s_new[h]
        s_out[h] = s_new[h]
        on = o[h] * lax.rsqrt(jnp.mean(o[h] * o[h], axis=-1, keepdims=True) + EPS) * ng_ref[...]
        o_ref[:, sl[h]] = (on * _silu(_pad_rows(gz_ref[:, sl[h]], n)))[:lin].astype(o_ref.dtype)


def gdn_scan(qkv, z, dt_row, nega_row, norm_g, s0, nb_batch, nqp, nvalid):
    lin = min(CHUNK, nqp)
    nc = nqp // lin
    w = GD_H * GD_D
    body = functools.partial(_gdn_body, nvalid=min(nvalid, CHUNK))
    tokq = lambda part: pl.BlockSpec((lin, w), lambda b, c: (b * nc + c, part))
    st = pl.BlockSpec((None, GD_H, GD_D, GD_D), lambda b, c: (b, 0, 0, 0))
    row = pl.BlockSpec((1, LANE), lambda b, c: (0, 0))
    return pl.pallas_call(
        body, grid=(nb_batch, nc),
        in_specs=[tokq(0), tokq(1), tokq(2),
                  pl.BlockSpec((lin, w), lambda b, c: (b * nc + c, Z_GZ // w)),
                  pl.BlockSpec((lin, LANE), lambda b, c: (b * nc + c, Z_SM // LANE)),
                  row, row, row, st],
        out_specs=[pl.BlockSpec((lin, w), lambda b, c: (b * nc + c, 0)), st],
        out_shape=[jax.ShapeDtypeStruct((nb_batch * nqp, w), BF16),
                   jax.ShapeDtypeStruct((nb_batch, GD_H, GD_D, GD_D), F32)],
        scratch_shapes=[pltpu.VMEM((GD_H, GD_D, GD_D), F32)],
        compiler_params=_cp(("parallel", "arbitrary")),
        name="gdn_scan")(qkv, qkv, qkv, z, z, dt_row, nega_row, norm_g.reshape(1, GD_D), s0)


HALO = 8


def _conv_rows(buf, x, hist_ref, w_ref, width, first):
    tb = x.shape[0]

    @pl.when(first)
    def _():
        buf[0:HALO, :] = hist_ref[...]

    buf[HALO:HALO + tb, :] = x
    y = w_ref[width - 1:width, :] * x
    for kk in range(1, width):
        y = y + w_ref[width - 1 - kk:width - kk, :] * buf[HALO - kk:HALO - kk + tb, :]
    return y


def _gdn_prep_body(z_ref, hist_ref, w_ref, o_ref, buf):
    ct = pl.program_id(1)
    tb = z_ref.shape[0]
    x = z_ref[...]
    y = _silu(_conv_rows(buf, x, hist_ref, w_ref, GDN_CONV, pl.program_id(2) == 0))
    buf[0:HALO, :] = x[tb - HALO:tb, :]
    part = ct // (GD_H * GD_D // y.shape[1])
    qscale = jnp.where(part == 0, GD_D ** -0.5, 1.0)
    for hh in range(y.shape[1] // GD_D):
        sl = slice(hh * GD_D, (hh + 1) * GD_D)
        yh = y[:, sl]
        nrm = yh * lax.rsqrt(jnp.sum(yh * yh, axis=-1, keepdims=True) + EPS) * qscale
        o_ref[:, sl] = jnp.where(part == 2, yh, nrm)


def gdn_prep(z, hist, w, nb_batch, nqp, tb, tc=512):
    nblk = nqp // tb
    ncol = 3 * GD_H * GD_D
    return pl.pallas_call(
        _gdn_prep_body, grid=(nb_batch, ncol // tc, nblk),
        in_specs=[pl.BlockSpec((tb, tc), lambda b, c, i: (b * nblk + i, Z_GQKV // tc + c)),
                  pl.BlockSpec((None, HALO, tc), lambda b, c, i: (b, 0, c)),
                  pl.BlockSpec((GDN_CONV, tc), lambda b, c, i: (0, c))],
        out_specs=pl.BlockSpec((tb, tc), lambda b, c, i: (b * nblk + i, c)),
        out_shape=jax.ShapeDtypeStruct((nb_batch * nqp, ncol), F32),
        scratch_shapes=[pltpu.VMEM((HALO + tb, tc), F32)],
        compiler_params=_cp(("parallel", "parallel", "arbitrary")), name="gdn_prep")(z, hist, w)


def _ffn_act_body(ua_ref, ub_ref, ha_ref, hb_ref, wa_ref, wb_ref, o_ref, bufa, bufb):
    first = pl.program_id(1) == 0
    tb = ua_ref.shape[0]
    xa = ua_ref[...]
    xb = ub_ref[...]
    ya = _conv_rows(bufa, xa, ha_ref, wa_ref, FFN_CONV, first)
    yb = _conv_rows(bufb, xb, hb_ref, wb_ref, FFN_CONV, first)
    bufa[0:HALO, :] = xa[tb - HALO:tb, :]
    bufb[0:HALO, :] = xb[tb - HALO:tb, :]
    o_ref[...] = (_silu(ya) * yb).astype(o_ref.dtype)


def ffn_act(u, hist, w, nb_batch, nqp, tb):
    nblk = nqp // tb
    f = u.shape[1] // 2
    tok = lambda half: pl.BlockSpec((tb, f), lambda b, i: (b * nblk + i, half))
    hs = lambda half: pl.BlockSpec((None, HALO, f), lambda b, i: (b, 0, half))
    ws = lambda half: pl.BlockSpec((FFN_CONV, f), lambda b, i: (0, half))
    return pl.pallas_call(
        _ffn_act_body, grid=(nb_batch, nblk),
        in_specs=[tok(0), tok(1), hs(0), hs(1), ws(0), ws(1)],
        out_specs=pl.BlockSpec((tb, f), lambda b, i: (b * nblk + i, 0)),
        out_shape=jax.ShapeDtypeStruct((nb_batch * nqp, f), BF16),
        scratch_shapes=[pltpu.VMEM((HALO + tb, f), F32), pltpu.VMEM((HALO + tb, f), F32)],
        compiler_params=_cp(("parallel", "arbitrary")), name="ffn_act")(u, u, hist, hist, w, w)


def _merge_body(oa_ref, ob_ref, oc_ref, w_ref, g0_ref, g1_ref, g2_ref, o_ref):
    acc = jax.nn.sigmoid(g0_ref[...]) * _dot(oa_ref[...], w_ref[0])
    acc = acc + jax.nn.sigmoid(g1_ref[...]) * _dot(ob_ref[...], w_ref[1])
    acc = acc + jax.nn.sigmoid(g2_ref[...]) * _dot(oc_ref[...], w_ref[2])
    o_ref[...] = acc.astype(o_ref.dtype)


def merge_branches(oa, ob, oc, wb, z, tm, tn):
    t = oa.shape[0]
    d = wb.shape[2]
    tok = pl.BlockSpec((tm, 1024), lambda i, j: (i, 0))
    gate = lambda nbr: pl.BlockSpec((tm, tn), lambda i, j: (i, (Z_BR + nbr * d) // tn + j))
    return pl.pallas_call(
        _merge_body, grid=(t // tm, d // tn),
        in_specs=[tok, tok, tok, pl.BlockSpec((3, 1024, tn), lambda i, j: (0, 0, j)), gate(0), gate(1), gate(2)],
        out_specs=pl.BlockSpec((tm, tn), lambda i, j: (i, j)),
        out_shape=jax.ShapeDtypeStruct((t, d), BF16),
        compiler_params=_cp(("parallel", "parallel")), name="merge_branches")(oa, ob, oc, wb, z, z, z)


def _layer_weights(p):
    w_in = p['w_in']
    offs = np.cumsum([0, 1024, 1536, 24, 512, 512, 1024, 4, 4, 1024, 3072, 8, 8, 1024, 6144])
    seg = lambda i: w_in[:, offs[i]:offs[i + 1]]
    kv = seg(1).reshape(-1, 3, 2, NSA_G * DH)
    small = jnp.concatenate([seg(2), seg(6), seg(7), seg(10), seg(11)], axis=1)
    small = jnp.pad(small, ((0, 0), (0, Z_COLS - Z_SM - small.shape[1])))
    gap = jnp.zeros((w_in.shape[0], Z_KV - Z_MQ - ML_H * ML_DK), w_in.dtype)
    w_all = jnp.concatenate([seg(0), seg(5), seg(8), seg(12), seg(9), seg(13), seg(3), gap,
                             kv[:, :, 0].reshape(-1, KV_W // 2), kv[:, :, 1].reshape(-1, KV_W // 2),
                             seg(4), small], axis=1)
    assert w_all.shape[1] == Z_COLS
    w1 = p['w_cmp1']
    wcat = jnp.concatenate([w1[:, :CMP_STRIDE].reshape(2, CMP_STRIDE * DH, DH),
                            w1[:, CMP_STRIDE:].reshape(2, CMP_STRIDE * DH, DH)], axis=2).astype(BF16)

    def lane_row(pairs):
        row = jnp.zeros((LANE,), F32)
        for off, v in pairs:
            row = row.at[off:off + v.shape[0]].set(v.astype(F32))
        return row.reshape(1, LANE)

    f = p['w_down'].shape[0]
    fpad = FFN_PAD - f
    pad_halves = lambda a: jnp.concatenate(
        [jnp.pad(a[..., :f], ((0, 0),) * (a.ndim - 1) + ((0, fpad),)),
         jnp.pad(a[..., f:], ((0, 0),) * (a.ndim - 1) + ((0, fpad),))], axis=-1)
    return dict(
        w_up=pad_halves(p['w_up']).astype(BF16), ffn_conv_w=pad_halves(p['ffn_conv_w']),
        w_down=jnp.pad(p['w_down'], ((0, fpad), (0, 0))).astype(BF16), pad_halves=pad_halves,
        w_all=w_all.astype(BF16), wcat=wcat,
        pe8=jnp.pad(p['cmp_pe'].reshape(2, 1, CMP_BLOCK * DH), ((0, 0), (0, 7), (0, 0))).astype(BF16),
        w1flat=w1.reshape(2, CMP_BLOCK * DH, DH).astype(BF16), w2=p['w_cmp2'].astype(BF16),
        ml_bias=lane_row([(SM_MI, p['mlstm_b_i']), (SM_MF, p['mlstm_b_f'])]),
        gd_dt=lane_row([(SM_GA, p['gdn_dt_bias'])]),
        gd_nega=lane_row([(SM_GA, -jnp.exp(p['gdn_a_log'].astype(F32)))]),
        w_branch=p['w_branch'].astype(BF16), w_out=p['w_out'].astype(BF16))


def _hist_rows(state):
    return jnp.pad(state, ((0, 0), (HALO - state.shape[1], 0), (0, 0)))


def _group_layer(x, grp, st, p, w, layer):
    nb, nq, nqp, pos0 = grp['nb'], grp['nq'], grp['nqp'], grp['pos0']
    t = nb * nqp
    big = nqp >= 512
    tm = 512 if big else t
    h = rmsnorm_bf16(x, p['norm1_g'], min(tm, 256))
    z = matmul(h, w['w_all'], tm, 1024 if big else 2048, name="in_proj")

    tb = 256 if big else nqp
    q_rot, rows, kvs, win_new = nsa_prep(z, grp['rope'], p['q_norm_g'], p['k_norm_g'], nqp, tb)
    if pos0 == 0:
        n_pages = nqp // PAGE
        fs = compress_pages(rows.reshape(nb * n_pages, PAGE_SUB, LANE), w['wcat'], nb, n_pages, 8)
        newpage = None
    else:
        live = (jnp.arange(nqp * ROW_SUB) < nq * ROW_SUB)[None, :, None]
        newpage = jnp.pad(jnp.where(live, rows.reshape(nb, nqp * ROW_SUB, LANE), 0.0),
                          ((0, 0), (0, PAGE_SUB - nqp * ROW_SUB), (0, 0)))
        fs_past = compress_pages(st['cache'], w['wcat'], nb, st['table'].shape[1], 8,
                                 table=st['table'], layer=layer)
        fs_new = compress_pages(newpage, w['wcat'], nb, 1, 1)
        fs = jnp.concatenate([fs_past, fs_new], axis=1)
    nblk_c = fs.shape[1]
    ncp = -(-nblk_c // LANE) * LANE
    fs = fs.reshape(nb, nblk_c, 4, 2, DH)
    first = jnp.pad(fs[:, :, :, 0].reshape(nb, nblk_c, 512), ((0, 0), (0, ncp - nblk_c), (0, 0)))
    second = jnp.pad(fs[:, 1:, :, 1].reshape(nb, nblk_c - 1, 512), ((0, 0), (0, ncp - nblk_c + 1), (0, 0)))
    kvc = compress_finish(first, second, w['pe8'], w['w1flat'], w['w2'])
    ns = nblk_c * CMP_STRIDE // SLC_BLOCK
    nsp = -(-ns // LANE) * LANE
    t_all = pos0 + nq
    nsel = min(N_SELECT, -(-t_all // SLC_BLOCK))
    qb = 64 if big else nqp
    o_cmp, sel = cmp_select(q_rot, kvc, importance_matrix(ncp, nsp), nb, nqp, qb, pos0, ns, nsel)
    emat = expand_matrices()
    if pos0 == 0:
        o_slc = slc_attention(q_rot, sel, emat, nb, nqp, qb, pos0, nsp, kvs=kvs)
    else:
        o_slc = slc_attention(q_rot, sel, emat, nb, nqp, qb, pos0, nsp, cache=st['cache'], table=st['table'],
                              layer=layer, newpage=newpage)
    win3 = win_new.reshape(nb, nqp, 512)
    win_all = jnp.concatenate([st['win'], win3], axis=1)
    o_win = win_attention(q_rot, win_all, nb, nqp, qb, pos0)
    o_nsa = nsa_combine(o_cmp, o_slc, o_win, z, tb)
    win_state = jnp.concatenate([st['win'], win3[:, :nq]], axis=1)[:, -WINDOW:]

    o_ml, m_c, m_n, m_m = mlstm_scan(z, w['ml_bias'], p['mlstm_norm_g'], st['m_c'], st['m_n'], st['m_m'],
                                     nb, nqp, nq)

    tbc = 512 if big else nqp
    qkv = gdn_prep(z, _hist_rows(st['g_conv']), p['gdn_conv_w'], nb, nqp, tbc)
    o_gd, g_s = gdn_scan(qkv, z, w['gd_dt'], w['gd_nega'], p['gdn_norm_g'], st['g_s'], nb, nqp, nq)
    z3 = z.reshape(nb, nqp, Z_COLS)
    g_conv_new = z3[:, nq - (GDN_CONV - 1):nq, Z_GQKV:Z_GQKV + 3 * GD_H * GD_D]

    mrg = merge_branches(o_nsa, o_ml, o_gd, w['w_branch'], z, tm, 512)
    x1 = matmul(mrg, w['w_out'], tm, 512 if big else 2048, res=x, name="out_proj")

    h2 = rmsnorm_bf16(x1, p['norm2_g'], min(tm, 256))
    u = matmul(h2, w['w_up'], tm, 1024, name="up_proj")
    act = ffn_act(u, _hist_rows(w['pad_halves'](st['f_conv'])), w['ffn_conv_w'], nb, nqp, 128 if big else nqp)
    x2 = matmul(act, w['w_down'], tm, 512, res=x1, name="down_proj")
    f = p['w_down'].shape[0]
    u_last = u.reshape(nb, nqp, -1)[:, nq - (FFN_CONV - 1):nq]
    f_conv_new = jnp.concatenate([u_last[..., :f], u_last[..., FFN_PAD:FFN_PAD + f]], axis=-1)

    new_rows = rows.reshape(nb, nqp, 4, NSA_G, DH)[:, :nq]
    states = (new_rows, win_state.reshape(nb, WINDOW, 2, NSA_G, DH), m_c, m_n.reshape(nb, ML_H, ML_DK),
              m_m.reshape(nb, ML_H), g_conv_new, g_s, f_conv_new)
    return x2, states


def kernel(x_prompt, x_sample, cache_nsa_kv, page_table, cache_nsa_win, state_mlstm_c, state_mlstm_n,
           state_mlstm_m, state_gdn_conv, state_gdn_s, state_ffn_conv, norm1_g, w_in, q_norm_g, k_norm_g,
           w_cmp1, cmp_pe, w_cmp2, mlstm_b_i, mlstm_b_f, mlstm_norm_g, gdn_conv_w, gdn_a_log, gdn_dt_bias,
           gdn_norm_g, w_branch, w_out, norm2_g, w_up, ffn_conv_w, w_down):
    depth = w_in.shape[0]
    bp, sp, d = x_prompt.shape
    bs, ss, _ = x_sample.shape
    n_pages = page_table.shape[1]
    past = n_pages * PAGE
    ssp = -(-ss // 16) * 16
    assert sp % KC == 0 and ss >= GDN_CONV - 1 and cache_nsa_win.shape[2] == WINDOW and n_pages % (KC // PAGE) == 0
    grp_p = dict(nb=bp, nq=sp, nqp=sp, pos0=0, rope=rope_tables(0, sp))
    grp_s = dict(nb=bs, nq=ss, nqp=ssp, pos0=past, rope=rope_tables(past, ssp))
    cache = cache_nsa_kv.reshape(depth, cache_nsa_kv.shape[1], PAGE_SUB, LANE)
    xp = x_prompt.reshape(bp * sp, d)
    xs = jnp.pad(x_sample, ((0, 0), (0, ssp - ss), (0, 0))).reshape(bs * ssp, d)
    st_p, st_s = [], []
    for l in range(depth):
        p = dict(norm1_g=norm1_g[l], w_in=w_in[l], q_norm_g=q_norm_g[l], k_norm_g=k_norm_g[l],
                 w_cmp1=w_cmp1[l], cmp_pe=cmp_pe[l], w_cmp2=w_cmp2[l], mlstm_b_i=mlstm_b_i[l],
                 mlstm_b_f=mlstm_b_f[l], mlstm_norm_g=mlstm_norm_g[l], gdn_conv_w=gdn_conv_w[l],
                 gdn_a_log=gdn_a_log[l], gdn_dt_bias=gdn_dt_bias[l], gdn_norm_g=gdn_norm_g[l],
                 w_branch=w_branch[l], w_out=w_out[l], norm2_g=norm2_g[l], w_up=w_up[l],
                 ffn_conv_w=ffn_conv_w[l], w_down=w_down[l])
        w = _layer_weights(p)
        zero_p = dict(win=jnp.zeros((bp, WINDOW, 512), F32),
                      m_c=jnp.zeros((bp, ML_H, ML_DK, ML_DV), F32), m_n=jnp.zeros((bp, ML_H, 1, ML_DK), F32),
                      m_m=jnp.zeros((bp, ML_H, 1, 1), F32),
                      g_conv=jnp.zeros((bp, GDN_CONV - 1, 3 * GD_H * GD_D), F32),
                      g_s=jnp.zeros((bp, GD_H, GD_D, GD_D), F32),
                      f_conv=jnp.zeros((bp, FFN_CONV - 1, state_ffn_conv.shape[-1]), F32))
        xp, s_p = _group_layer(xp, grp_p, zero_p, p, w, l)
        st = dict(cache=cache, table=page_table, win=cache_nsa_win[l].reshape(bs, WINDOW, 512),
                  m_c=state_mlstm_c[l], m_n=state_mlstm_n[l].reshape(bs, ML_H, 1, ML_DK),
                  m_m=state_mlstm_m[l].reshape(bs, ML_H, 1, 1), g_conv=state_gdn_conv[l],
                  g_s=state_gdn_s[l], f_conv=state_ffn_conv[l])
        xs, s_s = _group_layer(xs, grp_s, st, p, w, l)
        st_p.append(s_p)
        st_s.append(s_s)
    outs = [xp.reshape(bp, sp, d), xs.reshape(bs, ssp, d)[:, :ss]]
    for i in range(8):
        outs.append(jnp.stack([s[i] for s in st_p], axis=0))
        outs.append(jnp.stack([s[i] for s in st_s], axis=0))
    return tuple(outs)
```

```python
import functools
import math

import numpy as np
import jax
import jax.numpy as jnp
from jax import lax
from jax.experimental import pallas as pl
from jax.experimental.pallas import tpu as pltpu

F32 = jnp.float32
BF16 = jnp.bfloat16
HI = lax.Precision.HIGHEST

EPS = 1e-6
NEG_INF = -1e30
LANE = 128
SUBLANE = 8
PAGE = 128
DH = 128
NSA_HEADS = 8
NSA_G = 2
NSA_GROUP = NSA_HEADS // NSA_G
ROPE_DIM = DH // 4
ROPE_THETA = 500000.0
CMP_BLOCK = 32
CMP_STRIDE = 16
SLC_BLOCK = 64
N_SELECT = 16
WINDOW = 512
ML_H, ML_DK, ML_DV = 4, 128, 256
GD_H, GD_D = 8, 128
GDN_CONV = 4
FFN_CONV = 3
CHUNK = 64
FFN_PAD = 5632
KC = 1024
BLK_PER_KC = KC // SLC_BLOCK
WIN_PER_LANE = LANE // BLK_PER_KC
VMEM_LIMIT = 56 * 1024 * 1024

Z_Q, Z_MV, Z_MO, Z_GZ = 0, 1024, 2048, 3072
Z_GQKV, Z_BR = 4096, 7168
Z_MQ, Z_KV, Z_MK, Z_SM = 13312, 13824, 15360, 15872
Z_COLS = 16384
KV_W = 1536
ROW_W = 4 * NSA_G * DH
ROW_SUB = ROW_W // LANE
PAGE_SUB = PAGE * ROW_SUB
SM_NG, SM_MI, SM_MF, SM_GA, SM_GB = 0, 24, 28, 32, 40


def _cp(sem):
    return pltpu.CompilerParams(dimension_semantics=sem, vmem_limit_bytes=VMEM_LIMIT)


def _iota(shape, dim):
    return lax.broadcasted_iota(jnp.int32, shape, dim)


def _dot(a, b, **kw):
    return jnp.dot(a, b, preferred_element_type=F32, **kw)


def _dot_nt(a, b, **kw):
    return lax.dot_general(a, b, (((1,), (1,)), ((), ())), preferred_element_type=F32, **kw)


def _silu(x):
    return x * jax.nn.sigmoid(x)


def _split(x):
    hi = x.astype(BF16)
    return hi, (x - hi.astype(F32)).astype(BF16)


def _dot3(a, b):
    ah, al = _split(a)
    bh, bl = _split(b)
    return _dot(ah, bh) + (_dot(ah, bl) + _dot(al, bh))


def _rmsnorm_body(x_ref, g_ref, o_ref):
    x = x_ref[...]
    y = x * lax.rsqrt(jnp.mean(x * x, axis=-1, keepdims=True) + EPS)
    o_ref[...] = (y * g_ref[...]).astype(o_ref.dtype)


def rmsnorm_bf16(x, g, tm):
    t, d = x.shape
    return pl.pallas_call(
        _rmsnorm_body, grid=(t // tm,),
        in_specs=[pl.BlockSpec((tm, d), lambda i: (i, 0)), pl.BlockSpec((1, d), lambda i: (0, 0))],
        out_specs=pl.BlockSpec((tm, d), lambda i: (i, 0)),
        out_shape=jax.ShapeDtypeStruct((t, d), BF16),
        compiler_params=_cp(("parallel",)), name="rmsnorm")(x, g.reshape(1, d))


def _mm_body(a_ref, w_ref, o_ref):
    o_ref[...] = _dot(a_ref[...], w_ref[...]).astype(o_ref.dtype)


def _mm_res_body(a_ref, w_ref, r_ref, o_ref):
    o_ref[...] = r_ref[...] + _dot(a_ref[...], w_ref[...])


def matmul(a, w, tm, tn, res=None, name="matmul"):
    m, k = a.shape
    n = w.shape[1]
    in_specs = [pl.BlockSpec((tm, k), lambda i, j: (i, 0)), pl.BlockSpec((k, tn), lambda i, j: (0, j))]
    args = [a, w]
    body = _mm_body
    if res is not None:
        in_specs.append(pl.BlockSpec((tm, tn), lambda i, j: (i, j)))
        args.append(res)
        body = _mm_res_body
    return pl.pallas_call(
        body, grid=(m // tm, n // tn), in_specs=in_specs,
        out_specs=pl.BlockSpec((tm, tn), lambda i, j: (i, j)),
        out_shape=jax.ShapeDtypeStruct((m, n), F32),
        compiler_params=_cp(("parallel", "parallel")), name=name)(*args)


def _nsa_prep_body(zq_ref, zkv_ref, c_ref, s1_ref, s2_ref, qg_ref, kg_ref, q_out, rows_out, kvs_out, win_out):
    c = c_ref[...]
    s1 = s1_ref[...]
    s2 = s2_ref[...]
    tb = zq_ref.shape[0]

    def norm_rope(x, g):
        y = x * lax.rsqrt(jnp.mean(x * x, axis=-1, keepdims=True) + EPS) * g
        return y * c + pltpu.roll(y, LANE - ROPE_DIM // 2, 1) * s1 + pltpu.roll(y, ROPE_DIM // 2, 1) * s2

    for h in range(NSA_HEADS):
        sl = slice(h * DH, (h + 1) * DH)
        q_out[:, sl] = (norm_rope(zq_ref[:, sl], qg_ref[...]) * (DH ** -0.5)).astype(q_out.dtype)
    for br in range(3):
        for g in range(NSA_G):
            col = (br * NSA_G + g) * DH
            kk = norm_rope(zkv_ref[:, col:col + DH], kg_ref[br:br + 1, :])
            vv = zkv_ref[:, KV_W // 2 + col:KV_W // 2 + col + DH]
            if br < 2:
                rows_out[pl.ds(4 * br + g, tb, stride=ROW_SUB), :] = kk
                rows_out[pl.ds(4 * br + 2 + g, tb, stride=ROW_SUB), :] = vv
            else:
                win_out[:, g * DH:(g + 1) * DH] = kk
                win_out[:, 2 * DH + g * DH:2 * DH + (g + 1) * DH] = vv
            if br == 1:
                kvs_out[:, g * DH:(g + 1) * DH] = kk.astype(kvs_out.dtype)
                kvs_out[:, (2 + g) * DH:(3 + g) * DH] = vv.astype(kvs_out.dtype)


def nsa_prep(z, tabs, q_g, k_g, nqp, tb):
    t = z.shape[0]
    nblk = nqp // tb
    tab_spec = pl.BlockSpec((tb, LANE), lambda i: (i % nblk, 0))
    return pl.pallas_call(
        _nsa_prep_body, grid=(t // tb,),
        in_specs=[pl.BlockSpec((tb, 1024), lambda i: (i, Z_Q // 1024)),
                  pl.BlockSpec((tb, KV_W), lambda i: (i, Z_KV // KV_W)), tab_spec, tab_spec, tab_spec,
                  pl.BlockSpec((1, DH), lambda i: (0, 0)), pl.BlockSpec((3, DH), lambda i: (0, 0))],
        out_specs=[pl.BlockSpec((tb, 1024), lambda i: (i, 0)), pl.BlockSpec((tb * ROW_SUB, LANE), lambda i: (i, 0)),
                   pl.BlockSpec((tb, 512), lambda i: (i, 0)), pl.BlockSpec((tb, 512), lambda i: (i, 0))],
        out_shape=[jax.ShapeDtypeStruct((t, 1024), BF16), jax.ShapeDtypeStruct((t * ROW_SUB, LANE), F32),
                   jax.ShapeDtypeStruct((t, 512), BF16), jax.ShapeDtypeStruct((t, 512), F32)],
        compiler_params=_cp(("parallel",)), name="nsa_prep")(z, z, *tabs, q_g.reshape(1, DH), k_g)


def rope_tables(pos0, nqp):
    half = ROPE_DIM // 2
    pos = pos0 + jnp.arange(nqp, dtype=jnp.int32)
    inv = ROPE_THETA ** (-jnp.arange(half, dtype=F32) / half)
    ang = pos.astype(F32)[:, None] * inv[None, :]
    cos, sin = jnp.cos(ang), jnp.sin(ang)
    zeros = jnp.zeros((nqp, LANE - ROPE_DIM), F32)
    c = jnp.concatenate([cos, cos, jnp.ones((nqp, LANE - ROPE_DIM), F32)], axis=1)
    s1 = jnp.concatenate([-sin, jnp.zeros((nqp, half), F32), zeros], axis=1)
    s2 = jnp.concatenate([jnp.zeros((nqp, half), F32), sin, zeros], axis=1)
    return c, s1, s2


def _compress_body(*refs, pps, paged):
    if paged:
        refs = refs[1:]
    slots = refs[:pps]
    w_refs = refs[pps:pps + 2]
    o_ref = refs[pps + 2]
    nb = PAGE // CMP_STRIDE
    for kind in range(2):
        xs = []
        for g in range(NSA_G):
            part = kind * 2 + g
            per_slot = [jnp.concatenate([s[pl.ds(ROW_SUB * j + part, nb, stride=CMP_STRIDE * ROW_SUB), :]
                                         for j in range(CMP_STRIDE)], axis=1) for s in slots]
            xs.append(jnp.concatenate(per_slot, axis=0) if pps > 1 else per_slot[0])
        x = jnp.concatenate(xs, axis=0).astype(BF16)
        fs = _dot(x, w_refs[kind][...])
        for g in range(NSA_G):
            o_ref[:, (kind * 2 + g) * 2 * DH:(kind * 2 + g + 1) * 2 * DH] = fs[g * nb * pps:(g + 1) * nb * pps]


def compress_pages(src, wcat, nb_batch, n_pages, pps, table=None, layer=None):
    nb = PAGE // CMP_STRIDE
    steps = n_pages // pps
    w_spec = pl.BlockSpec((CMP_STRIDE * DH, 2 * DH), lambda b, i, *_: (0, 0))
    out_spec = pl.BlockSpec((None, nb * pps, 1024), lambda b, i, *_: (b, i, 0))
    out_shape = jax.ShapeDtypeStruct((nb_batch, n_pages * nb, 1024), F32)
    body = functools.partial(_compress_body, pps=pps, paged=table is not None)
    if table is None:
        slot_specs = [pl.BlockSpec((None, PAGE_SUB, LANE), functools.partial(
            lambda b, i, s: (b * n_pages + i * pps + s, 0, 0), s=s)) for s in range(pps)]
        return pl.pallas_call(
            body, grid=(nb_batch, steps), in_specs=slot_specs + [w_spec, w_spec], out_specs=out_spec,
            out_shape=out_shape, compiler_params=_cp(("parallel", "parallel")),
            name="compress")(*([src] * pps), wcat[0], wcat[1])
    slot_specs = [pl.BlockSpec((None, None, PAGE_SUB, LANE), functools.partial(
        lambda b, i, tbl, s: (layer, tbl[b, i * pps + s], 0, 0), s=s)) for s in range(pps)]
    gs = pltpu.PrefetchScalarGridSpec(num_scalar_prefetch=1, grid=(nb_batch, steps),
                                      in_specs=slot_specs + [w_spec, w_spec], out_specs=out_spec)
    return pl.pallas_call(body, grid_spec=gs, out_shape=out_shape,
                          compiler_params=_cp(("parallel", "parallel")),
                          name="compress_paged")(table, *([src] * pps), wcat[0], wcat[1])


def _cfin_body(f_ref, s_ref, pe_ref, w1_ref, w2_ref, o_ref):
    for kind in range(2):
        bias = _dot(pe_ref[kind], w1_ref[kind])[0:1, :]
        for g in range(NSA_G):
            sl = slice((kind * 2 + g) * DH, (kind * 2 + g + 1) * DH)
            hid = f_ref[:, sl] + s_ref[:, sl] + bias
            o_ref[:, sl] = _dot(_silu(hid).astype(BF16), w2_ref[kind]).astype(o_ref.dtype)


def compress_finish(first, second_next, pe8, w1flat, w2):
    b, ncp, _ = first.shape
    spec = pl.BlockSpec((None, ncp, 512), lambda i: (i, 0, 0))
    return pl.pallas_call(
        _cfin_body, grid=(b,),
        in_specs=[spec, spec, pl.BlockSpec(pe8.shape, lambda i: (0, 0, 0)),
                  pl.BlockSpec(w1flat.shape, lambda i: (0, 0, 0)), pl.BlockSpec(w2.shape, lambda i: (0, 0, 0))],
        out_specs=spec, out_shape=jax.ShapeDtypeStruct((b, ncp, 512), BF16),
        compiler_params=_cp(("parallel",)), name="compress_finish")(first, second_next, pe8, w1flat, w2)


def _stack_heads(q_ref, g):
    return jnp.concatenate([q_ref[:, (g * NSA_GROUP + h) * DH:(g * NSA_GROUP + h + 1) * DH]
                            for h in range(NSA_GROUP)], axis=0)


def _unstack_heads(o_ref, g, o, qb):
    for h in range(NSA_GROUP):
        o_ref[:, (g * NSA_GROUP + h) * DH:(g * NSA_GROUP + h + 1) * DH] = o[h * qb:(h + 1) * qb]


def _rank(score, column, blk, ns):
    rank = jnp.zeros(score.shape, F32)
    for j in range(ns):
        sj = column(j)
        rank = rank + jnp.where(sj > score, 1.0, 0.0) + jnp.where(sj == score, jnp.where(blk > j, 1.0, 0.0), 0.0)
    return rank


def _cmp_body(q_ref, kv_ref, m_ref, o_ref, sel_ref, *, qb, pos0, ncp, ns, nsp, nsel, packed):
    i = pl.program_id(1)
    t1 = pos0 + i * qb + _iota((qb, 1), 0)
    vis = (_iota((1, ncp), 1) * CMP_STRIDE + (CMP_BLOCK - 1)) <= t1
    bias = jnp.concatenate([jnp.where(vis, 0.0, NEG_INF)] * NSA_GROUP, axis=0)
    any_vis = jnp.concatenate([jnp.where(t1 >= CMP_BLOCK - 1, 1.0, 0.0)] * NSA_GROUP, axis=0)
    pcs = []
    for g in range(NSA_G):
        s = _dot_nt(_stack_heads(q_ref, g), kv_ref[:, g * DH:(g + 1) * DH]) + bias
        e = jnp.exp(s - jnp.max(s, axis=-1, keepdims=True))
        pc = e * (any_vis / jnp.sum(e, axis=-1, keepdims=True))
        _unstack_heads(o_ref, g, _dot(pc.astype(BF16), kv_ref[:, (2 + g) * DH:(3 + g) * DH]), qb)
        pcs.append((pc[0:qb] + pc[qb:2 * qb]) + (pc[2 * qb:3 * qb] + pc[3 * qb:4 * qb]))
    cur = t1 // SLC_BLOCK

    def scores(imp, blk):
        valid = (blk * SLC_BLOCK <= t1) & (blk < ns)
        forced = (blk == 0) | (blk == cur) | (blk == cur - 1)
        return jnp.where(valid, jnp.where(forced, jnp.inf, imp), -jnp.inf)

    if packed:
        half = LANE // 2
        lane = _iota((1, LANE), 1)
        low = lane < half
        blk = lane & (half - 1)
        score = scores(_dot(pcs[0], m_ref[0], precision=HI) + _dot(pcs[1], m_ref[1], precision=HI), blk)
        st = score.T
        row8 = _iota((SUBLANE, 1), 0)
        tiles = []
        for hf in range(2):
            for v in range(half // SUBLANE):
                if SUBLANE * v >= ns:
                    tiles.append(jnp.full((SUBLANE, qb), -1.0, F32))
                    continue
                tile = st[hf * half + SUBLANE * v:hf * half + SUBLANE * (v + 1), :]
                acc = jnp.zeros((SUBLANE, qb), F32)
                for j in range(ns):
                    sj = st[hf * half + j:hf * half + j + 1, :]
                    ge = jnp.where(sj >= tile, 1.0, 0.0)
                    gt = jnp.where(sj > tile, 1.0, 0.0)
                    if SUBLANE * v > j:
                        acc = acc + ge
                    elif SUBLANE * v + SUBLANE - 1 < j:
                        acc = acc + gt
                    else:
                        acc = acc + jnp.where(row8 + SUBLANE * v > j, ge, gt)
                tiles.append(jnp.where(acc < nsel, 0.0, -1.0))
        selm = jnp.concatenate(tiles, axis=0).T
        sel_ref[:, 0:LANE] = jnp.where(low, selm, -1.0).astype(sel_ref.dtype)
        sel_ref[:, LANE:2 * LANE] = jnp.where(low, pltpu.roll(selm, half, 1), -1.0).astype(sel_ref.dtype)
    else:
        blk = _iota((1, nsp), 1)
        for g in range(NSA_G):
            score = scores(_dot(pcs[g], m_ref[0], precision=HI), blk)
            rank = _rank(score, lambda j: score[:, j:j + 1], blk, ns)
            sel_ref[:, g * nsp:(g + 1) * nsp] = jnp.where(rank < nsel, 0.0, -1.0).astype(sel_ref.dtype)


def cmp_select(q, kvc, nb_batch, nqp, qb, pos0, ns, nsp, nsel):
    ncp = kvc.shape[1]
    nblk = nqp // qb
    packed = 2 * ns <= LANE
    mmat = importance_matrix(ncp, nsp, packed)
    body = functools.partial(_cmp_body, qb=qb, pos0=pos0, ncp=ncp, ns=ns, nsp=nsp, nsel=nsel, packed=packed)
    return pl.pallas_call(
        body, grid=(nb_batch, nblk),
        in_specs=[pl.BlockSpec((qb, 1024), lambda b, i: (b * nblk + i, 0)),
                  pl.BlockSpec((None, ncp, 512), lambda b, i: (b, 0, 0)),
                  pl.BlockSpec(mmat.shape, lambda b, i: (0, 0, 0))],
        out_specs=[pl.BlockSpec((qb, 1024), lambda b, i: (b * nblk + i, 0)),
                   pl.BlockSpec((qb, NSA_G * nsp), lambda b, i: (b * nblk + i, 0))],
        out_shape=[jax.ShapeDtypeStruct((nb_batch * nqp, 1024), F32),
                   jax.ShapeDtypeStruct((nb_batch * nqp, NSA_G * nsp), BF16)],
        compiler_params=_cp(("parallel", "parallel")), name="cmp_select")(q, kvc, mmat)


def importance_matrix(ncp, nsp, packed):
    m = np.zeros((2 if packed else 1, ncp, nsp), np.float32)
    for j in range(nsp // 2 if packed else nsp):
        for n, w in ((4 * j - 1, 1.0), (4 * j, 2.0), (4 * j + 1, 2.0), (4 * j + 2, 2.0), (4 * j + 3, 1.0)):
            if 0 <= n < ncp:
                m[0, n, j] = w
                if packed:
                    m[1, n, nsp // 2 + j] = w
    return jnp.asarray(m)


MASK_BIG = 2.0 ** 100


def expand_matrices():
    e = np.zeros((WIN_PER_LANE, LANE, KC), np.float32)
    for w in range(WIN_PER_LANE):
        for key in range(KC):
            e[w, BLK_PER_KC * w + key // SLC_BLOCK, key] = MASK_BIG
    return jnp.asarray(e, BF16)


def _slc_body(*refs, qb, pos0, nslot, paged):
    if paged:
        refs = refs[1:]
    q_ref, sel0_ref, sel1_ref, e_ref = refs[:4]
    slots = refs[4:4 + nslot]
    k = 4 + nslot
    new_ref = refs[k] if paged else None
    k += 1 if paged else 0
    o_ref, m_sc, l_sc, acc_sc = refs[k:k + 4]
    i = pl.program_id(1)
    c = pl.program_id(2)
    nchunk = pl.num_programs(2)

    @pl.when(c == 0)
    def _():
        m_sc[...] = jnp.full(m_sc.shape, NEG_INF, F32)
        l_sc[...] = jnp.zeros(l_sc.shape, F32)
        acc_sc[...] = jnp.zeros(acc_sc.shape, F32)

    def keys_values(g):
        if not paged:
            kv = slots[0]
            return kv[:, g * DH:(g + 1) * DH], kv[:, (2 + g) * DH:(3 + g) * DH]
        out = []
        for part in (4 + g, 6 + g):
            pieces = [s[pl.ds(part, PAGE, stride=ROW_SUB), :] for s in slots]
            new = new_ref[pl.ds(part, PAGE, stride=ROW_SUB), :]
            pieces[0] = jnp.where(c == nchunk - 1, new, pieces[0])
            out.append(jnp.concatenate(pieces, axis=0).astype(BF16))
        return out

    def step(diagonal):
        ew = e_ref[c % WIN_PER_LANE]
        for g, sel_ref in ((0, sel0_ref), (1, sel1_ref)):
            kk, vv = keys_values(g)
            bias = _dot(sel_ref[...], ew)
            if diagonal:
                t1 = pos0 + i * qb + _iota((qb, 1), 0)
                bias = bias + jnp.where(c * KC + _iota((1, KC), 1) <= t1, 0.0, NEG_INF)
            s = _dot_nt(_stack_heads(q_ref, g), kk) + jnp.concatenate([bias] * NSA_GROUP, axis=0)
            m_old = m_sc[g]
            m_new = jnp.maximum(m_old, jnp.max(s, axis=-1, keepdims=True))
            alpha = jnp.exp(m_old - m_new)
            p = jnp.exp(s - m_new)
            l_sc[g] = alpha * l_sc[g] + jnp.sum(p, axis=-1, keepdims=True)
            acc_sc[g] = alpha * acc_sc[g] + _dot(p.astype(BF16), vv)
            m_sc[g] = m_new

    first_partial = (pos0 + i * qb) // KC
    last_needed = (pos0 + i * qb + qb - 1) // KC
    pl.when(c < first_partial)(functools.partial(step, False))
    pl.when((c >= first_partial) & (c <= last_needed))(functools.partial(step, True))

    @pl.when(c == nchunk - 1)
    def _():
        for g in range(NSA_G):
            _unstack_heads(o_ref, g, acc_sc[g] / l_sc[g], qb)


def slc_attention(q, sel, emat, nb_batch, nqp, qb, pos0, nsp, kvs=None, cache=None, table=None,
                  layer=None, newpage=None):
    nblk = nqp // qb
    lanes_g = nsp // LANE
    paged = cache is not None
    scratch = [pltpu.VMEM((NSA_G, NSA_GROUP * qb, 1), F32), pltpu.VMEM((NSA_G, NSA_GROUP * qb, 1), F32),
               pltpu.VMEM((NSA_G, NSA_GROUP * qb, DH), F32)]
    out_shape = jax.ShapeDtypeStruct((nb_batch * nqp, 1024), F32)
    q_map = lambda b, i, c, *_: (b * nblk + i, 0)
    e_spec = pl.BlockSpec((WIN_PER_LANE, LANE, KC), lambda b, i, c, *_: (0, 0, 0))
    if not paged:
        nchunk = nqp // KC
        last = lambda i: (pos0 + i * qb + qb - 1) // KC
        sel_specs = [pl.BlockSpec((qb, LANE), functools.partial(
            lambda b, i, c, g: (b * nblk + i, g * lanes_g + jnp.minimum(c, last(i)) // WIN_PER_LANE), g=g))
            for g in range(NSA_G)]
        kv_spec = pl.BlockSpec((KC, 512), lambda b, i, c: (b * nchunk + jnp.minimum(c, last(i)), 0))
        body = functools.partial(_slc_body, qb=qb, pos0=pos0, nslot=1, paged=False)
        return pl.pallas_call(
            body, grid=(nb_batch, nblk, nchunk),
            in_specs=[pl.BlockSpec((qb, 1024), q_map)] + sel_specs + [e_spec, kv_spec],
            out_specs=pl.BlockSpec((qb, 1024), q_map), out_shape=out_shape, scratch_shapes=scratch,
            compiler_params=_cp(("parallel", "parallel", "arbitrary")), name="slc_attention")(q, sel, sel, emat, kvs)
    n_pages = table.shape[1]
    nslot = KC // PAGE
    nchunk = n_pages // nslot + 1
    sel_specs = [pl.BlockSpec((qb, LANE), functools.partial(
        lambda b, i, c, tbl, g: (b * nblk + i, g * lanes_g + c // WIN_PER_LANE), g=g)) for g in range(NSA_G)]
    slot_specs = [pl.BlockSpec((None, None, PAGE_SUB, LANE), functools.partial(
        lambda b, i, c, tbl, s: (layer, tbl[b, jnp.minimum(c * nslot + s, n_pages - 1)], 0, 0), s=s))
        for s in range(nslot)]
    new_spec = pl.BlockSpec((None, PAGE_SUB, LANE), lambda b, i, c, tbl: (b, 0, 0))
    body = functools.partial(_slc_body, qb=qb, pos0=pos0, nslot=nslot, paged=True)
    gs = pltpu.PrefetchScalarGridSpec(
        num_scalar_prefetch=1, grid=(nb_batch, nblk, nchunk),
        in_specs=[pl.BlockSpec((qb, 1024), q_map)] + sel_specs + [e_spec] + slot_specs + [new_spec],
        out_specs=pl.BlockSpec((qb, 1024), q_map), scratch_shapes=scratch)
    return pl.pallas_call(body, grid_spec=gs, out_shape=out_shape,
                          compiler_params=_cp(("parallel", "parallel", "arbitrary")),
                          name="slc_attention_paged")(table, q, sel, sel, emat, *([cache] * nslot), newpage)


def _win_body(q_ref, w_ref, o_ref, *, qb, pos0, span):
    i = pl.program_id(1)
    kv = w_ref[pl.ds(pl.multiple_of(i * qb, qb), span), :].astype(BF16)
    qpos = pos0 + i * qb + _iota((qb, 1), 0)
    kpos = pos0 - WINDOW + i * qb + _iota((1, span), 1)
    rel = qpos - kpos
    mask = (rel >= 0) & (rel <= WINDOW) & (kpos >= 0)
    bias = jnp.concatenate([jnp.where(mask, 0.0, NEG_INF)] * NSA_GROUP, axis=0)
    for g in range(NSA_G):
        s = _dot_nt(_stack_heads(q_ref, g), kv[:, g * DH:(g + 1) * DH]) + bias
        e = jnp.exp(s - jnp.max(s, axis=-1, keepdims=True))
        p = e * (1.0 / jnp.sum(e, axis=-1, keepdims=True))
        _unstack_heads(o_ref, g, _dot(p.astype(BF16), kv[:, (2 + g) * DH:(3 + g) * DH]), qb)


def win_attention(q, win_all, nb_batch, nqp, qb, pos0):
    nblk = nqp // qb
    wtot = win_all.shape[1]
    body = functools.partial(_win_body, qb=qb, pos0=pos0, span=WINDOW + qb)
    return pl.pallas_call(
        body, grid=(nb_batch, nblk),
        in_specs=[pl.BlockSpec((qb, 1024), lambda b, i: (b * nblk + i, 0)),
                  pl.BlockSpec((None, wtot, 512), lambda b, i: (b, 0, 0))],
        out_specs=pl.BlockSpec((qb, 1024), lambda b, i: (b * nblk + i, 0)),
        out_shape=jax.ShapeDtypeStruct((nb_batch * nqp, 1024), F32),
        compiler_params=_cp(("parallel", "parallel")), name="win_attention")(q, win_all)


def _combine_body(oc_ref, os_ref, ow_ref, zs_ref, o_ref):
    gate = jax.nn.sigmoid(zs_ref[...])
    for h in range(NSA_HEADS):
        sl = slice(h * DH, (h + 1) * DH)
        c0 = SM_NG + 3 * h
        o = (gate[:, c0:c0 + 1] * oc_ref[:, sl] + gate[:, c0 + 1:c0 + 2] * os_ref[:, sl]
             + gate[:, c0 + 2:c0 + 3] * ow_ref[:, sl])
        o_ref[:, sl] = o.astype(o_ref.dtype)


def nsa_combine(oc, osl, ow, z, tb):
    t = oc.shape[0]
    spec = pl.BlockSpec((tb, 1024), lambda i: (i, 0))
    return pl.pallas_call(
        _combine_body, grid=(t // tb,),
        in_specs=[spec, spec, spec, pl.BlockSpec((tb, LANE), lambda i: (i, Z_SM // LANE))],
        out_specs=spec, out_shape=jax.ShapeDtypeStruct((t, 1024), BF16),
        compiler_params=_cp(("parallel",)), name="nsa_combine")(oc, osl, ow, z)


def _pad_rows(x, n):
    if x.shape[0] == n:
        return x
    return jnp.concatenate([x, jnp.zeros((n - x.shape[0], x.shape[1]), x.dtype)], axis=0)


def _cumsum_rows(x):
    n = x.shape[0]
    tri = jnp.where(_iota((n, n), 0) >= _iota((n, n), 1), 1.0, 0.0).astype(BF16)
    hi, lo = _split(x)
    lo2 = (x - hi.astype(F32) - lo.astype(F32)).astype(BF16)
    return _dot(tri, hi) + (_dot(tri, lo) + _dot(tri, lo2))


def _log_sigmoid(x):
    return jnp.minimum(x, 0.0) - jnp.log1p(jnp.exp(-jnp.abs(x)))


def _softplus(x):
    return jnp.maximum(x, 0.0) + jnp.log1p(jnp.exp(-jnp.abs(x)))


def _mlstm_body(q_ref, k_ref, v_ref, og_ref, zs_ref, b_ref, ng_ref, c0_ref, n0_ref, m0_ref,
                o_ref, c_out, n_out, m_out, c_sc, n_sc, m_sc, *, nvalid):
    ck = pl.program_id(1)
    n = CHUNK
    lin = q_ref.shape[0]

    @pl.when(ck == 0)
    def _():
        c_sc[...] = c0_ref[...]
        n_sc[...] = n0_ref[...]
        m_sc[...] = m0_ref[...]

    gates = _pad_rows(zs_ref[...], n) + b_ref[...]
    lane = _iota((1, LANE), 1)
    is_f = (lane >= SM_MF) & (lane < SM_MF + ML_H)
    lf_all = jnp.where(is_f, _log_sigmoid(gates), 0.0)
    ig_all = gates
    if nvalid < n:
        live = _iota((n, 1), 0) < nvalid
        lf_all = jnp.where(live, lf_all, 0.0)
        ig_all = jnp.where(live, gates, NEG_INF)
    comb = jnp.where(is_f, _cumsum_rows(lf_all), ig_all)
    comb_t = comb.T
    incl = _iota((n, n), 0) >= _iota((n, n), 1)
    hs = range(ML_H)
    sk = [slice(h * ML_DK, (h + 1) * ML_DK) for h in hs]
    sv = [slice(h * ML_DV, (h + 1) * ML_DV) for h in hs]
    c0 = [c_sc[h] for h in hs]
    n0 = [n_sc[h] for h in hs]
    m0 = [m_sc[h] for h in hs]
    q = [_pad_rows(q_ref[:, sk[h]], n) for h in hs]
    k = [_pad_rows(k_ref[:, sk[h]], n) * (ML_DK ** -0.5) for h in hs]
    q16 = [q[h].astype(BF16) for h in hs]
    v16 = [_pad_rows(v_ref[:, sv[h]], n).astype(BF16) for h in hs]
    ig = [comb[:, SM_MI + h:SM_MI + h + 1] for h in hs]
    bc = [comb[:, SM_MF + h:SM_MF + h + 1] for h in hs]
    dlog = [jnp.where(incl, bc[h] + (comb_t[SM_MI + h:SM_MI + h + 1, :] - comb_t[SM_MF + h:SM_MF + h + 1, :]),
                      -jnp.inf) for h in hs]
    qk = [_dot_nt(q16[h], k[h].astype(BF16)) for h in hs]
    qc = [_dot(q16[h], c0[h].astype(BF16)) for h in hs]
    inter = [bc[h] + m0[h] for h in hs]
    m_t = [jnp.maximum(inter[h], jnp.max(dlog[h], axis=1, keepdims=True)) for h in hs]
    a = [qk[h] * jnp.exp(dlog[h] - m_t[h]) for h in hs]
    w_inter = [jnp.exp(inter[h] - m_t[h]) for h in hs]
    num = [_dot(a[h].astype(BF16), v16[h]) + w_inter[h] * qc[h] for h in hs]
    den = [jnp.sum(a[h], axis=1, keepdims=True) + w_inter[h] * jnp.sum(q[h] * n0[h], axis=1, keepdims=True)
           for h in hs]
    hh = [num[h] / jnp.maximum(jnp.abs(den[h]), jnp.exp(-m_t[h])) for h in hs]
    m_new = [m_t[h][n - 1:n] for h in hs]
    kw = [k[h] * jnp.exp(bc[h][n - 1:n] - bc[h] + ig[h] - m_new[h]) for h in hs]
    decay = [jnp.exp(bc[h][n - 1:n] + m0[h] - m_new[h]) for h in hs]
    c_new = [decay[h] * c0[h] + _dot(kw[h].T.astype(BF16), v16[h]) for h in hs]
    n_new = [decay[h] * n0[h] + jnp.sum(kw[h], axis=0, keepdims=True) for h in hs]
    for h in hs:
        c_sc[h] = c_new[h]
        n_sc[h] = n_new[h]
        m_sc[h] = m_new[h]
        c_out[h] = c_new[h]
        n_out[h] = n_new[h]
        m_out[h] = m_new[h]
        hn = hh[h] * lax.rsqrt(jnp.mean(hh[h] * hh[h], axis=-1, keepdims=True) + EPS) * ng_ref[:, sv[h]]
        o = jax.nn.sigmoid(_pad_rows(og_ref[:, sv[h]], n)) * hn
        o_ref[:, sv[h]] = o[:lin].astype(o_ref.dtype)


def mlstm_scan(z, bias_row, norm_g, c0, n0, m0, nb_batch, nqp, nvalid):
    lin = min(CHUNK, nqp)
    nc = nqp // lin
    tok = lambda w, off: pl.BlockSpec((lin, w), lambda b, c: (b * nc + c, off // w))
    body = functools.partial(_mlstm_body, nvalid=min(nvalid, CHUNK))
    st = lambda shape: pl.BlockSpec((None, ML_H) + shape, lambda b, c: (b, 0, 0, 0))
    wq, wv = ML_H * ML_DK, ML_H * ML_DV
    return pl.pallas_call(
        body, grid=(nb_batch, nc),
        in_specs=[tok(wq, Z_MQ), tok(wq, Z_MK), tok(wv, Z_MV), tok(wv, Z_MO), tok(LANE, Z_SM),
                  pl.BlockSpec((1, LANE), lambda b, c: (0, 0)), pl.BlockSpec((1, wv), lambda b, c: (0, 0)),
                  st((ML_DK, ML_DV)), st((1, ML_DK)), st((1, 1))],
        out_specs=[pl.BlockSpec((lin, wv), lambda b, c: (b * nc + c, 0)),
                   st((ML_DK, ML_DV)), st((1, ML_DK)), st((1, 1))],
        out_shape=[jax.ShapeDtypeStruct((nb_batch * nqp, wv), BF16),
                   jax.ShapeDtypeStruct((nb_batch, ML_H, ML_DK, ML_DV), F32),
                   jax.ShapeDtypeStruct((nb_batch, ML_H, 1, ML_DK), F32),
                   jax.ShapeDtypeStruct((nb_batch, ML_H, 1, 1), F32)],
        scratch_shapes=[pltpu.VMEM((ML_H, ML_DK, ML_DV), F32), pltpu.VMEM((ML_H, 1, ML_DK), F32),
                        pltpu.VMEM((ML_H, 1, 1), F32)],
        compiler_params=_cp(("parallel", "arbitrary")),
        name="mlstm_scan")(z, z, z, z, z, bias_row, norm_g.reshape(1, wv), c0, n0, m0)


def _gdn_body(q_ref, k_ref, v_ref, gz_ref, zs_ref, b_ref, a_ref, ng_ref, s0_ref,
              o_ref, s_out, s_sc, *, nvalid):
    ck = pl.program_id(1)
    n = CHUNK
    lin = q_ref.shape[0]

    @pl.when(ck == 0)
    def _():
        s_sc[...] = s0_ref[...]

    zs = _pad_rows(zs_ref[...], n)
    lane = _iota((1, LANE), 1)
    is_a = (lane >= SM_GA) & (lane < SM_GA + GD_H)
    g_all = jnp.where(is_a, a_ref[...] * _softplus(zs + b_ref[...]), 0.0)
    beta_all = jax.nn.sigmoid(zs)
    if nvalid < n:
        live = _iota((n, 1), 0) < nvalid
        g_all = jnp.where(live, g_all, 0.0)
        beta_all = jnp.where(live, beta_all, 0.0)
    gc_all = _cumsum_rows(g_all)
    gc_t = gc_all.T
    incl = _iota((n, n), 0) >= _iota((n, n), 1)
    strict = _iota((n, n), 0) > _iota((n, n), 1)
    eye = jnp.where(_iota((n, n), 0) == _iota((n, n), 1), 1.0, 0.0)
    hs = range(GD_H)
    sl = [slice(h * GD_D, (h + 1) * GD_D) for h in hs]
    s0 = [s_sc[h] for h in hs]
    q16 = [_pad_rows(q_ref[:, sl[h]], n).astype(BF16) for h in hs]
    k = [_pad_rows(k_ref[:, sl[h]], n) for h in hs]
    k16 = [k[h].astype(BF16) for h in hs]
    gc = [gc_all[:, SM_GA + h:SM_GA + h + 1] for h in hs]
    beta = [beta_all[:, SM_GB + h:SM_GB + h + 1] for h in hs]
    dec = [jnp.where(incl, jnp.exp(jnp.where(incl, gc[h] - gc_t[SM_GA + h:SM_GA + h + 1, :], 0.0)), 0.0) for h in hs]
    egc = [jnp.exp(gc[h]) for h in hs]
    kb = [k[h] * beta[h] for h in hs]
    xp = [-jnp.where(strict, _dot_nt(kb[h].astype(BF16), k16[h]) * dec[h], 0.0) for h in hs]
    tinv = [eye + xp[h] for h in hs]
    for _ in range(int(math.log2(n)) - 1):
        xp = [_dot3(xp[h], xp[h]) for h in hs]
        tinv = [tinv[h] + _dot3(tinv[h], xp[h]) for h in hs]
    rhs = [jnp.concatenate([_pad_rows(v_ref[:, sl[h]], n) * beta[h], kb[h] * egc[h]], axis=1) for h in hs]
    sol = [_dot3(tinv[h], rhs[h]) for h in hs]
    s016 = [s0[h].astype(BF16) for h in hs]
    u16 = [(sol[h][:, :GD_D] - _dot(sol[h][:, GD_D:].astype(BF16), s016[h])).astype(BF16) for h in hs]
    qk = [(_dot_nt(q16[h], k16[h]) * dec[h]).astype(BF16) for h in hs]
    o = [_dot(qk[h], u16[h]) + egc[h] * _dot(q16[h], s016[h]) for h in hs]
    kw = [(k[h] * jnp.exp(gc[h][n - 1:n] - gc[h])).T.astype(BF16) for h in hs]
    s_new = [jnp.exp(gc[h][n - 1:n]) * s0[h] + _dot(kw[h], u16[h]) for h in hs]
    for h in hs:
        s_sc[h] = s_new[h]
        s_out[h] = s_new[h]
        on = o[h] * lax.rsqrt(jnp.mean(o[h] * o[h], axis=-1, keepdims=True) + EPS) * ng_ref[...]
        o_ref[:, sl[h]] = (on * _silu(_pad_rows(gz_ref[:, sl[h]], n)))[:lin].astype(o_ref.dtype)


def gdn_scan(qkv, z, dt_row, nega_row, norm_g, s0, nb_batch, nqp, nvalid):
    lin = min(CHUNK, nqp)
    nc = nqp // lin
    w = GD_H * GD_D
    body = functools.partial(_gdn_body, nvalid=min(nvalid, CHUNK))
    tokq = lambda part: pl.BlockSpec((lin, w), lambda b, c: (b * nc + c, part))
    st = pl.BlockSpec((None, GD_H, GD_D, GD_D), lambda b, c: (b, 0, 0, 0))
    row = pl.BlockSpec((1, LANE), lambda b, c: (0, 0))
    return pl.pallas_call(
        body, grid=(nb_batch, nc),
        in_specs=[tokq(0), tokq(1), tokq(2),
                  pl.BlockSpec((lin, w), lambda b, c: (b * nc + c, Z_GZ // w)),
                  pl.BlockSpec((lin, LANE), lambda b, c: (b * nc + c, Z_SM // LANE)),
                  row, row, row, st],
        out_specs=[pl.BlockSpec((lin, w), lambda b, c: (b * nc + c, 0)), st],
        out_shape=[jax.ShapeDtypeStruct((nb_batch * nqp, w), BF16),
                   jax.ShapeDtypeStruct((nb_batch, GD_H, GD_D, GD_D), F32)],
        scratch_shapes=[pltpu.VMEM((GD_H, GD_D, GD_D), F32)],
        compiler_params=_cp(("parallel", "arbitrary")),
        name="gdn_scan")(qkv, qkv, qkv, z, z, dt_row, nega_row, norm_g.reshape(1, GD_D), s0)


HALO = 8


def _conv_rows(buf, x, hist_ref, w_ref, width, first):
    tb = x.shape[0]

    @pl.when(first)
    def _():
        buf[0:HALO, :] = hist_ref[...]

    buf[HALO:HALO + tb, :] = x
    y = w_ref[width - 1:width, :] * x
    for kk in range(1, width):
        y = y + w_ref[width - 1 - kk:width - kk, :] * buf[HALO - kk:HALO - kk + tb, :]
    return y


def _gdn_prep_body(z_ref, hist_ref, w_ref, o_ref, buf):
    ct = pl.program_id(1)
    tb = z_ref.shape[0]
    x = z_ref[...]
    y = _silu(_conv_rows(buf, x, hist_ref, w_ref, GDN_CONV, pl.program_id(2) == 0))
    buf[0:HALO, :] = x[tb - HALO:tb, :]
    part = ct // (GD_H * GD_D // y.shape[1])
    qscale = jnp.where(part == 0, GD_D ** -0.5, 1.0)
    for hh in range(y.shape[1] // GD_D):
        sl = slice(hh * GD_D, (hh + 1) * GD_D)
        yh = y[:, sl]
        nrm = yh * lax.rsqrt(jnp.sum(yh * yh, axis=-1, keepdims=True) + EPS) * qscale
        o_ref[:, sl] = jnp.where(part == 2, yh, nrm)


def gdn_prep(z, hist, w, nb_batch, nqp, tb, tc=512):
    nblk = nqp // tb
    ncol = 3 * GD_H * GD_D
    return pl.pallas_call(
        _gdn_prep_body, grid=(nb_batch, ncol // tc, nblk),
        in_specs=[pl.BlockSpec((tb, tc), lambda b, c, i: (b * nblk + i, Z_GQKV // tc + c)),
                  pl.BlockSpec((None, HALO, tc), lambda b, c, i: (b, 0, c)),
                  pl.BlockSpec((GDN_CONV, tc), lambda b, c, i: (0, c))],
        out_specs=pl.BlockSpec((tb, tc), lambda b, c, i: (b * nblk + i, c)),
        out_shape=jax.ShapeDtypeStruct((nb_batch * nqp, ncol), F32),
        scratch_shapes=[pltpu.VMEM((HALO + tb, tc), F32)],
        compiler_params=_cp(("parallel", "parallel", "arbitrary")), name="gdn_prep")(z, hist, w)


def _ffn_up_body(h_ref, wa_ref, wb_ref, ca_ref, cb_ref, ha_ref, hb_ref, act_ref, ta_ref, tb_ref, bufa, bufb,
                 *, bps, tail0):
    i = pl.program_id(1)
    tm = h_ref.shape[0]
    first = (i % bps) == 0
    h = h_ref[...]
    ys = []
    for w_ref, c_ref, hist_ref, t_ref, buf in ((wa_ref, ca_ref, ha_ref, ta_ref, bufa),
                                               (wb_ref, cb_ref, hb_ref, tb_ref, bufb)):
        u = _dot(h, w_ref[...])
        ys.append(_conv_rows(buf, u, hist_ref, c_ref, FFN_CONV, first))
        buf[0:HALO, :] = u[tm - HALO:tm, :]
        t_ref[...] = u[tail0:tail0 + HALO, :]
    act_ref[...] = (_silu(ys[0]) * ys[1]).astype(act_ref.dtype)


def ffn_up_act(h, w_up, conv_w, hist, nb_batch, nq, nqp, tm, tn):
    t, d = h.shape
    nj = FFN_PAD // tn
    bps = nqp // tm
    assert (nq - 1) // tm == bps - 1 and (nq - 1) % HALO >= FFN_CONV - 2
    tail0 = ((nq - 1) % tm) // HALO * HALO
    body = functools.partial(_ffn_up_body, bps=bps, tail0=tail0)
    col = lambda half: (lambda j, i: (0, half * nj + j))
    seq = lambda half: (lambda j, i: (i // bps, 0, half * nj + j))
    tail = jax.ShapeDtypeStruct((nb_batch, HALO, FFN_PAD), F32)
    return pl.pallas_call(
        body, grid=(nj, t // tm),
        in_specs=[pl.BlockSpec((tm, d), lambda j, i: (i, 0)),
                  pl.BlockSpec((d, tn), col(0)), pl.BlockSpec((d, tn), col(1)),
                  pl.BlockSpec((FFN_CONV, tn), col(0)), pl.BlockSpec((FFN_CONV, tn), col(1)),
                  pl.BlockSpec((None, HALO, tn), seq(0)), pl.BlockSpec((None, HALO, tn), seq(1))],
        out_specs=[pl.BlockSpec((tm, tn), lambda j, i: (i, j)),
                   pl.BlockSpec((None, HALO, tn), lambda j, i: (i // bps, 0, j)),
                   pl.BlockSpec((None, HALO, tn), lambda j, i: (i // bps, 0, j))],
        out_shape=[jax.ShapeDtypeStruct((t, FFN_PAD), BF16), tail, tail],
        scratch_shapes=[pltpu.VMEM((HALO + tm, tn), F32), pltpu.VMEM((HALO + tm, tn), F32)],
        compiler_params=_cp(("parallel", "arbitrary")),
        name="ffn_up_act")(h, w_up, w_up, conv_w, conv_w, hist, hist)


def _merge_body(oa_ref, ob_ref, oc_ref, w_ref, g0_ref, g1_ref, g2_ref, o_ref):
    acc = jax.nn.sigmoid(g0_ref[...]) * _dot(oa_ref[...], w_ref[0])
    acc = acc + jax.nn.sigmoid(g1_ref[...]) * _dot(ob_ref[...], w_ref[1])
    acc = acc + jax.nn.sigmoid(g2_ref[...]) * _dot(oc_ref[...], w_ref[2])
    o_ref[...] = acc.astype(o_ref.dtype)


def merge_branches(oa, ob, oc, wb, z, tm, tn):
    t = oa.shape[0]
    d = wb.shape[2]
    tok = pl.BlockSpec((tm, 1024), lambda i, j: (i, 0))
    gate = lambda nbr: pl.BlockSpec((tm, tn), lambda i, j: (i, (Z_BR + nbr * d) // tn + j))
    return pl.pallas_call(
        _merge_body, grid=(t // tm, d // tn),
        in_specs=[tok, tok, tok, pl.BlockSpec((3, 1024, tn), lambda i, j: (0, 0, j)), gate(0), gate(1), gate(2)],
        out_specs=pl.BlockSpec((tm, tn), lambda i, j: (i, j)),
        out_shape=jax.ShapeDtypeStruct((t, d), BF16),
        compiler_params=_cp(("parallel", "parallel")), name="merge_branches")(oa, ob, oc, wb, z, z, z)


def _layer_weights(p):
    w_in = p['w_in']
    offs = np.cumsum([0, 1024, 1536, 24, 512, 512, 1024, 4, 4, 1024, 3072, 8, 8, 1024, 6144])
    seg = lambda i: w_in[:, offs[i]:offs[i + 1]]
    kv = seg(1).reshape(-1, 3, 2, NSA_G * DH)
    small = jnp.concatenate([seg(2), seg(6), seg(7), seg(10), seg(11)], axis=1)
    small = jnp.pad(small, ((0, 0), (0, Z_COLS - Z_SM - small.shape[1])))
    gap = jnp.zeros((w_in.shape[0], Z_KV - Z_MQ - ML_H * ML_DK), w_in.dtype)
    w_all = jnp.concatenate([seg(0), seg(5), seg(8), seg(12), seg(9), seg(13), seg(3), gap,
                             kv[:, :, 0].reshape(-1, KV_W // 2), kv[:, :, 1].reshape(-1, KV_W // 2),
                             seg(4), small], axis=1)
    assert w_all.shape[1] == Z_COLS
    w1 = p['w_cmp1']
    wcat = jnp.concatenate([w1[:, :CMP_STRIDE].reshape(2, CMP_STRIDE * DH, DH),
                            w1[:, CMP_STRIDE:].reshape(2, CMP_STRIDE * DH, DH)], axis=2).astype(BF16)

    def lane_row(pairs):
        row = jnp.zeros((LANE,), F32)
        for off, v in pairs:
            row = row.at[off:off + v.shape[0]].set(v.astype(F32))
        return row.reshape(1, LANE)

    f = p['w_down'].shape[0]
    fpad = FFN_PAD - f
    pad_halves = lambda a: jnp.concatenate(
        [jnp.pad(a[..., :f], ((0, 0),) * (a.ndim - 1) + ((0, fpad),)),
         jnp.pad(a[..., f:], ((0, 0),) * (a.ndim - 1) + ((0, fpad),))], axis=-1)
    return dict(
        w_up=pad_halves(p['w_up']).astype(BF16), ffn_conv_w=pad_halves(p['ffn_conv_w']),
        w_down=jnp.pad(p['w_down'], ((0, fpad), (0, 0))).astype(BF16), pad_halves=pad_halves,
        w_all=w_all.astype(BF16), wcat=wcat,
        pe8=jnp.pad(p['cmp_pe'].reshape(2, 1, CMP_BLOCK * DH), ((0, 0), (0, 7), (0, 0))).astype(BF16),
        w1flat=w1.reshape(2, CMP_BLOCK * DH, DH).astype(BF16), w2=p['w_cmp2'].astype(BF16),
        ml_bias=lane_row([(SM_MI, p['mlstm_b_i']), (SM_MF, p['mlstm_b_f'])]),
        gd_dt=lane_row([(SM_GA, p['gdn_dt_bias'])]),
        gd_nega=lane_row([(SM_GA, -jnp.exp(p['gdn_a_log'].astype(F32)))]),
        w_branch=p['w_branch'].astype(BF16), w_out=p['w_out'].astype(BF16))


def _hist_rows(state):
    return jnp.pad(state, ((0, 0), (HALO - state.shape[1], 0), (0, 0)))


def _group_layer(x, grp, st, p, w, layer):
    nb, nq, nqp, pos0 = grp['nb'], grp['nq'], grp['nqp'], grp['pos0']
    t = nb * nqp
    big = nqp >= 512
    tm = 1024 if big else t
    h = rmsnorm_bf16(x, p['norm1_g'], min(tm, 256))
    z = matmul(h, w['w_all'], tm, 1024 if big else 2048, name="in_proj")

    tb = 256 if big else nqp
    q_rot, rows, kvs, win_new = nsa_prep(z, grp['rope'], p['q_norm_g'], p['k_norm_g'], nqp, tb)
    if pos0 == 0:
        n_pages = nqp // PAGE
        fs = compress_pages(rows.reshape(nb * n_pages, PAGE_SUB, LANE), w['wcat'], nb, n_pages, 8)
        newpage = None
    else:
        live = (jnp.arange(nqp * ROW_SUB) < nq * ROW_SUB)[None, :, None]
        newpage = jnp.pad(jnp.where(live, rows.reshape(nb, nqp * ROW_SUB, LANE), 0.0),
                          ((0, 0), (0, PAGE_SUB - nqp * ROW_SUB), (0, 0)))
        fs_past = compress_pages(st['cache'], w['wcat'], nb, st['table'].shape[1], 8,
                                 table=st['table'], layer=layer)
        fs_new = compress_pages(newpage, w['wcat'], nb, 1, 1)
        fs = jnp.concatenate([fs_past, fs_new], axis=1)
    nblk_c = fs.shape[1]
    ncp = -(-nblk_c // LANE) * LANE
    fs = fs.reshape(nb, nblk_c, 4, 2, DH)
    first = jnp.pad(fs[:, :, :, 0].reshape(nb, nblk_c, 512), ((0, 0), (0, ncp - nblk_c), (0, 0)))
    second = jnp.pad(fs[:, 1:, :, 1].reshape(nb, nblk_c - 1, 512), ((0, 0), (0, ncp - nblk_c + 1), (0, 0)))
    kvc = compress_finish(first, second, w['pe8'], w['w1flat'], w['w2'])
    ns = nblk_c * CMP_STRIDE // SLC_BLOCK
    nsp = -(-ns // LANE) * LANE
    t_all = pos0 + nq
    nsel = min(N_SELECT, -(-t_all // SLC_BLOCK))
    qb = 64 if big else nqp
    o_cmp, sel = cmp_select(q_rot, kvc, nb, nqp, 2 * qb if big else qb, pos0, ns, nsp, nsel)
    emat = expand_matrices()
    if pos0 == 0:
        o_slc = slc_attention(q_rot, sel, emat, nb, nqp, qb, pos0, nsp, kvs=kvs)
    else:
        o_slc = slc_attention(q_rot, sel, emat, nb, nqp, qb, pos0, nsp, cache=st['cache'], table=st['table'],
                              layer=layer, newpage=newpage)
    win3 = win_new.reshape(nb, nqp, 512)
    win_all = jnp.concatenate([st['win'], win3], axis=1)
    o_win = win_attention(q_rot, win_all, nb, nqp, qb, pos0)
    o_nsa = nsa_combine(o_cmp, o_slc, o_win, z, tb)
    win_state = jnp.concatenate([st['win'], win3[:, :nq]], axis=1)[:, -WINDOW:]

    o_ml, m_c, m_n, m_m = mlstm_scan(z, w['ml_bias'], p['mlstm_norm_g'], st['m_c'], st['m_n'], st['m_m'],
                                     nb, nqp, nq)

    tbc = 512 if big else nqp
    qkv = gdn_prep(z, _hist_rows(st['g_conv']), p['gdn_conv_w'], nb, nqp, tbc)
    o_gd, g_s = gdn_scan(qkv, z, w['gd_dt'], w['gd_nega'], p['gdn_norm_g'], st['g_s'], nb, nqp, nq)
    z3 = z.reshape(nb, nqp, Z_COLS)
    g_conv_new = z3[:, nq - (GDN_CONV - 1):nq, Z_GQKV:Z_GQKV + 3 * GD_H * GD_D]

    mrg = merge_branches(o_nsa, o_ml, o_gd, w['w_branch'], z, tm, 512)
    x1 = matmul(mrg, w['w_out'], tm, 1024 if big else 2048, res=x, name="out_proj")

    h2 = rmsnorm_bf16(x1, p['norm2_g'], min(tm, 256))
    tmf = 1024 if big else nqp
    act, tail_a, tail_b = ffn_up_act(h2, w['w_up'], w['ffn_conv_w'], _hist_rows(w['pad_halves'](st['f_conv'])),
                                     nb, nq, nqp, tmf, 512)
    x2 = matmul(act, w['w_down'], tm, 512, res=x1, name="down_proj")
    f = p['w_down'].shape[0]
    r0 = (nq - (FFN_CONV - 1)) % tmf - ((nq - 1) % tmf) // HALO * HALO
    f_conv_new = jnp.concatenate([tail_a[:, r0:r0 + FFN_CONV - 1, :f], tail_b[:, r0:r0 + FFN_CONV - 1, :f]], axis=-1)

    new_rows = rows.reshape(nb, nqp, 4, NSA_G, DH)[:, :nq]
    states = (new_rows, win_state.reshape(nb, WINDOW, 2, NSA_G, DH), m_c, m_n.reshape(nb, ML_H, ML_DK),
              m_m.reshape(nb, ML_H), g_conv_new, g_s, f_conv_new)
    return x2, states


def kernel(x_prompt, x_sample, cache_nsa_kv, page_table, cache_nsa_win, state_mlstm_c, state_mlstm_n,
           state_mlstm_m, state_gdn_conv, state_gdn_s, state_ffn_conv, norm1_g, w_in, q_norm_g, k_norm_g,
           w_cmp1, cmp_pe, w_cmp2, mlstm_b_i, mlstm_b_f, mlstm_norm_g, gdn_conv_w, gdn_a_log, gdn_dt_bias,
           gdn_norm_g, w_branch, w_out, norm2_g, w_up, ffn_conv_w, w_down):
    depth = w_in.shape[0]
    bp, sp, d = x_prompt.shape
    bs, ss, _ = x_sample.shape
    n_pages = page_table.shape[1]
    past = n_pages * PAGE
    ssp = -(-ss // 16) * 16
    assert sp % KC == 0 and ss >= GDN_CONV - 1 and cache_nsa_win.shape[2] == WINDOW and n_pages % (KC // PAGE) == 0
    grp_p = dict(nb=bp, nq=sp, nqp=sp, pos0=0, rope=rope_tables(0, sp))
    grp_s = dict(nb=bs, nq=ss, nqp=ssp, pos0=past, rope=rope_tables(past, ssp))
    cache = cache_nsa_kv.reshape(depth, cache_nsa_kv.shape[1], PAGE_SUB, LANE)
    xp = x_prompt.reshape(bp * sp, d)
    xs = jnp.pad(x_sample, ((0, 0), (0, ssp - ss), (0, 0))).reshape(bs * ssp, d)
    st_p, st_s = [], []
    for l in range(depth):
        p = dict(norm1_g=norm1_g[l], w_in=w_in[l], q_norm_g=q_norm_g[l], k_norm_g=k_norm_g[l],
                 w_cmp1=w_cmp1[l], cmp_pe=cmp_pe[l], w_cmp2=w_cmp2[l], mlstm_b_i=mlstm_b_i[l],
                 mlstm_b_f=mlstm_b_f[l], mlstm_norm_g=mlstm_norm_g[l], gdn_conv_w=gdn_conv_w[l],
                 gdn_a_log=gdn_a_log[l], gdn_dt_bias=gdn_dt_bias[l], gdn_norm_g=gdn_norm_g[l],
                 w_branch=w_branch[l], w_out=w_out[l], norm2_g=norm2_g[l], w_up=w_up[l],
                 ffn_conv_w=ffn_conv_w[l], w_down=w_down[l])
        w = _layer_weights(p)
        zero_p = dict(win=jnp.zeros((bp, WINDOW, 512), F32),
                      m_c=jnp.zeros((bp, ML_H, ML_DK, ML_DV), F32), m_n=jnp.zeros((bp, ML_H, 1, ML_DK), F32),
                      m_m=jnp.zeros((bp, ML_H, 1, 1), F32),
                      g_conv=jnp.zeros((bp, GDN_CONV - 1, 3 * GD_H * GD_D), F32),
                      g_s=jnp.zeros((bp, GD_H, GD_D, GD_D), F32),
                      f_conv=jnp.zeros((bp, FFN_CONV - 1, state_ffn_conv.shape[-1]), F32))
        xp, s_p = _group_layer(xp, grp_p, zero_p, p, w, l)
        st = dict(cache=cache, table=page_table, win=cache_nsa_win[l].reshape(bs, WINDOW, 512),
                  m_c=state_mlstm_c[l], m_n=state_mlstm_n[l].reshape(bs, ML_H, 1, ML_DK),
                  m_m=state_mlstm_m[l].reshape(bs, ML_H, 1, 1), g_conv=state_gdn_conv[l],
                  g_s=state_gdn_s[l], f_conv=state_ffn_conv[l])
        xs, s_s = _group_layer(xs, grp_s, st, p, w, l)
        st_p.append(s_p)
        st_s.append(s_s)
    outs = [xp.reshape(bp, sp, d), xs.reshape(bs, ssp, d)[:, :ss]]
    for i in range(8):
        outs.append(jnp.stack([s[i] for s in st_p], axis=0))
        outs.append(jnp.stack([s[i] for s in st_s], axis=0))
    return tuple(outs)
```

```python
import functools
import math

import numpy as np
import jax
import jax.numpy as jnp
from jax import lax
from jax.experimental import pallas as pl
from jax.experimental.pallas import tpu as pltpu

F32 = jnp.float32
BF16 = jnp.bfloat16
HI = lax.Precision.HIGHEST

EPS = 1e-6
NEG_INF = -1e30
LANE = 128
SUBLANE = 8
PAGE = 128
DH = 128
NSA_HEADS = 8
NSA_G = 2
NSA_GROUP = NSA_HEADS // NSA_G
ROPE_DIM = DH // 4
ROPE_THETA = 500000.0
CMP_BLOCK = 32
CMP_STRIDE = 16
SLC_BLOCK = 64
N_SELECT = 16
WINDOW = 512
ML_H, ML_DK, ML_DV = 4, 128, 256
GD_H, GD_D = 8, 128
GDN_CONV = 4
FFN_CONV = 3
CHUNK = 64
FFN_PAD = 5632
KC = 1024
BLK_PER_KC = KC // SLC_BLOCK
WIN_PER_LANE = LANE // BLK_PER_KC
VMEM_LIMIT = 56 * 1024 * 1024

Z_Q, Z_MV, Z_MO, Z_GZ = 0, 1024, 2048, 3072
Z_GQKV, Z_BR = 4096, 7168
Z_MQ, Z_KV, Z_MK, Z_SM = 13312, 13824, 15360, 15872
Z_COLS = 16384
KV_W = 1536
ROW_W = 4 * NSA_G * DH
ROW_SUB = ROW_W // LANE
PAGE_SUB = PAGE * ROW_SUB
SM_NG, SM_MI, SM_MF, SM_GA, SM_GB = 0, 24, 28, 32, 40


def _cp(sem):
    return pltpu.CompilerParams(dimension_semantics=sem, vmem_limit_bytes=VMEM_LIMIT)


def _iota(shape, dim):
    return lax.broadcasted_iota(jnp.int32, shape, dim)


def _dot(a, b, **kw):
    return jnp.dot(a, b, preferred_element_type=F32, **kw)


def _dot_nt(a, b, **kw):
    return lax.dot_general(a, b, (((1,), (1,)), ((), ())), preferred_element_type=F32, **kw)


def _silu(x):
    return x * jax.nn.sigmoid(x)


def _split(x):
    hi = x.astype(BF16)
    return hi, (x - hi.astype(F32)).astype(BF16)


def _dot3(a, b):
    ah, al = _split(a)
    bh, bl = _split(b)
    return _dot(ah, bh) + (_dot(ah, bl) + _dot(al, bh))


def _rmsnorm_body(x_ref, g_ref, o_ref):
    x = x_ref[...]
    y = x * lax.rsqrt(jnp.mean(x * x, axis=-1, keepdims=True) + EPS)
    o_ref[...] = (y * g_ref[...]).astype(o_ref.dtype)


def rmsnorm_bf16(x, g, tm):
    t, d = x.shape
    return pl.pallas_call(
        _rmsnorm_body, grid=(t // tm,),
        in_specs=[pl.BlockSpec((tm, d), lambda i: (i, 0)), pl.BlockSpec((1, d), lambda i: (0, 0))],
        out_specs=pl.BlockSpec((tm, d), lambda i: (i, 0)),
        out_shape=jax.ShapeDtypeStruct((t, d), BF16),
        compiler_params=_cp(("parallel",)), name="rmsnorm")(x, g.reshape(1, d))


def _mm_body(a_ref, w_ref, o_ref):
    o_ref[...] = _dot(a_ref[...], w_ref[...]).astype(o_ref.dtype)


def _mm_res_body(a_ref, w_ref, r_ref, o_ref):
    o_ref[...] = r_ref[...] + _dot(a_ref[...], w_ref[...])


def matmul(a, w, tm, tn, res=None, name="matmul"):
    m, k = a.shape
    n = w.shape[1]
    in_specs = [pl.BlockSpec((tm, k), lambda i, j: (i, 0)), pl.BlockSpec((k, tn), lambda i, j: (0, j))]
    args = [a, w]
    body = _mm_body
    if res is not None:
        in_specs.append(pl.BlockSpec((tm, tn), lambda i, j: (i, j)))
        args.append(res)
        body = _mm_res_body
    return pl.pallas_call(
        body, grid=(m // tm, n // tn), in_specs=in_specs,
        out_specs=pl.BlockSpec((tm, tn), lambda i, j: (i, j)),
        out_shape=jax.ShapeDtypeStruct((m, n), F32),
        compiler_params=_cp(("parallel", "parallel")), name=name)(*args)


def _nsa_prep_body(zq_ref, zkv_ref, c_ref, s1_ref, s2_ref, qg_ref, kg_ref, q_out, rows_out, kvs_out, win_out):
    c = c_ref[...]
    s1 = s1_ref[...]
    s2 = s2_ref[...]
    tb = zq_ref.shape[0]

    def norm_rope(x, g):
        y = x * lax.rsqrt(jnp.mean(x * x, axis=-1, keepdims=True) + EPS) * g
        return y * c + pltpu.roll(y, LANE - ROPE_DIM // 2, 1) * s1 + pltpu.roll(y, ROPE_DIM // 2, 1) * s2

    for h in range(NSA_HEADS):
        sl = slice(h * DH, (h + 1) * DH)
        q_out[:, sl] = (norm_rope(zq_ref[:, sl], qg_ref[...]) * (DH ** -0.5)).astype(q_out.dtype)
    for br in range(3):
        for g in range(NSA_G):
            col = (br * NSA_G + g) * DH
            kk = norm_rope(zkv_ref[:, col:col + DH], kg_ref[br:br + 1, :])
            vv = zkv_ref[:, KV_W // 2 + col:KV_W // 2 + col + DH]
            if br < 2:
                rows_out[pl.ds(4 * br + g, tb, stride=ROW_SUB), :] = kk
                rows_out[pl.ds(4 * br + 2 + g, tb, stride=ROW_SUB), :] = vv
            else:
                win_out[:, g * DH:(g + 1) * DH] = kk
                win_out[:, 2 * DH + g * DH:2 * DH + (g + 1) * DH] = vv
            if br == 1:
                kvs_out[:, g * DH:(g + 1) * DH] = kk.astype(kvs_out.dtype)
                kvs_out[:, (2 + g) * DH:(3 + g) * DH] = vv.astype(kvs_out.dtype)


def nsa_prep(z, tabs, q_g, k_g, nqp, tb):
    t = z.shape[0]
    nblk = nqp // tb
    tab_spec = pl.BlockSpec((tb, LANE), lambda i: (i % nblk, 0))
    return pl.pallas_call(
        _nsa_prep_body, grid=(t // tb,),
        in_specs=[pl.BlockSpec((tb, 1024), lambda i: (i, Z_Q // 1024)),
                  pl.BlockSpec((tb, KV_W), lambda i: (i, Z_KV // KV_W)), tab_spec, tab_spec, tab_spec,
                  pl.BlockSpec((1, DH), lambda i: (0, 0)), pl.BlockSpec((3, DH), lambda i: (0, 0))],
        out_specs=[pl.BlockSpec((tb, 1024), lambda i: (i, 0)), pl.BlockSpec((tb * ROW_SUB, LANE), lambda i: (i, 0)),
                   pl.BlockSpec((tb, 512), lambda i: (i, 0)), pl.BlockSpec((tb, 512), lambda i: (i, 0))],
        out_shape=[jax.ShapeDtypeStruct((t, 1024), BF16), jax.ShapeDtypeStruct((t * ROW_SUB, LANE), F32),
                   jax.ShapeDtypeStruct((t, 512), BF16), jax.ShapeDtypeStruct((t, 512), F32)],
        compiler_params=_cp(("parallel",)), name="nsa_prep")(z, z, *tabs, q_g.reshape(1, DH), k_g)


def rope_tables(pos0, nqp):
    half = ROPE_DIM // 2
    pos = pos0 + jnp.arange(nqp, dtype=jnp.int32)
    inv = ROPE_THETA ** (-jnp.arange(half, dtype=F32) / half)
    ang = pos.astype(F32)[:, None] * inv[None, :]
    cos, sin = jnp.cos(ang), jnp.sin(ang)
    zeros = jnp.zeros((nqp, LANE - ROPE_DIM), F32)
    c = jnp.concatenate([cos, cos, jnp.ones((nqp, LANE - ROPE_DIM), F32)], axis=1)
    s1 = jnp.concatenate([-sin, jnp.zeros((nqp, half), F32), zeros], axis=1)
    s2 = jnp.concatenate([jnp.zeros((nqp, half), F32), sin, zeros], axis=1)
    return c, s1, s2


def _compress_body(*refs, pps, paged, new_step):
    if paged:
        refs = refs[1:]
    slots = refs[:pps]
    k = pps
    new_ref = refs[k] if paged else None
    k += 1 if paged else 0
    w_refs = refs[k:k + 2]
    o_ref = refs[k + 2]
    kvs_ref = refs[k + 3] if paged else None
    i = pl.program_id(1)
    nb = PAGE // CMP_STRIDE

    def rows(idx, start, size, stride):
        v = slots[idx][pl.ds(start, size, stride=stride), :]
        if paged and idx == 0:
            v = jnp.where(i == new_step, new_ref[pl.ds(start, size, stride=stride), :], v)
        return v

    for kind in range(2):
        xs = []
        for g in range(NSA_G):
            part = kind * 2 + g
            per_slot = [jnp.concatenate([rows(s, ROW_SUB * j + part, nb, CMP_STRIDE * ROW_SUB)
                                         for j in range(CMP_STRIDE)], axis=1) for s in range(pps)]
            xs.append(jnp.concatenate(per_slot, axis=0) if pps > 1 else per_slot[0])
        x = jnp.concatenate(xs, axis=0).astype(BF16)
        fs = _dot(x, w_refs[kind][...])
        for g in range(NSA_G):
            o_ref[:, (kind * 2 + g) * 2 * DH:(kind * 2 + g + 1) * 2 * DH] = fs[g * nb * pps:(g + 1) * nb * pps]
    if paged:
        for s in range(pps):
            for q in range(4):
                kvs_ref[s * PAGE:(s + 1) * PAGE, q * DH:(q + 1) * DH] = rows(s, 4 + q, PAGE, ROW_SUB).astype(kvs_ref.dtype)


def compress_pages(src, wcat, nb_batch, steps, pps, table=None, layer=None, newpage=None):
    nb = PAGE // CMP_STRIDE
    w_spec = pl.BlockSpec((CMP_STRIDE * DH, 2 * DH), lambda b, i, *_: (0, 0))
    out_spec = pl.BlockSpec((None, nb * pps, 1024), lambda b, i, *_: (b, i, 0))
    out_shape = jax.ShapeDtypeStruct((nb_batch, steps * pps * nb, 1024), F32)
    if table is None:
        body = functools.partial(_compress_body, pps=pps, paged=False, new_step=None)
        slot_specs = [pl.BlockSpec((None, PAGE_SUB, LANE), functools.partial(
            lambda b, i, s: ((b * steps + i) * pps + s, 0, 0), s=s)) for s in range(pps)]
        return pl.pallas_call(
            body, grid=(nb_batch, steps), in_specs=slot_specs + [w_spec, w_spec], out_specs=out_spec,
            out_shape=out_shape, compiler_params=_cp(("parallel", "parallel")),
            name="compress")(*([src] * pps), wcat[0], wcat[1])
    n_pages = table.shape[1]
    body = functools.partial(_compress_body, pps=pps, paged=True, new_step=n_pages // pps)
    slot_specs = [pl.BlockSpec((None, None, PAGE_SUB, LANE), functools.partial(
        lambda b, i, tbl, s: (layer, tbl[b, jnp.minimum(i * pps + s, n_pages - 1)], 0, 0), s=s)) for s in range(pps)]
    new_spec = pl.BlockSpec((None, PAGE_SUB, LANE), lambda b, i, tbl: (b, 0, 0))
    kvs_spec = pl.BlockSpec((None, pps * PAGE, 512), lambda b, i, tbl: (b, i, 0))
    gs = pltpu.PrefetchScalarGridSpec(num_scalar_prefetch=1, grid=(nb_batch, steps),
                                      in_specs=slot_specs + [new_spec, w_spec, w_spec],
                                      out_specs=[out_spec, kvs_spec])
    return pl.pallas_call(body, grid_spec=gs,
                          out_shape=[out_shape, jax.ShapeDtypeStruct((nb_batch, steps * pps * PAGE, 512), BF16)],
                          compiler_params=_cp(("parallel", "parallel")),
                          name="compress_paged")(table, *([src] * pps), newpage, wcat[0], wcat[1])


def _cfin_body(fs_ref, pe_ref, w1_ref, w2_ref, o_ref):
    ncp = fs_ref.shape[0]
    last = _iota((ncp, 1), 0) == ncp - 1
    for kind in range(2):
        bias = _dot(pe_ref[kind], w1_ref[kind])[0:1, :]
        for g in range(NSA_G):
            base = (kind * 2 + g) * 2 * DH
            second_next = jnp.where(last, 0.0, pltpu.roll(fs_ref[:, base + DH:base + 2 * DH], ncp - 1, 0))
            hid = fs_ref[:, base:base + DH] + second_next + bias
            o_ref[:, (kind * 2 + g) * DH:(kind * 2 + g + 1) * DH] = _dot(
                _silu(hid).astype(BF16), w2_ref[kind]).astype(o_ref.dtype)


def compress_finish(fs, pe8, w1flat, w2):
    b, ncp, _ = fs.shape
    return pl.pallas_call(
        _cfin_body, grid=(b,),
        in_specs=[pl.BlockSpec((None, ncp, 1024), lambda i: (i, 0, 0)), pl.BlockSpec(pe8.shape, lambda i: (0, 0, 0)),
                  pl.BlockSpec(w1flat.shape, lambda i: (0, 0, 0)), pl.BlockSpec(w2.shape, lambda i: (0, 0, 0))],
        out_specs=pl.BlockSpec((None, ncp, 512), lambda i: (i, 0, 0)),
        out_shape=jax.ShapeDtypeStruct((b, ncp, 512), BF16),
        compiler_params=_cp(("parallel",)), name="compress_finish")(fs, pe8, w1flat, w2)


def _stack_heads(q_ref, g):
    return jnp.concatenate([q_ref[:, (g * NSA_GROUP + h) * DH:(g * NSA_GROUP + h + 1) * DH]
                            for h in range(NSA_GROUP)], axis=0)


def _unstack_heads(o_ref, g, o, qb):
    for h in range(NSA_GROUP):
        o_ref[:, (g * NSA_GROUP + h) * DH:(g * NSA_GROUP + h + 1) * DH] = o[h * qb:(h + 1) * qb]


def _rank(score, column, blk, ns):
    rank = jnp.zeros(score.shape, F32)
    for j in range(ns):
        sj = column(j)
        rank = rank + jnp.where(sj > score, 1.0, 0.0) + jnp.where(sj == score, jnp.where(blk > j, 1.0, 0.0), 0.0)
    return rank


def _cmp_body(q_ref, kv_ref, m_ref, o_ref, sel_ref, *, qb, pos0, ncp, ns, nsp, nsel, packed):
    i = pl.program_id(1)
    t1 = pos0 + i * qb + _iota((qb, 1), 0)
    vis = (_iota((1, ncp), 1) * CMP_STRIDE + (CMP_BLOCK - 1)) <= t1
    bias = jnp.concatenate([jnp.where(vis, 0.0, NEG_INF)] * NSA_GROUP, axis=0)
    any_vis = jnp.concatenate([jnp.where(t1 >= CMP_BLOCK - 1, 1.0, 0.0)] * NSA_GROUP, axis=0)
    pcs = []
    for g in range(NSA_G):
        s = _dot_nt(_stack_heads(q_ref, g), kv_ref[:, g * DH:(g + 1) * DH]) + bias
        e = jnp.exp(s - jnp.max(s, axis=-1, keepdims=True))
        pc = e * (any_vis / jnp.sum(e, axis=-1, keepdims=True))
        _unstack_heads(o_ref, g, _dot(pc.astype(BF16), kv_ref[:, (2 + g) * DH:(3 + g) * DH]), qb)
        pcs.append((pc[0:qb] + pc[qb:2 * qb]) + (pc[2 * qb:3 * qb] + pc[3 * qb:4 * qb]))
    cur = t1 // SLC_BLOCK

    def scores(imp, blk):
        valid = (blk * SLC_BLOCK <= t1) & (blk < ns)
        forced = (blk == 0) | (blk == cur) | (blk == cur - 1)
        return jnp.where(valid, jnp.where(forced, jnp.inf, imp), -jnp.inf)

    if packed:
        half = LANE // 2
        lane = _iota((1, LANE), 1)
        low = lane < half
        blk = lane & (half - 1)
        score = scores(_dot(pcs[0], m_ref[0], precision=HI) + _dot(pcs[1], m_ref[1], precision=HI), blk)
        st = score.T
        row8 = _iota((SUBLANE, 1), 0)
        tiles = []
        for hf in range(2):
            for v in range(half // SUBLANE):
                if SUBLANE * v >= ns:
                    tiles.append(jnp.full((SUBLANE, qb), -1.0, F32))
                    continue
                tile = st[hf * half + SUBLANE * v:hf * half + SUBLANE * (v + 1), :]
                acc = jnp.zeros((SUBLANE, qb), F32)
                for j in range(ns):
                    sj = st[hf * half + j:hf * half + j + 1, :]
                    ge = jnp.where(sj >= tile, 1.0, 0.0)
                    gt = jnp.where(sj > tile, 1.0, 0.0)
                    if SUBLANE * v > j:
                        acc = acc + ge
                    elif SUBLANE * v + SUBLANE - 1 < j:
                        acc = acc + gt
                    else:
                        acc = acc + jnp.where(row8 + SUBLANE * v > j, ge, gt)
                tiles.append(jnp.where(acc < nsel, 0.0, -1.0))
        selm = jnp.concatenate(tiles, axis=0).T
        sel_ref[:, 0:LANE] = jnp.where(low, selm, -1.0).astype(sel_ref.dtype)
        sel_ref[:, LANE:2 * LANE] = jnp.where(low, pltpu.roll(selm, half, 1), -1.0).astype(sel_ref.dtype)
    else:
        blk = _iota((1, nsp), 1)
        for g in range(NSA_G):
            score = scores(_dot(pcs[g], m_ref[0], precision=HI), blk)
            rank = _rank(score, lambda j: score[:, j:j + 1], blk, ns)
            sel_ref[:, g * nsp:(g + 1) * nsp] = jnp.where(rank < nsel, 0.0, -1.0).astype(sel_ref.dtype)


def cmp_select(q, kvc, nb_batch, nqp, qb, pos0, ns, nsp, nsel):
    ncp = kvc.shape[1]
    nblk = nqp // qb
    packed = 2 * ns <= LANE
    mmat = importance_matrix(ncp, nsp, packed)
    body = functools.partial(_cmp_body, qb=qb, pos0=pos0, ncp=ncp, ns=ns, nsp=nsp, nsel=nsel, packed=packed)
    return pl.pallas_call(
        body, grid=(nb_batch, nblk),
        in_specs=[pl.BlockSpec((qb, 1024), lambda b, i: (b * nblk + i, 0)),
                  pl.BlockSpec((None, ncp, 512), lambda b, i: (b, 0, 0)),
                  pl.BlockSpec(mmat.shape, lambda b, i: (0, 0, 0))],
        out_specs=[pl.BlockSpec((qb, 1024), lambda b, i: (b * nblk + i, 0)),
                   pl.BlockSpec((qb, NSA_G * nsp), lambda b, i: (b * nblk + i, 0))],
        out_shape=[jax.ShapeDtypeStruct((nb_batch * nqp, 1024), F32),
                   jax.ShapeDtypeStruct((nb_batch * nqp, NSA_G * nsp), BF16)],
        compiler_params=_cp(("parallel", "parallel")), name="cmp_select")(q, kvc, mmat)


def importance_matrix(ncp, nsp, packed):
    m = np.zeros((2 if packed else 1, ncp, nsp), np.float32)
    for j in range(nsp // 2 if packed else nsp):
        for n, w in ((4 * j - 1, 1.0), (4 * j, 2.0), (4 * j + 1, 2.0), (4 * j + 2, 2.0), (4 * j + 3, 1.0)):
            if 0 <= n < ncp:
                m[0, n, j] = w
                if packed:
                    m[1, n, nsp // 2 + j] = w
    return jnp.asarray(m)


MASK_BIG = 2.0 ** 100


def expand_matrices():
    e = np.zeros((WIN_PER_LANE, LANE, KC), np.float32)
    for w in range(WIN_PER_LANE):
        for key in range(KC):
            e[w, BLK_PER_KC * w + key // SLC_BLOCK, key] = MASK_BIG
    return jnp.asarray(e, BF16)


def _slc_body(q_ref, sel0_ref, sel1_ref, e_ref, kv_ref, o_ref, m_sc, l_sc, acc_sc, *, qb, pos0):
    i = pl.program_id(1)
    c = pl.program_id(2)

    @pl.when(c == 0)
    def _():
        m_sc[...] = jnp.full(m_sc.shape, NEG_INF, F32)
        l_sc[...] = jnp.zeros(l_sc.shape, F32)
        acc_sc[...] = jnp.zeros(acc_sc.shape, F32)

    def step(diagonal):
        ew = e_ref[c % WIN_PER_LANE]
        for g, sel_ref in ((0, sel0_ref), (1, sel1_ref)):
            bias = _dot(sel_ref[...], ew)
            if diagonal:
                t1 = pos0 + i * qb + _iota((qb, 1), 0)
                bias = bias + jnp.where(c * KC + _iota((1, KC), 1) <= t1, 0.0, NEG_INF)
            s = _dot_nt(_stack_heads(q_ref, g), kv_ref[:, g * DH:(g + 1) * DH])
            s = s + jnp.concatenate([bias] * NSA_GROUP, axis=0)
            m_old = m_sc[g]
            m_new = jnp.maximum(m_old, jnp.max(s, axis=-1, keepdims=True))
            alpha = jnp.exp(m_old - m_new)
            p = jnp.exp(s - m_new)
            l_sc[g] = alpha * l_sc[g] + jnp.sum(p, axis=-1, keepdims=True)
            acc_sc[g] = alpha * acc_sc[g] + _dot(p.astype(BF16), kv_ref[:, (2 + g) * DH:(3 + g) * DH])
            m_sc[g] = m_new

    first_partial = (pos0 + i * qb) // KC
    last_needed = (pos0 + i * qb + qb - 1) // KC
    pl.when(c < first_partial)(functools.partial(step, False))
    pl.when((c >= first_partial) & (c <= last_needed))(functools.partial(step, True))

    @pl.when(c == pl.num_programs(2) - 1)
    def _():
        for g in range(NSA_G):
            _unstack_heads(o_ref, g, acc_sc[g] / l_sc[g], qb)


def slc_attention(q, sel, emat, kvs, nb_batch, nqp, qb, pos0, nsp):
    nblk = nqp // qb
    lanes_g = nsp // LANE
    chunks_per_seq = kvs.shape[0] // (nb_batch * KC)
    nchunk = (pos0 + nqp - 1) // KC + 1
    last = lambda i: (pos0 + i * qb + qb - 1) // KC
    q_map = lambda b, i, c: (b * nblk + i, 0)
    sel_specs = [pl.BlockSpec((qb, LANE), functools.partial(
        lambda b, i, c, g: (b * nblk + i, g * lanes_g + jnp.minimum(c, last(i)) // WIN_PER_LANE), g=g))
        for g in range(NSA_G)]
    return pl.pallas_call(
        functools.partial(_slc_body, qb=qb, pos0=pos0), grid=(nb_batch, nblk, nchunk),
        in_specs=[pl.BlockSpec((qb, 1024), q_map)] + sel_specs
                 + [pl.BlockSpec((WIN_PER_LANE, LANE, KC), lambda b, i, c: (0, 0, 0)),
                    pl.BlockSpec((KC, 512), lambda b, i, c: (b * chunks_per_seq + jnp.minimum(c, last(i)), 0))],
        out_specs=pl.BlockSpec((qb, 1024), q_map),
        out_shape=jax.ShapeDtypeStruct((nb_batch * nqp, 1024), F32),
        scratch_shapes=[pltpu.VMEM((NSA_G, NSA_GROUP * qb, 1), F32), pltpu.VMEM((NSA_G, NSA_GROUP * qb, 1), F32),
                        pltpu.VMEM((NSA_G, NSA_GROUP * qb, DH), F32)],
        compiler_params=_cp(("parallel", "parallel", "arbitrary")), name="slc_attention")(q, sel, sel, emat, kvs)


def _win_body(q_ref, w_ref, o_ref, *, qb, pos0, span):
    i = pl.program_id(1)
    kv = w_ref[pl.ds(pl.multiple_of(i * qb, qb), span), :].astype(BF16)
    qpos = pos0 + i * qb + _iota((qb, 1), 0)
    kpos = pos0 - WINDOW + i * qb + _iota((1, span), 1)
    rel = qpos - kpos
    mask = (rel >= 0) & (rel <= WINDOW) & (kpos >= 0)
    bias = jnp.concatenate([jnp.where(mask, 0.0, NEG_INF)] * NSA_GROUP, axis=0)
    for g in range(NSA_G):
        s = _dot_nt(_stack_heads(q_ref, g), kv[:, g * DH:(g + 1) * DH]) + bias
        e = jnp.exp(s - jnp.max(s, axis=-1, keepdims=True))
        p = e * (1.0 / jnp.sum(e, axis=-1, keepdims=True))
        _unstack_heads(o_ref, g, _dot(p.astype(BF16), kv[:, (2 + g) * DH:(3 + g) * DH]), qb)


def win_attention(q, win_all, nb_batch, nqp, qb, pos0):
    nblk = nqp // qb
    wtot = win_all.shape[1]
    body = functools.partial(_win_body, qb=qb, pos0=pos0, span=WINDOW + qb)
    return pl.pallas_call(
        body, grid=(nb_batch, nblk),
        in_specs=[pl.BlockSpec((qb, 1024), lambda b, i: (b * nblk + i, 0)),
                  pl.BlockSpec((None, wtot, 512), lambda b, i: (b, 0, 0))],
        out_specs=pl.BlockSpec((qb, 1024), lambda b, i: (b * nblk + i, 0)),
        out_shape=jax.ShapeDtypeStruct((nb_batch * nqp, 1024), F32),
        compiler_params=_cp(("parallel", "parallel")), name="win_attention")(q, win_all)


def _combine_body(oc_ref, os_ref, ow_ref, zs_ref, o_ref):
    gate = jax.nn.sigmoid(zs_ref[...])
    for h in range(NSA_HEADS):
        sl = slice(h * DH, (h + 1) * DH)
        c0 = SM_NG + 3 * h
        o = (gate[:, c0:c0 + 1] * oc_ref[:, sl] + gate[:, c0 + 1:c0 + 2] * os_ref[:, sl]
             + gate[:, c0 + 2:c0 + 3] * ow_ref[:, sl])
        o_ref[:, sl] = o.astype(o_ref.dtype)


def nsa_combine(oc, osl, ow, z, tb):
    t = oc.shape[0]
    spec = pl.BlockSpec((tb, 1024), lambda i: (i, 0))
    return pl.pallas_call(
        _combine_body, grid=(t // tb,),
        in_specs=[spec, spec, spec, pl.BlockSpec((tb, LANE), lambda i: (i, Z_SM // LANE))],
        out_specs=spec, out_shape=jax.ShapeDtypeStruct((t, 1024), BF16),
        compiler_params=_cp(("parallel",)), name="nsa_combine")(oc, osl, ow, z)


def _pad_rows(x, n):
    if x.shape[0] == n:
        return x
    return jnp.concatenate([x, jnp.zeros((n - x.shape[0], x.shape[1]), x.dtype)], axis=0)


def _cumsum_rows(x):
    n = x.shape[0]
    tri = jnp.where(_iota((n, n), 0) >= _iota((n, n), 1), 1.0, 0.0).astype(BF16)
    hi, lo = _split(x)
    lo2 = (x - hi.astype(F32) - lo.astype(F32)).astype(BF16)
    return _dot(tri, hi) + (_dot(tri, lo) + _dot(tri, lo2))


def _log_sigmoid(x):
    return jnp.minimum(x, 0.0) - jnp.log1p(jnp.exp(-jnp.abs(x)))


def _softplus(x):
    return jnp.maximum(x, 0.0) + jnp.log1p(jnp.exp(-jnp.abs(x)))


def _mlstm_body(q_ref, k_ref, v_ref, og_ref, zs_ref, b_ref, ng_ref, c0_ref, n0_ref, m0_ref,
                o_ref, c_out, n_out, m_out, c_sc, n_sc, m_sc, *, nvalid):
    ck = pl.program_id(1)
    n = CHUNK
    lin = q_ref.shape[0]

    @pl.when(ck == 0)
    def _():
        c_sc[...] = c0_ref[...]
        n_sc[...] = n0_ref[...]
        m_sc[...] = m0_ref[...]

    gates = _pad_rows(zs_ref[...], n) + b_ref[...]
    lane = _iota((1, LANE), 1)
    is_f = (lane >= SM_MF) & (lane < SM_MF + ML_H)
    lf_all = jnp.where(is_f, _log_sigmoid(gates), 0.0)
    ig_all = gates
    if nvalid < n:
        live = _iota((n, 1), 0) < nvalid
        lf_all = jnp.where(live, lf_all, 0.0)
        ig_all = jnp.where(live, gates, NEG_INF)
    comb = jnp.where(is_f, _cumsum_rows(lf_all), ig_all)
    comb_t = comb.T
    incl = _iota((n, n), 0) >= _iota((n, n), 1)
    hs = range(ML_H)
    sk = [slice(h * ML_DK, (h + 1) * ML_DK) for h in hs]
    sv = [slice(h * ML_DV, (h + 1) * ML_DV) for h in hs]
    c0 = [c_sc[h] for h in hs]
    n0 = [n_sc[h] for h in hs]
    m0 = [m_sc[h] for h in hs]
    q = [_pad_rows(q_ref[:, sk[h]], n) for h in hs]
    k = [_pad_rows(k_ref[:, sk[h]], n) * (ML_DK ** -0.5) for h in hs]
    q16 = [q[h].astype(BF16) for h in hs]
    v16 = [_pad_rows(v_ref[:, sv[h]], n).astype(BF16) for h in hs]
    ig = [comb[:, SM_MI + h:SM_MI + h + 1] for h in hs]
    bc = [comb[:, SM_MF + h:SM_MF + h + 1] for h in hs]
    dlog = [jnp.where(incl, bc[h] + (comb_t[SM_MI + h:SM_MI + h + 1, :] - comb_t[SM_MF + h:SM_MF + h + 1, :]),
                      -jnp.inf) for h in hs]
    qk = [_dot_nt(q16[h], k[h].astype(BF16)) for h in hs]
    qc = [_dot(q16[h], c0[h].astype(BF16)) for h in hs]
    inter = [bc[h] + m0[h] for h in hs]
    m_t = [jnp.maximum(inter[h], jnp.max(dlog[h], axis=1, keepdims=True)) for h in hs]
    a = [qk[h] * jnp.exp(dlog[h] - m_t[h]) for h in hs]
    w_inter = [jnp.exp(inter[h] - m_t[h]) for h in hs]
    num = [_dot(a[h].astype(BF16), v16[h]) + w_inter[h] * qc[h] for h in hs]
    den = [jnp.sum(a[h], axis=1, keepdims=True) + w_inter[h] * jnp.sum(q[h] * n0[h], axis=1, keepdims=True)
           for h in hs]
    hh = [num[h] / jnp.maximum(jnp.abs(den[h]), jnp.exp(-m_t[h])) for h in hs]
    m_new = [m_t[h][n - 1:n] for h in hs]
    kw = [k[h] * jnp.exp(bc[h][n - 1:n] - bc[h] + ig[h] - m_new[h]) for h in hs]
    decay = [jnp.exp(bc[h][n - 1:n] + m0[h] - m_new[h]) for h in hs]
    c_new = [decay[h] * c0[h] + _dot(kw[h].T.astype(BF16), v16[h]) for h in hs]
    n_new = [decay[h] * n0[h] + jnp.sum(kw[h], axis=0, keepdims=True) for h in hs]
    for h in hs:
        c_sc[h] = c_new[h]
        n_sc[h] = n_new[h]
        m_sc[h] = m_new[h]
        c_out[h] = c_new[h]
        n_out[h] = n_new[h]
        m_out[h] = m_new[h]
        hn = hh[h] * lax.rsqrt(jnp.mean(hh[h] * hh[h], axis=-1, keepdims=True) + EPS) * ng_ref[:, sv[h]]
        o = jax.nn.sigmoid(_pad_rows(og_ref[:, sv[h]], n)) * hn
        o_ref[:, sv[h]] = o[:lin].astype(o_ref.dtype)


def mlstm_scan(z, bias_row, norm_g, c0, n0, m0, nb_batch, nqp, nvalid):
    lin = min(CHUNK, nqp)
    nc = nqp // lin
    tok = lambda w, off: pl.BlockSpec((lin, w), lambda b, c: (b * nc + c, off // w))
    body = functools.partial(_mlstm_body, nvalid=min(nvalid, CHUNK))
    st = lambda shape: pl.BlockSpec((None, ML_H) + shape, lambda b, c: (b, 0, 0, 0))
    wq, wv = ML_H * ML_DK, ML_H * ML_DV
    return pl.pallas_call(
        body, grid=(nb_batch, nc),
        in_specs=[tok(wq, Z_MQ), tok(wq, Z_MK), tok(wv, Z_MV), tok(wv, Z_MO), tok(LANE, Z_SM),
                  pl.BlockSpec((1, LANE), lambda b, c: (0, 0)), pl.BlockSpec((1, wv), lambda b, c: (0, 0)),
                  st((ML_DK, ML_DV)), st((1, ML_DK)), st((1, 1))],
        out_specs=[pl.BlockSpec((lin, wv), lambda b, c: (b * nc + c, 0)),
                   st((ML_DK, ML_DV)), st((1, ML_DK)), st((1, 1))],
        out_shape=[jax.ShapeDtypeStruct((nb_batch * nqp, wv), BF16),
                   jax.ShapeDtypeStruct((nb_batch, ML_H, ML_DK, ML_DV), F32),
                   jax.ShapeDtypeStruct((nb_batch, ML_H, 1, ML_DK), F32),
                   jax.ShapeDtypeStruct((nb_batch, ML_H, 1, 1), F32)],
        scratch_shapes=[pltpu.VMEM((ML_H, ML_DK, ML_DV), F32), pltpu.VMEM((ML_H, 1, ML_DK), F32),
                        pltpu.VMEM((ML_H, 1, 1), F32)],
        compiler_params=_cp(("parallel", "arbitrary")),
        name="mlstm_scan")(z, z, z, z, z, bias_row, norm_g.reshape(1, wv), c0, n0, m0)


def _gdn_body(q_ref, k_ref, v_ref, gz_ref, zs_ref, b_ref, a_ref, ng_ref, s0_ref,
              o_ref, s_out, s_sc, *, nvalid):
    ck = pl.program_id(1)
    n = CHUNK
    lin = q_ref.shape[0]

    @pl.when(ck == 0)
    def _():
        s_sc[...] = s0_ref[...]

    zs = _pad_rows(zs_ref[...], n)
    lane = _iota((1, LANE), 1)
    is_a = (lane >= SM_GA) & (lane < SM_GA + GD_H)
    g_all = jnp.where(is_a, a_ref[...] * _softplus(zs + b_ref[...]), 0.0)
    beta_all = jax.nn.sigmoid(zs)
    if nvalid < n:
        live = _iota((n, 1), 0) < nvalid
        g_all = jnp.where(live, g_all, 0.0)
        beta_all = jnp.where(live, beta_all, 0.0)
    gc_all = _cumsum_rows(g_all)
    gc_t = gc_all.T
    incl = _iota((n, n), 0) >= _iota((n, n), 1)
    strict = _iota((n, n), 0) > _iota((n, n), 1)
    eye = jnp.where(_iota((n, n), 0) == _iota((n, n), 1), 1.0, 0.0)
    hs = range(GD_H)
    sl = [slice(h * GD_D, (h + 1) * GD_D) for h in hs]
    s0 = [s_sc[h] for h in hs]
    q16 = [_pad_rows(q_ref[:, sl[h]], n).astype(BF16) for h in hs]
    k = [_pad_rows(k_ref[:, sl[h]], n) for h in hs]
    k16 = [k[h].astype(BF16) for h in hs]
    gc = [gc_all[:, SM_GA + h:SM_GA + h + 1] for h in hs]
    beta = [beta_all[:, SM_GB + h:SM_GB + h + 1] for h in hs]
    dec = [jnp.where(incl, jnp.exp(jnp.where(incl, gc[h] - gc_t[SM_GA + h:SM_GA + h + 1, :], 0.0)), 0.0) for h in hs]
    egc = [jnp.exp(gc[h]) for h in hs]
    kb = [k[h] * beta[h] for h in hs]
    xp = [-jnp.where(strict, _dot_nt(kb[h].astype(BF16), k16[h]) * dec[h], 0.0) for h in hs]
    tinv = [eye + xp[h] for h in hs]
    for _ in range(int(math.log2(n)) - 1):
        xp = [_dot3(xp[h], xp[h]) for h in hs]
        tinv = [tinv[h] + _dot3(tinv[h], xp[h]) for h in hs]
    rhs = [jnp.concatenate([_pad_rows(v_ref[:, sl[h]], n) * beta[h], kb[h] * egc[h]], axis=1) for h in hs]
    sol = [_dot3(tinv[h], rhs[h]) for h in hs]
    s016 = [s0[h].astype(BF16) for h in hs]
    u16 = [(sol[h][:, :GD_D] - _dot(sol[h][:, GD_D:].astype(BF16), s016[h])).astype(BF16) for h in hs]
    qk = [(_dot_nt(q16[h], k16[h]) * dec[h]).astype(BF16) for h in hs]
    o = [_dot(qk[h], u16[h]) + egc[h] * _dot(q16[h], s016[h]) for h in hs]
    kw = [(k[h] * jnp.exp(gc[h][n - 1:n] - gc[h])).T.astype(BF16) for h in hs]
    s_new = [jnp.exp(gc[h][n - 1:n]) * s0[h] + _dot(kw[h], u16[h]) for h in hs]
    for h in hs:
        s_sc[h] = s_new[h]
        s_out[h] = s_new[h]
        on = o[h] * lax.rsqrt(jnp.mean(o[h] * o[h], axis=-1, keepdims=True) + EPS) * ng_ref[...]
        o_ref[:, sl[h]] = (on * _silu(_pad_rows(gz_ref[:, sl[h]], n)))[:lin].astype(o_ref.dtype)


def gdn_scan(qkv, z, dt_row, nega_row, norm_g, s0, nb_batch, nqp, nvalid):
    lin = min(CHUNK, nqp)
    nc = nqp // lin
    w = GD_H * GD_D
    body = functools.partial(_gdn_body, nvalid=min(nvalid, CHUNK))
    tokq = lambda part: pl.BlockSpec((lin, w), lambda b, c: (b * nc + c, part))
    st = pl.BlockSpec((None, GD_H, GD_D, GD_D), lambda b, c: (b, 0, 0, 0))
    row = pl.BlockSpec((1, LANE), lambda b, c: (0, 0))
    return pl.pallas_call(
        body, grid=(nb_batch, nc),
        in_specs=[tokq(0), tokq(1), tokq(2),
                  pl.BlockSpec((lin, w), lambda b, c: (b * nc + c, Z_GZ // w)),
                  pl.BlockSpec((lin, LANE), lambda b, c: (b * nc + c, Z_SM // LANE)),
                  row, row, row, st],
        out_specs=[pl.BlockSpec((lin, w), lambda b, c: (b * nc + c, 0)), st],
        out_shape=[jax.ShapeDtypeStruct((nb_batch * nqp, w), BF16),
                   jax.ShapeDtypeStruct((nb_batch, GD_H, GD_D, GD_D), F32)],
        scratch_shapes=[pltpu.VMEM((GD_H, GD_D, GD_D), F32)],
        compiler_params=_cp(("parallel", "arbitrary")),
        name="gdn_scan")(qkv, qkv, qkv, z, z, dt_row, nega_row, norm_g.reshape(1, GD_D), s0)


HALO = 8


def _conv_apply(buf, x, w_ref, width):
    tb = x.shape[0]
    buf[HALO:HALO + tb, :] = x
    y = w_ref[width - 1:width, :] * x
    for kk in range(1, width):
        y = y + w_ref[width - 1 - kk:width - kk, :] * buf[HALO - kk:HALO - kk + tb, :]
    return y


def _conv_rows(buf, x, hist_ref, w_ref, width, first):
    @pl.when(first)
    def _():
        buf[0:HALO, :] = hist_ref[...]

    return _conv_apply(buf, x, w_ref, width)


def _gdn_prep_body(z_ref, hist_ref, w_ref, o_ref, buf):
    ct = pl.program_id(1)
    tb = z_ref.shape[0]
    x = z_ref[...]
    y = _silu(_conv_rows(buf, x, hist_ref, w_ref, GDN_CONV, pl.program_id(2) == 0))
    buf[0:HALO, :] = x[tb - HALO:tb, :]
    part = ct // (GD_H * GD_D // y.shape[1])
    qscale = jnp.where(part == 0, GD_D ** -0.5, 1.0)
    for hh in range(y.shape[1] // GD_D):
        sl = slice(hh * GD_D, (hh + 1) * GD_D)
        yh = y[:, sl]
        nrm = yh * lax.rsqrt(jnp.sum(yh * yh, axis=-1, keepdims=True) + EPS) * qscale
        o_ref[:, sl] = jnp.where(part == 2, yh, nrm)


def gdn_prep(z, hist, w, nb_batch, nqp, tb, tc=512):
    nblk = nqp // tb
    ncol = 3 * GD_H * GD_D
    return pl.pallas_call(
        _gdn_prep_body, grid=(nb_batch, ncol // tc, nblk),
        in_specs=[pl.BlockSpec((tb, tc), lambda b, c, i: (b * nblk + i, Z_GQKV // tc + c)),
                  pl.BlockSpec((None, HALO, tc), lambda b, c, i: (b, 0, c)),
                  pl.BlockSpec((GDN_CONV, tc), lambda b, c, i: (0, c))],
        out_specs=pl.BlockSpec((tb, tc), lambda b, c, i: (b * nblk + i, c)),
        out_shape=jax.ShapeDtypeStruct((nb_batch * nqp, ncol), F32),
        scratch_shapes=[pltpu.VMEM((HALO + tb, tc), F32)],
        compiler_params=_cp(("parallel", "parallel", "arbitrary")), name="gdn_prep")(z, hist, w)


def _ffn_up_body(h_ref, wa_ref, wb_ref, ca_ref, cb_ref, ha_ref, hb_ref, act_ref, ta_ref, tb_ref, bufa, bufb,
                 *, bps, nseq, tail0):
    i = pl.program_id(1)
    tm = h_ref.shape[0]
    rows = tm // nseq
    h = h_ref[...]
    ys = []
    for w_ref, c_ref, hist_ref, t_ref, buf in ((wa_ref, ca_ref, ha_ref, ta_ref, bufa),
                                               (wb_ref, cb_ref, hb_ref, tb_ref, bufb)):
        u = _dot(h, w_ref[...])
        if nseq == 1:
            ys.append(_conv_rows(buf, u, hist_ref.at[0], c_ref, FFN_CONV, (i % bps) == 0))
            buf[0:HALO, :] = u[tm - HALO:tm, :]
        else:
            parts = []
            for s in range(nseq):
                buf[0:HALO, :] = hist_ref[s]
                parts.append(_conv_apply(buf, u[s * rows:(s + 1) * rows], c_ref, FFN_CONV))
            ys.append(jnp.concatenate(parts, axis=0))
        for s in range(nseq):
            t_ref[s] = u[s * rows + tail0:s * rows + tail0 + HALO, :]
    act_ref[...] = (_silu(ys[0]) * ys[1]).astype(act_ref.dtype)


def ffn_up_act(h, w_up, conv_w, hist, nb_batch, nq, nqp, tm, tn):
    t, d = h.shape
    nj = FFN_PAD // tn
    bps = max(1, nqp // tm)
    nseq = max(1, tm // nqp)
    rows = tm // nseq
    assert (nq - 1) // rows == bps - 1 and (nq - 1) % HALO >= FFN_CONV - 2
    tail0 = ((nq - 1) % rows) // HALO * HALO
    body = functools.partial(_ffn_up_body, bps=bps, nseq=nseq, tail0=tail0)
    col = lambda half: (lambda j, i: (0, half * nj + j))
    seq = lambda half: (lambda j, i: (i // bps, 0, half * nj + j))
    tail = jax.ShapeDtypeStruct((nb_batch, HALO, FFN_PAD), F32)
    return pl.pallas_call(
        body, grid=(nj, t // tm),
        in_specs=[pl.BlockSpec((tm, d), lambda j, i: (i, 0)),
                  pl.BlockSpec((d, tn), col(0)), pl.BlockSpec((d, tn), col(1)),
                  pl.BlockSpec((FFN_CONV, tn), col(0)), pl.BlockSpec((FFN_CONV, tn), col(1)),
                  pl.BlockSpec((nseq, HALO, tn), seq(0)), pl.BlockSpec((nseq, HALO, tn), seq(1))],
        out_specs=[pl.BlockSpec((tm, tn), lambda j, i: (i, j)),
                   pl.BlockSpec((nseq, HALO, tn), lambda j, i: (i // bps, 0, j)),
                   pl.BlockSpec((nseq, HALO, tn), lambda j, i: (i // bps, 0, j))],
        out_shape=[jax.ShapeDtypeStruct((t, FFN_PAD), BF16), tail, tail],
        scratch_shapes=[pltpu.VMEM((HALO + tm, tn), F32), pltpu.VMEM((HALO + tm, tn), F32)],
        compiler_params=_cp(("parallel", "arbitrary")),
        name="ffn_up_act")(h, w_up, w_up, conv_w, conv_w, hist, hist)


def _merge_body(oa_ref, ob_ref, oc_ref, w_ref, g0_ref, g1_ref, g2_ref, o_ref):
    acc = jax.nn.sigmoid(g0_ref[...]) * _dot(oa_ref[...], w_ref[0])
    acc = acc + jax.nn.sigmoid(g1_ref[...]) * _dot(ob_ref[...], w_ref[1])
    acc = acc + jax.nn.sigmoid(g2_ref[...]) * _dot(oc_ref[...], w_ref[2])
    o_ref[...] = acc.astype(o_ref.dtype)


def merge_branches(oa, ob, oc, wb, z, tm, tn):
    t = oa.shape[0]
    d = wb.shape[2]
    tok = pl.BlockSpec((tm, 1024), lambda i, j: (i, 0))
    gate = lambda nbr: pl.BlockSpec((tm, tn), lambda i, j: (i, (Z_BR + nbr * d) // tn + j))
    return pl.pallas_call(
        _merge_body, grid=(t // tm, d // tn),
        in_specs=[tok, tok, tok, pl.BlockSpec((3, 1024, tn), lambda i, j: (0, 0, j)), gate(0), gate(1), gate(2)],
        out_specs=pl.BlockSpec((tm, tn), lambda i, j: (i, j)),
        out_shape=jax.ShapeDtypeStruct((t, d), BF16),
        compiler_params=_cp(("parallel", "parallel")), name="merge_branches")(oa, ob, oc, wb, z, z, z)


def _layer_weights(p):
    w_in = p['w_in']
    offs = np.cumsum([0, 1024, 1536, 24, 512, 512, 1024, 4, 4, 1024, 3072, 8, 8, 1024, 6144])
    seg = lambda i: w_in[:, offs[i]:offs[i + 1]]
    kv = seg(1).reshape(-1, 3, 2, NSA_G * DH)
    small = jnp.concatenate([seg(2), seg(6), seg(7), seg(10), seg(11)], axis=1)
    small = jnp.pad(small, ((0, 0), (0, Z_COLS - Z_SM - small.shape[1])))
    gap = jnp.zeros((w_in.shape[0], Z_KV - Z_MQ - ML_H * ML_DK), w_in.dtype)
    w_all = jnp.concatenate([seg(0), seg(5), seg(8), seg(12), seg(9), seg(13), seg(3), gap,
                             kv[:, :, 0].reshape(-1, KV_W // 2), kv[:, :, 1].reshape(-1, KV_W // 2),
                             seg(4), small], axis=1)
    assert w_all.shape[1] == Z_COLS
    w1 = p['w_cmp1']
    wcat = jnp.concatenate([w1[:, :CMP_STRIDE].reshape(2, CMP_STRIDE * DH, DH),
                            w1[:, CMP_STRIDE:].reshape(2, CMP_STRIDE * DH, DH)], axis=2).astype(BF16)

    def lane_row(pairs):
        row = jnp.zeros((LANE,), F32)
        for off, v in pairs:
            row = row.at[off:off + v.shape[0]].set(v.astype(F32))
        return row.reshape(1, LANE)

    f = p['w_down'].shape[0]
    fpad = FFN_PAD - f
    pad_halves = lambda a: jnp.concatenate(
        [jnp.pad(a[..., :f], ((0, 0),) * (a.ndim - 1) + ((0, fpad),)),
         jnp.pad(a[..., f:], ((0, 0),) * (a.ndim - 1) + ((0, fpad),))], axis=-1)
    return dict(
        w_up=pad_halves(p['w_up']).astype(BF16), ffn_conv_w=pad_halves(p['ffn_conv_w']),
        w_down=jnp.pad(p['w_down'], ((0, fpad), (0, 0))).astype(BF16), pad_halves=pad_halves,
        w_all=w_all.astype(BF16), wcat=wcat,
        pe8=jnp.pad(p['cmp_pe'].reshape(2, 1, CMP_BLOCK * DH), ((0, 0), (0, 7), (0, 0))).astype(BF16),
        w1flat=w1.reshape(2, CMP_BLOCK * DH, DH).astype(BF16), w2=p['w_cmp2'].astype(BF16),
        ml_bias=lane_row([(SM_MI, p['mlstm_b_i']), (SM_MF, p['mlstm_b_f'])]),
        gd_dt=lane_row([(SM_GA, p['gdn_dt_bias'])]),
        gd_nega=lane_row([(SM_GA, -jnp.exp(p['gdn_a_log'].astype(F32)))]),
        w_branch=p['w_branch'].astype(BF16), w_out=p['w_out'].astype(BF16))


def _hist_rows(state):
    return jnp.pad(state, ((0, 0), (HALO - state.shape[1], 0), (0, 0)))


def _group_layer(x, grp, st, p, w, layer):
    nb, nq, nqp, pos0 = grp['nb'], grp['nq'], grp['nqp'], grp['pos0']
    t = nb * nqp
    big = nqp >= 512
    tm = 1024 if big else t
    h = rmsnorm_bf16(x, p['norm1_g'], min(tm, 256))
    z = matmul(h, w['w_all'], tm, 1024 if big else 2048, name="in_proj")

    tb = 256 if big else nqp
    q_rot, rows, kvs, win_new = nsa_prep(z, grp['rope'], p['q_norm_g'], p['k_norm_g'], nqp, tb)
    pps = KC // PAGE
    if pos0 == 0:
        nblk_c = nqp // CMP_STRIDE
        ncp = nblk_c
        fs = compress_pages(rows.reshape(nb * nqp // PAGE, PAGE_SUB, LANE), w['wcat'], nb, nqp // KC, pps)
    else:
        n_pages = st['table'].shape[1]
        nblk_c = (n_pages + 1) * (PAGE // CMP_STRIDE)
        ncp = -(-nblk_c // LANE) * LANE
        live = (jnp.arange(nqp * ROW_SUB) < nq * ROW_SUB)[None, :, None]
        newpage = jnp.pad(jnp.where(live, rows.reshape(nb, nqp * ROW_SUB, LANE), 0.0),
                          ((0, 0), (0, PAGE_SUB - nqp * ROW_SUB), (0, 0)))
        fs, kvs = compress_pages(st['cache'], w['wcat'], nb, ncp * CMP_STRIDE // KC, pps, table=st['table'],
                                 layer=layer, newpage=newpage)
        kvs = kvs.reshape(-1, 512)
    kvc = compress_finish(fs, w['pe8'], w['w1flat'], w['w2'])
    ns = nblk_c * CMP_STRIDE // SLC_BLOCK
    nsp = -(-ns // LANE) * LANE
    t_all = pos0 + nq
    nsel = min(N_SELECT, -(-t_all // SLC_BLOCK))
    qb = 128 if big else nqp
    o_cmp, sel = cmp_select(q_rot, kvc, nb, nqp, qb, pos0, ns, nsp, nsel)
    o_slc = slc_attention(q_rot, sel, expand_matrices(), kvs, nb, nqp, qb, pos0, nsp)
    win3 = win_new.reshape(nb, nqp, 512)
    win_all = jnp.concatenate([st['win'], win3], axis=1)
    o_win = win_attention(q_rot, win_all, nb, nqp, qb, pos0)
    o_nsa = nsa_combine(o_cmp, o_slc, o_win, z, tb)
    win_state = jnp.concatenate([st['win'], win3[:, :nq]], axis=1)[:, -WINDOW:]

    o_ml, m_c, m_n, m_m = mlstm_scan(z, w['ml_bias'], p['mlstm_norm_g'], st['m_c'], st['m_n'], st['m_m'],
                                     nb, nqp, nq)

    tbc = 512 if big else nqp
    qkv = gdn_prep(z, _hist_rows(st['g_conv']), p['gdn_conv_w'], nb, nqp, tbc)
    o_gd, g_s = gdn_scan(qkv, z, w['gd_dt'], w['gd_nega'], p['gdn_norm_g'], st['g_s'], nb, nqp, nq)
    z3 = z.reshape(nb, nqp, Z_COLS)
    g_conv_new = z3[:, nq - (GDN_CONV - 1):nq, Z_GQKV:Z_GQKV + 3 * GD_H * GD_D]

    mrg = merge_branches(o_nsa, o_ml, o_gd, w['w_branch'], z, tm, 512)
    x1 = matmul(mrg, w['w_out'], tm, 1024 if big else 2048, res=x, name="out_proj")

    h2 = rmsnorm_bf16(x1, p['norm2_g'], min(tm, 256))
    act, tail_a, tail_b = ffn_up_act(h2, w['w_up'], w['ffn_conv_w'], _hist_rows(w['pad_halves'](st['f_conv'])),
                                     nb, nq, nqp, tm, 512)
    x2 = matmul(act, w['w_down'], tm, 512, res=x1, name="down_proj")
    f = p['w_down'].shape[0]
    rows_blk = min(tm, nqp)
    r0 = (nq - (FFN_CONV - 1)) % rows_blk - ((nq - 1) % rows_blk) // HALO * HALO
    f_conv_new = jnp.concatenate([tail_a[:, r0:r0 + FFN_CONV - 1, :f], tail_b[:, r0:r0 + FFN_CONV - 1, :f]], axis=-1)

    new_rows = rows.reshape(nb, nqp, 4, NSA_G, DH)[:, :nq]
    states = (new_rows, win_state.reshape(nb, WINDOW, 2, NSA_G, DH), m_c, m_n.reshape(nb, ML_H, ML_DK),
              m_m.reshape(nb, ML_H), g_conv_new, g_s, f_conv_new)
    return x2, states


def kernel(x_prompt, x_sample, cache_nsa_kv, page_table, cache_nsa_win, state_mlstm_c, state_mlstm_n,
           state_mlstm_m, state_gdn_conv, state_gdn_s, state_ffn_conv, norm1_g, w_in, q_norm_g, k_norm_g,
           w_cmp1, cmp_pe, w_cmp2, mlstm_b_i, mlstm_b_f, mlstm_norm_g, gdn_conv_w, gdn_a_log, gdn_dt_bias,
           gdn_norm_g, w_branch, w_out, norm2_g, w_up, ffn_conv_w, w_down):
    depth = w_in.shape[0]
    bp, sp, d = x_prompt.shape
    bs, ss, _ = x_sample.shape
    n_pages = page_table.shape[1]
    past = n_pages * PAGE
    ssp = -(-ss // 16) * 16
    assert sp % KC == 0 and ss >= GDN_CONV - 1 and cache_nsa_win.shape[2] == WINDOW and n_pages % (KC // PAGE) == 0
    grp_p = dict(nb=bp, nq=sp, nqp=sp, pos0=0, rope=rope_tables(0, sp))
    grp_s = dict(nb=bs, nq=ss, nqp=ssp, pos0=past, rope=rope_tables(past, ssp))
    cache = cache_nsa_kv.reshape(depth, cache_nsa_kv.shape[1], PAGE_SUB, LANE)
    xp = x_prompt.reshape(bp * sp, d)
    xs = jnp.pad(x_sample, ((0, 0), (0, ssp - ss), (0, 0))).reshape(bs * ssp, d)
    st_p, st_s = [], []
    for l in range(depth):
        p = dict(norm1_g=norm1_g[l], w_in=w_in[l], q_norm_g=q_norm_g[l], k_norm_g=k_norm_g[l],
                 w_cmp1=w_cmp1[l], cmp_pe=cmp_pe[l], w_cmp2=w_cmp2[l], mlstm_b_i=mlstm_b_i[l],
                 mlstm_b_f=mlstm_b_f[l], mlstm_norm_g=mlstm_norm_g[l], gdn_conv_w=gdn_conv_w[l],
                 gdn_a_log=gdn_a_log[l], gdn_dt_bias=gdn_dt_bias[l], gdn_norm_g=gdn_norm_g[l],
                 w_branch=w_branch[l], w_out=w_out[l], norm2_g=norm2_g[l], w_up=w_up[l],
                 ffn_conv_w=ffn_conv_w[l], w_down=w_down[l])
        w = _layer_weights(p)
        zero_p = dict(win=jnp.zeros((bp, WINDOW, 512), F32),
                      m_c=jnp.zeros((bp, ML_H, ML_DK, ML_DV), F32), m_n=jnp.zeros((bp, ML_H, 1, ML_DK), F32),
                      m_m=jnp.zeros((bp, ML_H, 1, 1), F32),
                      g_conv=jnp.zeros((bp, GDN_CONV - 1, 3 * GD_H * GD_D), F32),
                      g_s=jnp.zeros((bp, GD_H, GD_D, GD_D), F32),
                      f_conv=jnp.zeros((bp, FFN_CONV - 1, state_ffn_conv.shape[-1]), F32))
        xp, s_p = _group_layer(xp, grp_p, zero_p, p, w, l)
        st = dict(cache=cache, table=page_table, win=cache_nsa_win[l].reshape(bs, WINDOW, 512),
                  m_c=state_mlstm_c[l], m_n=state_mlstm_n[l].reshape(bs, ML_H, 1, ML_DK),
                  m_m=state_mlstm_m[l].reshape(bs, ML_H, 1, 1), g_conv=state_gdn_conv[l],
                  g_s=state_gdn_s[l], f_conv=state_ffn_conv[l])
        xs, s_s = _group_layer(xs, grp_s, st, p, w, l)
        st_p.append(s_p)
        st_s.append(s_s)
    outs = [xp.reshape(bp, sp, d), xs.reshape(bs, ssp, d)[:, :ss]]
    for i in range(8):
        outs.append(jnp.stack([s[i] for s in st_p], axis=0))
        outs.append(jnp.stack([s[i] for s in st_s], axis=0))
    return tuple(outs)
```

```python
import functools
import math

import numpy as np
import jax
import jax.numpy as jnp
from jax import lax
from jax.experimental import pallas as pl
from jax.experimental.pallas import tpu as pltpu

F32 = jnp.float32
BF16 = jnp.bfloat16
HI = lax.Precision.HIGHEST

EPS = 1e-6
NEG_INF = -1e30
LANE = 128
SUBLANE = 8
PAGE = 128
DH = 128
NSA_HEADS = 8
NSA_G = 2
NSA_GROUP = NSA_HEADS // NSA_G
ROPE_DIM = DH // 4
ROPE_THETA = 500000.0
CMP_BLOCK = 32
CMP_STRIDE = 16
SLC_BLOCK = 64
N_SELECT = 16
WINDOW = 512
ML_H, ML_DK, ML_DV = 4, 128, 256
GD_H, GD_D = 8, 128
GDN_CONV = 4
FFN_CONV = 3
CHUNK = 64
FFN_PAD = 5632
KC = 1024
BLK_PER_KC = KC // SLC_BLOCK
WIN_PER_LANE = LANE // BLK_PER_KC
VMEM_LIMIT = 56 * 1024 * 1024

Z_Q, Z_MV, Z_MO, Z_GZ = 0, 1024, 2048, 3072
Z_GQKV, Z_BR = 4096, 7168
Z_MQ, Z_KV, Z_MK, Z_SM = 13312, 13824, 15360, 15872
Z_COLS = 16384
KV_W = 1536
ROW_W = 4 * NSA_G * DH
ROW_SUB = ROW_W // LANE
PAGE_SUB = PAGE * ROW_SUB
SM_NG, SM_MI, SM_MF, SM_GA, SM_GB = 0, 24, 28, 32, 40


def _cp(sem):
    return pltpu.CompilerParams(dimension_semantics=sem, vmem_limit_bytes=VMEM_LIMIT)


def _iota(shape, dim):
    return lax.broadcasted_iota(jnp.int32, shape, dim)


def _dot(a, b, **kw):
    return jnp.dot(a, b, preferred_element_type=F32, **kw)


def _dot_nt(a, b, **kw):
    return lax.dot_general(a, b, (((1,), (1,)), ((), ())), preferred_element_type=F32, **kw)


def _silu(x):
    return x * jax.nn.sigmoid(x)


def _split(x):
    hi = x.astype(BF16)
    return hi, (x - hi.astype(F32)).astype(BF16)


def _dot3(a, b):
    ah, al = _split(a)
    bh, bl = _split(b)
    return _dot(ah, bh) + (_dot(ah, bl) + _dot(al, bh))


def _rmsnorm_body(x_ref, g_ref, o_ref):
    x = x_ref[...]
    y = x * lax.rsqrt(jnp.mean(x * x, axis=-1, keepdims=True) + EPS)
    o_ref[...] = (y * g_ref[...]).astype(o_ref.dtype)


def rmsnorm_bf16(x, g, tm):
    t, d = x.shape
    return pl.pallas_call(
        _rmsnorm_body, grid=(t // tm,),
        in_specs=[pl.BlockSpec((tm, d), lambda i: (i, 0)), pl.BlockSpec((1, d), lambda i: (0, 0))],
        out_specs=pl.BlockSpec((tm, d), lambda i: (i, 0)),
        out_shape=jax.ShapeDtypeStruct((t, d), BF16),
        compiler_params=_cp(("parallel",)), name="rmsnorm")(x, g.reshape(1, d))


def _mm_body(a_ref, w_ref, o_ref):
    o_ref[...] = _dot(a_ref[...], w_ref[...]).astype(o_ref.dtype)


def _mm_res_body(a_ref, w_ref, r_ref, o_ref):
    o_ref[...] = r_ref[...] + _dot(a_ref[...], w_ref[...])


def matmul(a, w, layer, tm, tn, res=None, name="matmul"):
    m, k = a.shape
    n = w.shape[2]
    in_specs = [pl.BlockSpec((tm, k), lambda i, j: (i, 0)), pl.BlockSpec((None, k, tn), lambda i, j: (layer, 0, j))]
    args = [a, w]
    body = _mm_body
    if res is not None:
        in_specs.append(pl.BlockSpec((tm, tn), lambda i, j: (i, j)))
        args.append(res)
        body = _mm_res_body
    return pl.pallas_call(
        body, grid=(m // tm, n // tn), in_specs=in_specs,
        out_specs=pl.BlockSpec((tm, tn), lambda i, j: (i, j)),
        out_shape=jax.ShapeDtypeStruct((m, n), F32),
        compiler_params=_cp(("parallel", "parallel")), name=name)(*args)


def _nsa_prep_body(zq_ref, zkv_ref, c_ref, s1_ref, s2_ref, qg_ref, kg_ref, q_out, rows_out, kvs_out, win_out):
    c = c_ref[...]
    s1 = s1_ref[...]
    s2 = s2_ref[...]
    tb = zq_ref.shape[0]

    def norm_rope(x, g):
        y = x * lax.rsqrt(jnp.mean(x * x, axis=-1, keepdims=True) + EPS) * g
        return y * c + pltpu.roll(y, LANE - ROPE_DIM // 2, 1) * s1 + pltpu.roll(y, ROPE_DIM // 2, 1) * s2

    for h in range(NSA_HEADS):
        sl = slice(h * DH, (h + 1) * DH)
        q_out[:, sl] = (norm_rope(zq_ref[:, sl], qg_ref[...]) * (DH ** -0.5)).astype(q_out.dtype)
    for br in range(3):
        for g in range(NSA_G):
            col = (br * NSA_G + g) * DH
            kk = norm_rope(zkv_ref[:, col:col + DH], kg_ref[br:br + 1, :])
            vv = zkv_ref[:, KV_W // 2 + col:KV_W // 2 + col + DH]
            if br < 2:
                rows_out[pl.ds(4 * br + g, tb, stride=ROW_SUB), :] = kk
                rows_out[pl.ds(4 * br + 2 + g, tb, stride=ROW_SUB), :] = vv
            else:
                win_out[:, g * DH:(g + 1) * DH] = kk
                win_out[:, 2 * DH + g * DH:2 * DH + (g + 1) * DH] = vv
            if br == 1:
                kvs_out[:, g * DH:(g + 1) * DH] = kk.astype(kvs_out.dtype)
                kvs_out[:, (2 + g) * DH:(3 + g) * DH] = vv.astype(kvs_out.dtype)


def nsa_prep(z, tabs, q_g, k_g, nqp, tb):
    t = z.shape[0]
    nblk = nqp // tb
    tab_spec = pl.BlockSpec((tb, LANE), lambda i: (i % nblk, 0))
    return pl.pallas_call(
        _nsa_prep_body, grid=(t // tb,),
        in_specs=[pl.BlockSpec((tb, 1024), lambda i: (i, Z_Q // 1024)),
                  pl.BlockSpec((tb, KV_W), lambda i: (i, Z_KV // KV_W)), tab_spec, tab_spec, tab_spec,
                  pl.BlockSpec((1, DH), lambda i: (0, 0)), pl.BlockSpec((3, DH), lambda i: (0, 0))],
        out_specs=[pl.BlockSpec((tb, 1024), lambda i: (i, 0)), pl.BlockSpec((tb * ROW_SUB, LANE), lambda i: (i, 0)),
                   pl.BlockSpec((tb, 512), lambda i: (i, 0)), pl.BlockSpec((tb, 512), lambda i: (i, 0))],
        out_shape=[jax.ShapeDtypeStruct((t, 1024), BF16), jax.ShapeDtypeStruct((t * ROW_SUB, LANE), F32),
                   jax.ShapeDtypeStruct((t, 512), BF16), jax.ShapeDtypeStruct((t, 512), F32)],
        compiler_params=_cp(("parallel",)), name="nsa_prep")(z, z, *tabs, q_g.reshape(1, DH), k_g)


def rope_tables(pos0, nqp):
    half = ROPE_DIM // 2
    pos = pos0 + jnp.arange(nqp, dtype=jnp.int32)
    inv = ROPE_THETA ** (-jnp.arange(half, dtype=F32) / half)
    ang = pos.astype(F32)[:, None] * inv[None, :]
    cos, sin = jnp.cos(ang), jnp.sin(ang)
    zeros = jnp.zeros((nqp, LANE - ROPE_DIM), F32)
    c = jnp.concatenate([cos, cos, jnp.ones((nqp, LANE - ROPE_DIM), F32)], axis=1)
    s1 = jnp.concatenate([-sin, jnp.zeros((nqp, half), F32), zeros], axis=1)
    s2 = jnp.concatenate([jnp.zeros((nqp, half), F32), sin, zeros], axis=1)
    return c, s1, s2


def _compress_body(*refs, pps, paged, new_step):
    if paged:
        refs = refs[1:]
    slots = refs[:pps]
    k = pps
    new_ref = refs[k] if paged else None
    k += 1 if paged else 0
    w_refs = refs[k:k + 2]
    o_ref = refs[k + 2]
    kvs_ref = refs[k + 3] if paged else None
    i = pl.program_id(1)
    nb = PAGE // CMP_STRIDE

    def rows(idx, start, size, stride):
        v = slots[idx][pl.ds(start, size, stride=stride), :]
        if paged and idx == 0:
            v = jnp.where(i == new_step, new_ref[pl.ds(start, size, stride=stride), :], v)
        return v

    for kind in range(2):
        xs = []
        for g in range(NSA_G):
            part = kind * 2 + g
            per_slot = [jnp.concatenate([rows(s, ROW_SUB * j + part, nb, CMP_STRIDE * ROW_SUB)
                                         for j in range(CMP_STRIDE)], axis=1) for s in range(pps)]
            xs.append(jnp.concatenate(per_slot, axis=0) if pps > 1 else per_slot[0])
        x = jnp.concatenate(xs, axis=0).astype(BF16)
        fs = _dot(x, w_refs[kind][...])
        for g in range(NSA_G):
            o_ref[:, (kind * 2 + g) * 2 * DH:(kind * 2 + g + 1) * 2 * DH] = fs[g * nb * pps:(g + 1) * nb * pps]
    if paged:
        for s in range(pps):
            for q in range(4):
                kvs_ref[s * PAGE:(s + 1) * PAGE, q * DH:(q + 1) * DH] = rows(s, 4 + q, PAGE, ROW_SUB).astype(kvs_ref.dtype)


def compress_pages(src, wcat, nb_batch, steps, pps, table=None, layer=None, newpage=None):
    nb = PAGE // CMP_STRIDE
    w_spec = pl.BlockSpec((CMP_STRIDE * DH, 2 * DH), lambda b, i, *_: (0, 0))
    out_spec = pl.BlockSpec((None, nb * pps, 1024), lambda b, i, *_: (b, i, 0))
    out_shape = jax.ShapeDtypeStruct((nb_batch, steps * pps * nb, 1024), F32)
    if table is None:
        body = functools.partial(_compress_body, pps=pps, paged=False, new_step=None)
        slot_specs = [pl.BlockSpec((None, PAGE_SUB, LANE), functools.partial(
            lambda b, i, s: ((b * steps + i) * pps + s, 0, 0), s=s)) for s in range(pps)]
        return pl.pallas_call(
            body, grid=(nb_batch, steps), in_specs=slot_specs + [w_spec, w_spec], out_specs=out_spec,
            out_shape=out_shape, compiler_params=_cp(("parallel", "parallel")),
            name="compress")(*([src] * pps), wcat[0], wcat[1])
    n_pages = table.shape[1]
    body = functools.partial(_compress_body, pps=pps, paged=True, new_step=n_pages // pps)
    slot_specs = [pl.BlockSpec((None, None, PAGE_SUB, LANE), functools.partial(
        lambda b, i, tbl, s: (layer, tbl[b, jnp.minimum(i * pps + s, n_pages - 1)], 0, 0), s=s)) for s in range(pps)]
    new_spec = pl.BlockSpec((None, PAGE_SUB, LANE), lambda b, i, tbl: (b, 0, 0))
    kvs_spec = pl.BlockSpec((None, pps * PAGE, 512), lambda b, i, tbl: (b, i, 0))
    gs = pltpu.PrefetchScalarGridSpec(num_scalar_prefetch=1, grid=(nb_batch, steps),
                                      in_specs=slot_specs + [new_spec, w_spec, w_spec],
                                      out_specs=[out_spec, kvs_spec])
    return pl.pallas_call(body, grid_spec=gs,
                          out_shape=[out_shape, jax.ShapeDtypeStruct((nb_batch, steps * pps * PAGE, 512), BF16)],
                          compiler_params=_cp(("parallel", "parallel")),
                          name="compress_paged")(table, *([src] * pps), newpage, wcat[0], wcat[1])


def _cfin_body(fs_ref, pe_ref, w1_ref, w2_ref, o_ref):
    ncp = fs_ref.shape[0]
    last = _iota((ncp, 1), 0) == ncp - 1
    for kind in range(2):
        bias = _dot(pe_ref[kind], w1_ref[kind])[0:1, :]
        for g in range(NSA_G):
            base = (kind * 2 + g) * 2 * DH
            second_next = jnp.where(last, 0.0, pltpu.roll(fs_ref[:, base + DH:base + 2 * DH], ncp - 1, 0))
            hid = fs_ref[:, base:base + DH] + second_next + bias
            o_ref[:, (kind * 2 + g) * DH:(kind * 2 + g + 1) * DH] = _dot(
                _silu(hid).astype(BF16), w2_ref[kind]).astype(o_ref.dtype)


def compress_finish(fs, pe8, w1flat, w2):
    b, ncp, _ = fs.shape
    return pl.pallas_call(
        _cfin_body, grid=(b,),
        in_specs=[pl.BlockSpec((None, ncp, 1024), lambda i: (i, 0, 0)), pl.BlockSpec(pe8.shape, lambda i: (0, 0, 0)),
                  pl.BlockSpec(w1flat.shape, lambda i: (0, 0, 0)), pl.BlockSpec(w2.shape, lambda i: (0, 0, 0))],
        out_specs=pl.BlockSpec((None, ncp, 512), lambda i: (i, 0, 0)),
        out_shape=jax.ShapeDtypeStruct((b, ncp, 512), BF16),
        compiler_params=_cp(("parallel",)), name="compress_finish")(fs, pe8, w1flat, w2)


def _stack_heads(q_ref, g):
    return jnp.concatenate([q_ref[:, (g * NSA_GROUP + h) * DH:(g * NSA_GROUP + h + 1) * DH]
                            for h in range(NSA_GROUP)], axis=0)


def _unstack_heads(o_ref, g, o, qb):
    for h in range(NSA_GROUP):
        o_ref[:, (g * NSA_GROUP + h) * DH:(g * NSA_GROUP + h + 1) * DH] = o[h * qb:(h + 1) * qb]


def _rank(score, column, blk, ns):
    rank = jnp.zeros(score.shape, F32)
    for j in range(ns):
        sj = column(j)
        rank = rank + jnp.where(sj > score, 1.0, 0.0) + jnp.where(sj == score, jnp.where(blk > j, 1.0, 0.0), 0.0)
    return rank


def _cmp_body(q_ref, kv_ref, m_ref, o_ref, sel_ref, *, qb, pos0, ncp, ns, nsp, nsel, packed):
    i = pl.program_id(1)
    t1 = pos0 + i * qb + _iota((qb, 1), 0)
    vis = (_iota((1, ncp), 1) * CMP_STRIDE + (CMP_BLOCK - 1)) <= t1
    bias = jnp.concatenate([jnp.where(vis, 0.0, NEG_INF)] * NSA_GROUP, axis=0)
    any_vis = jnp.concatenate([jnp.where(t1 >= CMP_BLOCK - 1, 1.0, 0.0)] * NSA_GROUP, axis=0)
    pcs = []
    for g in range(NSA_G):
        s = _dot_nt(_stack_heads(q_ref, g), kv_ref[:, g * DH:(g + 1) * DH]) + bias
        e = jnp.exp(s - jnp.max(s, axis=-1, keepdims=True))
        pc = e * (any_vis / jnp.sum(e, axis=-1, keepdims=True))
        _unstack_heads(o_ref, g, _dot(pc.astype(BF16), kv_ref[:, (2 + g) * DH:(3 + g) * DH]), qb)
        pcs.append((pc[0:qb] + pc[qb:2 * qb]) + (pc[2 * qb:3 * qb] + pc[3 * qb:4 * qb]))
    cur = t1 // SLC_BLOCK

    def scores(imp, blk):
        valid = (blk * SLC_BLOCK <= t1) & (blk < ns)
        forced = (blk == 0) | (blk == cur) | (blk == cur - 1)
        return jnp.where(valid, jnp.where(forced, jnp.inf, imp), -jnp.inf)

    if packed:
        half = LANE // 2
        lane = _iota((1, LANE), 1)
        low = lane < half
        blk = lane & (half - 1)
        score = scores(_dot(pcs[0], m_ref[0], precision=HI) + _dot(pcs[1], m_ref[1], precision=HI), blk)
        st = score.T
        row8 = _iota((SUBLANE, 1), 0)
        tiles = []
        for hf in range(2):
            for v in range(half // SUBLANE):
                if SUBLANE * v >= ns:
                    tiles.append(jnp.full((SUBLANE, qb), -1.0, F32))
                    continue
                tile = st[hf * half + SUBLANE * v:hf * half + SUBLANE * (v + 1), :]
                acc = jnp.zeros((SUBLANE, qb), F32)
                for j in range(ns):
                    sj = st[hf * half + j:hf * half + j + 1, :]
                    ge = jnp.where(sj >= tile, 1.0, 0.0)
                    gt = jnp.where(sj > tile, 1.0, 0.0)
                    if SUBLANE * v > j:
                        acc = acc + ge
                    elif SUBLANE * v + SUBLANE - 1 < j:
                        acc = acc + gt
                    else:
                        acc = acc + jnp.where(row8 + SUBLANE * v > j, ge, gt)
                tiles.append(jnp.where(acc < nsel, 0.0, -1.0))
        selm = jnp.concatenate(tiles, axis=0).T
        sel_ref[:, 0:LANE] = jnp.where(low, selm, -1.0).astype(sel_ref.dtype)
        sel_ref[:, LANE:2 * LANE] = jnp.where(low, pltpu.roll(selm, half, 1), -1.0).astype(sel_ref.dtype)
    else:
        blk = _iota((1, nsp), 1)
        for g in range(NSA_G):
            score = scores(_dot(pcs[g], m_ref[0], precision=HI), blk)
            rank = _rank(score, lambda j: score[:, j:j + 1], blk, ns)
            sel_ref[:, g * nsp:(g + 1) * nsp] = jnp.where(rank < nsel, 0.0, -1.0).astype(sel_ref.dtype)


def cmp_select(q, kvc, nb_batch, nqp, qb, pos0, ns, nsp, nsel):
    ncp = kvc.shape[1]
    nblk = nqp // qb
    packed = 2 * ns <= LANE
    mmat = importance_matrix(ncp, nsp, packed)
    body = functools.partial(_cmp_body, qb=qb, pos0=pos0, ncp=ncp, ns=ns, nsp=nsp, nsel=nsel, packed=packed)
    return pl.pallas_call(
        body, grid=(nb_batch, nblk),
        in_specs=[pl.BlockSpec((qb, 1024), lambda b, i: (b * nblk + i, 0)),
                  pl.BlockSpec((None, ncp, 512), lambda b, i: (b, 0, 0)),
                  pl.BlockSpec(mmat.shape, lambda b, i: (0, 0, 0))],
        out_specs=[pl.BlockSpec((qb, 1024), lambda b, i: (b * nblk + i, 0)),
                   pl.BlockSpec((qb, NSA_G * nsp), lambda b, i: (b * nblk + i, 0))],
        out_shape=[jax.ShapeDtypeStruct((nb_batch * nqp, 1024), F32),
                   jax.ShapeDtypeStruct((nb_batch * nqp, NSA_G * nsp), BF16)],
        compiler_params=_cp(("parallel", "parallel")), name="cmp_select")(q, kvc, mmat)


def importance_matrix(ncp, nsp, packed):
    m = np.zeros((2 if packed else 1, ncp, nsp), np.float32)
    for j in range(nsp // 2 if packed else nsp):
        for n, w in ((4 * j - 1, 1.0), (4 * j, 2.0), (4 * j + 1, 2.0), (4 * j + 2, 2.0), (4 * j + 3, 1.0)):
            if 0 <= n < ncp:
                m[0, n, j] = w
                if packed:
                    m[1, n, nsp // 2 + j] = w
    return jnp.asarray(m)


MASK_BIG = 2.0 ** 100


def expand_matrices():
    e = np.zeros((WIN_PER_LANE, LANE, KC), np.float32)
    for w in range(WIN_PER_LANE):
        for key in range(KC):
            e[w, BLK_PER_KC * w + key // SLC_BLOCK, key] = MASK_BIG
    return jnp.asarray(e, BF16)


def _slc_body(q_ref, sel0_ref, sel1_ref, e_ref, kv_ref, o_ref, m_sc, l_sc, acc_sc, *, qb, pos0):
    i = pl.program_id(1)
    c = pl.program_id(2)

    @pl.when(c == 0)
    def _():
        m_sc[...] = jnp.full(m_sc.shape, NEG_INF, F32)
        l_sc[...] = jnp.zeros(l_sc.shape, F32)
        acc_sc[...] = jnp.zeros(acc_sc.shape, F32)

    def step(diagonal):
        gs = range(NSA_G)
        sels = (sel0_ref, sel1_ref)
        ew = e_ref[c % WIN_PER_LANE]
        bias = [_dot(sels[g][...], ew) for g in gs]
        if diagonal:
            t1 = pos0 + i * qb + _iota((qb, 1), 0)
            causal = jnp.where(c * KC + _iota((1, KC), 1) <= t1, 0.0, NEG_INF)
            bias = [b + causal for b in bias]
        s = [_dot_nt(_stack_heads(q_ref, g), kv_ref[:, g * DH:(g + 1) * DH])
             + jnp.concatenate([bias[g]] * NSA_GROUP, axis=0) for g in gs]
        m_old = [m_sc[g] for g in gs]
        m_new = [jnp.maximum(m_old[g], jnp.max(s[g], axis=-1, keepdims=True)) for g in gs]
        p = [jnp.exp(s[g] - m_new[g]) for g in gs]
        alpha = [jnp.exp(m_old[g] - m_new[g]) for g in gs]
        pv = [_dot(p[g].astype(BF16), kv_ref[:, (2 + g) * DH:(3 + g) * DH]) for g in gs]
        for g in gs:
            l_sc[g] = alpha[g] * l_sc[g] + jnp.sum(p[g], axis=-1, keepdims=True)
            acc_sc[g] = alpha[g] * acc_sc[g] + pv[g]
            m_sc[g] = m_new[g]

    first_partial = (pos0 + i * qb) // KC
    last_needed = (pos0 + i * qb + qb - 1) // KC
    pl.when(c < first_partial)(functools.partial(step, False))
    pl.when((c >= first_partial) & (c <= last_needed))(functools.partial(step, True))

    @pl.when(c == pl.num_programs(2) - 1)
    def _():
        for g in range(NSA_G):
            _unstack_heads(o_ref, g, acc_sc[g] / l_sc[g], qb)


def slc_attention(q, sel, emat, kvs, nb_batch, nqp, qb, pos0, nsp):
    nblk = nqp // qb
    lanes_g = nsp // LANE
    chunks_per_seq = kvs.shape[0] // (nb_batch * KC)
    nchunk = (pos0 + nqp - 1) // KC + 1
    last = lambda i: (pos0 + i * qb + qb - 1) // KC
    q_map = lambda b, i, c: (b * nblk + i, 0)
    sel_specs = [pl.BlockSpec((qb, LANE), functools.partial(
        lambda b, i, c, g: (b * nblk + i, g * lanes_g + jnp.minimum(c, last(i)) // WIN_PER_LANE), g=g))
        for g in range(NSA_G)]
    return pl.pallas_call(
        functools.partial(_slc_body, qb=qb, pos0=pos0), grid=(nb_batch, nblk, nchunk),
        in_specs=[pl.BlockSpec((qb, 1024), q_map)] + sel_specs
                 + [pl.BlockSpec((WIN_PER_LANE, LANE, KC), lambda b, i, c: (0, 0, 0)),
                    pl.BlockSpec((KC, 512), lambda b, i, c: (b * chunks_per_seq + jnp.minimum(c, last(i)), 0))],
        out_specs=pl.BlockSpec((qb, 1024), q_map),
        out_shape=jax.ShapeDtypeStruct((nb_batch * nqp, 1024), F32),
        scratch_shapes=[pltpu.VMEM((NSA_G, NSA_GROUP * qb, 1), F32), pltpu.VMEM((NSA_G, NSA_GROUP * qb, 1), F32),
                        pltpu.VMEM((NSA_G, NSA_GROUP * qb, DH), F32)],
        compiler_params=_cp(("parallel", "parallel", "arbitrary")), name="slc_attention")(q, sel, sel, emat, kvs)


def _win_body(q_ref, w_ref, o_ref, *, qb, pos0, span):
    i = pl.program_id(1)
    kv = w_ref[pl.ds(pl.multiple_of(i * qb, qb), span), :].astype(BF16)
    qpos = pos0 + i * qb + _iota((qb, 1), 0)
    kpos = pos0 - WINDOW + i * qb + _iota((1, span), 1)
    rel = qpos - kpos
    mask = (rel >= 0) & (rel <= WINDOW) & (kpos >= 0)
    bias = jnp.concatenate([jnp.where(mask, 0.0, NEG_INF)] * NSA_GROUP, axis=0)
    for g in range(NSA_G):
        s = _dot_nt(_stack_heads(q_ref, g), kv[:, g * DH:(g + 1) * DH]) + bias
        e = jnp.exp(s - jnp.max(s, axis=-1, keepdims=True))
        p = e * (1.0 / jnp.sum(e, axis=-1, keepdims=True))
        _unstack_heads(o_ref, g, _dot(p.astype(BF16), kv[:, (2 + g) * DH:(3 + g) * DH]), qb)


def win_attention(q, win_all, nb_batch, nqp, qb, pos0):
    nblk = nqp // qb
    wtot = win_all.shape[1]
    body = functools.partial(_win_body, qb=qb, pos0=pos0, span=WINDOW + qb)
    return pl.pallas_call(
        body, grid=(nb_batch, nblk),
        in_specs=[pl.BlockSpec((qb, 1024), lambda b, i: (b * nblk + i, 0)),
                  pl.BlockSpec((None, wtot, 512), lambda b, i: (b, 0, 0))],
        out_specs=pl.BlockSpec((qb, 1024), lambda b, i: (b * nblk + i, 0)),
        out_shape=jax.ShapeDtypeStruct((nb_batch * nqp, 1024), F32),
        compiler_params=_cp(("parallel", "parallel")), name="win_attention")(q, win_all)


def _combine_body(oc_ref, os_ref, ow_ref, zs_ref, o_ref):
    gate = jax.nn.sigmoid(zs_ref[...])
    for h in range(NSA_HEADS):
        sl = slice(h * DH, (h + 1) * DH)
        c0 = SM_NG + 3 * h
        o = (gate[:, c0:c0 + 1] * oc_ref[:, sl] + gate[:, c0 + 1:c0 + 2] * os_ref[:, sl]
             + gate[:, c0 + 2:c0 + 3] * ow_ref[:, sl])
        o_ref[:, sl] = o.astype(o_ref.dtype)


def nsa_combine(oc, osl, ow, z, tb):
    t = oc.shape[0]
    spec = pl.BlockSpec((tb, 1024), lambda i: (i, 0))
    return pl.pallas_call(
        _combine_body, grid=(t // tb,),
        in_specs=[spec, spec, spec, pl.BlockSpec((tb, LANE), lambda i: (i, Z_SM // LANE))],
        out_specs=spec, out_shape=jax.ShapeDtypeStruct((t, 1024), BF16),
        compiler_params=_cp(("parallel",)), name="nsa_combine")(oc, osl, ow, z)


def _pad_rows(x, n):
    if x.shape[0] == n:
        return x
    return jnp.concatenate([x, jnp.zeros((n - x.shape[0], x.shape[1]), x.dtype)], axis=0)


def _cumsum_rows(x):
    n = x.shape[0]
    tri = jnp.where(_iota((n, n), 0) >= _iota((n, n), 1), 1.0, 0.0).astype(BF16)
    hi, lo = _split(x)
    lo2 = (x - hi.astype(F32) - lo.astype(F32)).astype(BF16)
    return _dot(tri, hi) + (_dot(tri, lo) + _dot(tri, lo2))


def _log_sigmoid(x):
    return jnp.minimum(x, 0.0) - jnp.log1p(jnp.exp(-jnp.abs(x)))


def _softplus(x):
    return jnp.maximum(x, 0.0) + jnp.log1p(jnp.exp(-jnp.abs(x)))


def _mlstm_body(q_ref, k_ref, v_ref, og_ref, zs_ref, b_ref, ng_ref, c0_ref, n0_ref, m0_ref,
                o_ref, c_out, n_out, m_out, c_sc, n_sc, m_sc, *, nvalid):
    ck = pl.program_id(1)
    n = CHUNK
    lin = q_ref.shape[0]

    @pl.when(ck == 0)
    def _():
        c_sc[...] = c0_ref[...]
        n_sc[...] = n0_ref[...]
        m_sc[...] = m0_ref[...]

    gates = _pad_rows(zs_ref[...], n) + b_ref[...]
    lane = _iota((1, LANE), 1)
    is_f = (lane >= SM_MF) & (lane < SM_MF + ML_H)
    lf_all = jnp.where(is_f, _log_sigmoid(gates), 0.0)
    ig_all = gates
    if nvalid < n:
        live = _iota((n, 1), 0) < nvalid
        lf_all = jnp.where(live, lf_all, 0.0)
        ig_all = jnp.where(live, gates, NEG_INF)
    comb = jnp.where(is_f, _cumsum_rows(lf_all), ig_all)
    comb_t = comb.T
    incl = _iota((n, n), 0) >= _iota((n, n), 1)
    hs = range(ML_H)
    sk = [slice(h * ML_DK, (h + 1) * ML_DK) for h in hs]
    sv = [slice(h * ML_DV, (h + 1) * ML_DV) for h in hs]
    c0 = [c_sc[h] for h in hs]
    n0 = [n_sc[h] for h in hs]
    m0 = [m_sc[h] for h in hs]
    q = [_pad_rows(q_ref[:, sk[h]], n) for h in hs]
    k = [_pad_rows(k_ref[:, sk[h]], n) * (ML_DK ** -0.5) for h in hs]
    q16 = [q[h].astype(BF16) for h in hs]
    v16 = [_pad_rows(v_ref[:, sv[h]], n).astype(BF16) for h in hs]
    ig = [comb[:, SM_MI + h:SM_MI + h + 1] for h in hs]
    bc = [comb[:, SM_MF + h:SM_MF + h + 1] for h in hs]
    dlog = [jnp.where(incl, bc[h] + (comb_t[SM_MI + h:SM_MI + h + 1, :] - comb_t[SM_MF + h:SM_MF + h + 1, :]),
                      -jnp.inf) for h in hs]
    qk = [_dot_nt(q16[h], k[h].astype(BF16)) for h in hs]
    qc = [_dot(q16[h], c0[h].astype(BF16)) for h in hs]
    inter = [bc[h] + m0[h] for h in hs]
    m_t = [jnp.maximum(inter[h], jnp.max(dlog[h], axis=1, keepdims=True)) for h in hs]
    a = [qk[h] * jnp.exp(dlog[h] - m_t[h]) for h in hs]
    w_inter = [jnp.exp(inter[h] - m_t[h]) for h in hs]
    num = [_dot(a[h].astype(BF16), v16[h]) + w_inter[h] * qc[h] for h in hs]
    den = [jnp.sum(a[h], axis=1, keepdims=True) + w_inter[h] * jnp.sum(q[h] * n0[h], axis=1, keepdims=True)
           for h in hs]
    hh = [num[h] / jnp.maximum(jnp.abs(den[h]), jnp.exp(-m_t[h])) for h in hs]
    m_new = [m_t[h][n - 1:n] for h in hs]
    kw = [k[h] * jnp.exp(bc[h][n - 1:n] - bc[h] + ig[h] - m_new[h]) for h in hs]
    decay = [jnp.exp(bc[h][n - 1:n] + m0[h] - m_new[h]) for h in hs]
    c_new = [decay[h] * c0[h] + _dot(kw[h].T.astype(BF16), v16[h]) for h in hs]
    n_new = [decay[h] * n0[h] + jnp.sum(kw[h], axis=0, keepdims=True) for h in hs]
    for h in hs:
        c_sc[h] = c_new[h]
        n_sc[h] = n_new[h]
        m_sc[h] = m_new[h]
        c_out[h] = c_new[h]
        n_out[h] = n_new[h]
        m_out[h] = m_new[h]
        hn = hh[h] * lax.rsqrt(jnp.mean(hh[h] * hh[h], axis=-1, keepdims=True) + EPS) * ng_ref[:, sv[h]]
        o = jax.nn.sigmoid(_pad_rows(og_ref[:, sv[h]], n)) * hn
        o_ref[:, sv[h]] = o[:lin].astype(o_ref.dtype)


def mlstm_scan(z, bias_row, norm_g, c0, n0, m0, nb_batch, nqp, nvalid):
    lin = min(CHUNK, nqp)
    nc = nqp // lin
    tok = lambda w, off: pl.BlockSpec((lin, w), lambda b, c: (b * nc + c, off // w))
    body = functools.partial(_mlstm_body, nvalid=min(nvalid, CHUNK))
    st = lambda shape: pl.BlockSpec((None, ML_H) + shape, lambda b, c: (b, 0, 0, 0))
    wq, wv = ML_H * ML_DK, ML_H * ML_DV
    return pl.pallas_call(
        body, grid=(nb_batch, nc),
        in_specs=[tok(wq, Z_MQ), tok(wq, Z_MK), tok(wv, Z_MV), tok(wv, Z_MO), tok(LANE, Z_SM),
                  pl.BlockSpec((1, LANE), lambda b, c: (0, 0)), pl.BlockSpec((1, wv), lambda b, c: (0, 0)),
                  st((ML_DK, ML_DV)), st((1, ML_DK)), st((1, 1))],
        out_specs=[pl.BlockSpec((lin, wv), lambda b, c: (b * nc + c, 0)),
                   st((ML_DK, ML_DV)), st((1, ML_DK)), st((1, 1))],
        out_shape=[jax.ShapeDtypeStruct((nb_batch * nqp, wv), BF16),
                   jax.ShapeDtypeStruct((nb_batch, ML_H, ML_DK, ML_DV), F32),
                   jax.ShapeDtypeStruct((nb_batch, ML_H, 1, ML_DK), F32),
                   jax.ShapeDtypeStruct((nb_batch, ML_H, 1, 1), F32)],
        scratch_shapes=[pltpu.VMEM((ML_H, ML_DK, ML_DV), F32), pltpu.VMEM((ML_H, 1, ML_DK), F32),
                        pltpu.VMEM((ML_H, 1, 1), F32)],
        compiler_params=_cp(("parallel", "arbitrary")),
        name="mlstm_scan")(z, z, z, z, z, bias_row, norm_g.reshape(1, wv), c0, n0, m0)


def _gdn_body(q_ref, k_ref, v_ref, gz_ref, zs_ref, b_ref, a_ref, ng_ref, s0_ref,
              o_ref, s_out, s_sc, *, nvalid):
    ck = pl.program_id(1)
    n = CHUNK
    lin = q_ref.shape[0]

    @pl.when(ck == 0)
    def _():
        s_sc[...] = s0_ref[...]

    zs = _pad_rows(zs_ref[...], n)
    lane = _iota((1, LANE), 1)
    is_a = (lane >= SM_GA) & (lane < SM_GA + GD_H)
    g_all = jnp.where(is_a, a_ref[...] * _softplus(zs + b_ref[...]), 0.0)
    beta_all = jax.nn.sigmoid(zs)
    if nvalid < n:
        live = _iota((n, 1), 0) < nvalid
        g_all = jnp.where(live, g_all, 0.0)
        beta_all = jnp.where(live, beta_all, 0.0)
    gc_all = _cumsum_rows(g_all)
    gc_t = gc_all.T
    incl = _iota((n, n), 0) >= _iota((n, n), 1)
    strict = _iota((n, n), 0) > _iota((n, n), 1)
    eye = jnp.where(_iota((n, n), 0) == _iota((n, n), 1), 1.0, 0.0)
    hs = range(GD_H)
    sl = [slice(h * GD_D, (h + 1) * GD_D) for h in hs]
    s0 = [s_sc[h] for h in hs]
    q16 = [_pad_rows(q_ref[:, sl[h]], n).astype(BF16) for h in hs]
    k = [_pad_rows(k_ref[:, sl[h]], n) for h in hs]
    k16 = [k[h].astype(BF16) for h in hs]
    gc = [gc_all[:, SM_GA + h:SM_GA + h + 1] for h in hs]
    beta = [beta_all[:, SM_GB + h:SM_GB + h + 1] for h in hs]
    dec = [jnp.where(incl, jnp.exp(jnp.where(incl, gc[h] - gc_t[SM_GA + h:SM_GA + h + 1, :], 0.0)), 0.0) for h in hs]
    egc = [jnp.exp(gc[h]) for h in hs]
    kb = [k[h] * beta[h] for h in hs]
    xp = [-jnp.where(strict, _dot_nt(kb[h].astype(BF16), k16[h]) * dec[h], 0.0) for h in hs]
    tinv = [eye + xp[h] for h in hs]
    for _ in range(int(math.log2(n)) - 1):
        xp = [_dot3(xp[h], xp[h]) for h in hs]
        tinv = [tinv[h] + _dot3(tinv[h], xp[h]) for h in hs]
    rhs = [jnp.concatenate([_pad_rows(v_ref[:, sl[h]], n) * beta[h], kb[h] * egc[h]], axis=1) for h in hs]
    sol = [_dot3(tinv[h], rhs[h]) for h in hs]
    s016 = [s0[h].astype(BF16) for h in hs]
    u16 = [(sol[h][:, :GD_D] - _dot(sol[h][:, GD_D:].astype(BF16), s016[h])).astype(BF16) for h in hs]
    qk = [(_dot_nt(q16[h], k16[h]) * dec[h]).astype(BF16) for h in hs]
    o = [_dot(qk[h], u16[h]) + egc[h] * _dot(q16[h], s016[h]) for h in hs]
    kw = [(k[h] * jnp.exp(gc[h][n - 1:n] - gc[h])).T.astype(BF16) for h in hs]
    s_new = [jnp.exp(gc[h][n - 1:n]) * s0[h] + _dot(kw[h], u16[h]) for h in hs]
    for h in hs:
        s_sc[h] = s_new[h]
        s_out[h] = s_new[h]
        on = o[h] * lax.rsqrt(jnp.mean(o[h] * o[h], axis=-1, keepdims=True) + EPS) * ng_ref[...]
        o_ref[:, sl[h]] = (on * _silu(_pad_rows(gz_ref[:, sl[h]], n)))[:lin].astype(o_ref.dtype)


def gdn_scan(qkv, z, dt_row, nega_row, norm_g, s0, nb_batch, nqp, nvalid):
    lin = min(CHUNK, nqp)
    nc = nqp // lin
    w = GD_H * GD_D
    body = functools.partial(_gdn_body, nvalid=min(nvalid, CHUNK))
    tokq = lambda part: pl.BlockSpec((lin, w), lambda b, c: (b * nc + c, part))
    st = pl.BlockSpec((None, GD_H, GD_D, GD_D), lambda b, c: (b, 0, 0, 0))
    row = pl.BlockSpec((1, LANE), lambda b, c: (0, 0))
    return pl.pallas_call(
        body, grid=(nb_batch, nc),
        in_specs=[tokq(0), tokq(1), tokq(2),
                  pl.BlockSpec((lin, w), lambda b, c: (b * nc + c, Z_GZ // w)),
                  pl.BlockSpec((lin, LANE), lambda b, c: (b * nc + c, Z_SM // LANE)),
                  row, row, row, st],
        out_specs=[pl.BlockSpec((lin, w), lambda b, c: (b * nc + c, 0)), st],
        out_shape=[jax.ShapeDtypeStruct((nb_batch * nqp, w), BF16),
                   jax.ShapeDtypeStruct((nb_batch, GD_H, GD_D, GD_D), F32)],
        scratch_shapes=[pltpu.VMEM((GD_H, GD_D, GD_D), F32)],
        compiler_params=_cp(("parallel", "arbitrary")),
        name="gdn_scan")(qkv, qkv, qkv, z, z, dt_row, nega_row, norm_g.reshape(1, GD_D), s0)


HALO = 8
FFN_SUBTILES = 8


def _conv_apply(buf, x, w_ref, width):
    tb = x.shape[0]
    buf[HALO:HALO + tb, :] = x
    y = w_ref[width - 1:width, :] * x
    for kk in range(1, width):
        y = y + w_ref[width - 1 - kk:width - kk, :] * buf[HALO - kk:HALO - kk + tb, :]
    return y


def _conv_rows(buf, x, hist_ref, w_ref, width, first):
    @pl.when(first)
    def _():
        buf[0:HALO, :] = hist_ref[...]

    return _conv_apply(buf, x, w_ref, width)


def _gdn_prep_body(z_ref, hist_ref, w_ref, o_ref, buf):
    ct = pl.program_id(1)
    tb = z_ref.shape[0]
    x = z_ref[...]
    y = _silu(_conv_rows(buf, x, hist_ref, w_ref, GDN_CONV, pl.program_id(2) == 0))
    buf[0:HALO, :] = x[tb - HALO:tb, :]
    part = ct // (GD_H * GD_D // y.shape[1])
    qscale = jnp.where(part == 0, GD_D ** -0.5, 1.0)
    for hh in range(y.shape[1] // GD_D):
        sl = slice(hh * GD_D, (hh + 1) * GD_D)
        yh = y[:, sl]
        nrm = yh * lax.rsqrt(jnp.sum(yh * yh, axis=-1, keepdims=True) + EPS) * qscale
        o_ref[:, sl] = jnp.where(part == 2, yh, nrm)


def gdn_prep(z, hist, w, nb_batch, nqp, tb, tc=512):
    nblk = nqp // tb
    ncol = 3 * GD_H * GD_D
    return pl.pallas_call(
        _gdn_prep_body, grid=(nb_batch, ncol // tc, nblk),
        in_specs=[pl.BlockSpec((tb, tc), lambda b, c, i: (b * nblk + i, Z_GQKV // tc + c)),
                  pl.BlockSpec((None, HALO, tc), lambda b, c, i: (b, 0, c)),
                  pl.BlockSpec((GDN_CONV, tc), lambda b, c, i: (0, c))],
        out_specs=pl.BlockSpec((tb, tc), lambda b, c, i: (b * nblk + i, c)),
        out_shape=jax.ShapeDtypeStruct((nb_batch * nqp, ncol), F32),
        scratch_shapes=[pltpu.VMEM((HALO + tb, tc), F32)],
        compiler_params=_cp(("parallel", "parallel", "arbitrary")), name="gdn_prep")(z, hist, w)


def _ffn_up_body(h_ref, wa_ref, wb_ref, ca_ref, cb_ref, ha_ref, hb_ref, act_ref, ta_ref, tb_ref, bufa, bufb,
                 *, bps, nseq, tail0):
    i = pl.program_id(1)
    tm = h_ref.shape[0]
    halves = ((wa_ref, ca_ref, ha_ref, ta_ref, bufa), (wb_ref, cb_ref, hb_ref, tb_ref, bufb))
    if nseq == 1:
        nsub = FFN_SUBTILES if tm % (FFN_SUBTILES * HALO) == 0 else 1
        rs = tm // nsub

        @pl.when((i % bps) == 0)
        def _():
            for _, _, hist_ref, _, buf in halves:
                buf[0:HALO, :] = hist_ref[0]

        def conv_sub(buf, c_ref, off):
            y = c_ref[FFN_CONV - 1:FFN_CONV, :] * buf[HALO + off:HALO + off + rs, :]
            for kk in range(1, FFN_CONV):
                y = y + c_ref[FFN_CONV - 1 - kk:FFN_CONV - kk, :] * buf[HALO + off - kk:HALO + off - kk + rs, :]
            return y

        for r in range(nsub + 1):
            if r < nsub:
                hr = h_ref[r * rs:(r + 1) * rs, :]
                for w_ref, _, _, _, buf in halves:
                    buf[HALO + r * rs:HALO + (r + 1) * rs, :] = _dot(hr, w_ref[...])
            if r > 0:
                off = (r - 1) * rs
                act_ref[off:off + rs, :] = (_silu(conv_sub(bufa, ca_ref, off))
                                            * conv_sub(bufb, cb_ref, off)).astype(act_ref.dtype)
        for _, _, _, t_ref, buf in halves:
            t_ref[0] = buf[HALO + tail0:HALO + tail0 + HALO, :]
            buf[0:HALO, :] = buf[tm:tm + HALO, :]
    else:
        rows = tm // nseq
        h = h_ref[...]
        ys = []
        for w_ref, c_ref, hist_ref, t_ref, buf in halves:
            u = _dot(h, w_ref[...])
            parts = []
            for s in range(nseq):
                buf[0:HALO, :] = hist_ref[s]
                parts.append(_conv_apply(buf, u[s * rows:(s + 1) * rows], c_ref, FFN_CONV))
                t_ref[s] = u[s * rows + tail0:s * rows + tail0 + HALO, :]
            ys.append(jnp.concatenate(parts, axis=0))
        act_ref[...] = (_silu(ys[0]) * ys[1]).astype(act_ref.dtype)


def ffn_up_act(h, w_up, layer, conv_w, hist, nb_batch, nq, nqp, tm, tn):
    t, d = h.shape
    nj = FFN_PAD // tn
    bps = max(1, nqp // tm)
    nseq = max(1, tm // nqp)
    rows = tm // nseq
    assert (nq - 1) // rows == bps - 1 and (nq - 1) % HALO >= FFN_CONV - 2
    tail0 = ((nq - 1) % rows) // HALO * HALO
    body = functools.partial(_ffn_up_body, bps=bps, nseq=nseq, tail0=tail0)
    col = lambda half: (lambda j, i: (0, half * nj + j))
    seq = lambda half: (lambda j, i: (i // bps, 0, half * nj + j))
    tail = jax.ShapeDtypeStruct((nb_batch, HALO, FFN_PAD), F32)
    return pl.pallas_call(
        body, grid=(nj, t // tm),
        in_specs=[pl.BlockSpec((tm, d), lambda j, i: (i, 0)),
                  pl.BlockSpec((None, d, tn), lambda j, i: (layer, 0, j)),
                  pl.BlockSpec((None, d, tn), lambda j, i: (layer, 0, nj + j)),
                  pl.BlockSpec((FFN_CONV, tn), col(0)), pl.BlockSpec((FFN_CONV, tn), col(1)),
                  pl.BlockSpec((nseq, HALO, tn), seq(0)), pl.BlockSpec((nseq, HALO, tn), seq(1))],
        out_specs=[pl.BlockSpec((tm, tn), lambda j, i: (i, j)),
                   pl.BlockSpec((nseq, HALO, tn), lambda j, i: (i // bps, 0, j)),
                   pl.BlockSpec((nseq, HALO, tn), lambda j, i: (i // bps, 0, j))],
        out_shape=[jax.ShapeDtypeStruct((t, FFN_PAD), BF16), tail, tail],
        scratch_shapes=[pltpu.VMEM((HALO + tm, tn), F32), pltpu.VMEM((HALO + tm, tn), F32)],
        compiler_params=_cp(("parallel", "arbitrary")),
        name="ffn_up_act")(h, w_up, w_up, conv_w, conv_w, hist, hist)


def _merge_body(oa_ref, ob_ref, oc_ref, w_ref, g0_ref, g1_ref, g2_ref, o_ref):
    acc = jax.nn.sigmoid(g0_ref[...]) * _dot(oa_ref[...], w_ref[0])
    acc = acc + jax.nn.sigmoid(g1_ref[...]) * _dot(ob_ref[...], w_ref[1])
    acc = acc + jax.nn.sigmoid(g2_ref[...]) * _dot(oc_ref[...], w_ref[2])
    o_ref[...] = acc.astype(o_ref.dtype)


def merge_branches(oa, ob, oc, wb, layer, z, tm, tn):
    t = oa.shape[0]
    d = wb.shape[3]
    tok = pl.BlockSpec((tm, 1024), lambda i, j: (i, 0))
    gate = lambda nbr: pl.BlockSpec((tm, tn), lambda i, j: (i, (Z_BR + nbr * d) // tn + j))
    return pl.pallas_call(
        _merge_body, grid=(t // tm, d // tn),
        in_specs=[tok, tok, tok, pl.BlockSpec((None, 3, 1024, tn), lambda i, j: (layer, 0, 0, j)),
                  gate(0), gate(1), gate(2)],
        out_specs=pl.BlockSpec((tm, tn), lambda i, j: (i, j)),
        out_shape=jax.ShapeDtypeStruct((t, d), BF16),
        compiler_params=_cp(("parallel", "parallel")), name="merge_branches")(oa, ob, oc, wb, z, z, z)


def _pad_halves(a, f):
    pad = ((0, 0),) * (a.ndim - 1) + ((0, FFN_PAD - f),)
    return jnp.concatenate([jnp.pad(a[..., :f], pad), jnp.pad(a[..., f:], pad)], axis=-1)


def _stacked_weights(w_in, w_up, w_down, w_branch, w_out):
    offs = np.cumsum([0, 1024, 1536, 24, 512, 512, 1024, 4, 4, 1024, 3072, 8, 8, 1024, 6144])
    lead = w_in.shape[:2]
    seg = lambda i: w_in[:, :, offs[i]:offs[i + 1]]
    kv = seg(1).reshape(lead + (3, 2, NSA_G * DH))
    small = jnp.concatenate([seg(2), seg(6), seg(7), seg(10), seg(11)], axis=-1)
    small = jnp.pad(small, ((0, 0), (0, 0), (0, Z_COLS - Z_SM - small.shape[-1])))
    gap = jnp.zeros(lead + (Z_KV - Z_MQ - ML_H * ML_DK,), w_in.dtype)
    w_all = jnp.concatenate([seg(0), seg(5), seg(8), seg(12), seg(9), seg(13), seg(3), gap,
                             kv[:, :, :, 0].reshape(lead + (KV_W // 2,)), kv[:, :, :, 1].reshape(lead + (KV_W // 2,)),
                             seg(4), small], axis=-1)
    assert w_all.shape[-1] == Z_COLS
    f = w_down.shape[1]
    return dict(w_all=w_all.astype(BF16), w_up=_pad_halves(w_up, f).astype(BF16),
                w_down=jnp.pad(w_down, ((0, 0), (0, FFN_PAD - f), (0, 0))).astype(BF16),
                w_branch=w_branch.astype(BF16), w_out=w_out.astype(BF16))


def _layer_weights(p):
    w1 = p['w_cmp1']
    wcat = jnp.concatenate([w1[:, :CMP_STRIDE].reshape(2, CMP_STRIDE * DH, DH),
                            w1[:, CMP_STRIDE:].reshape(2, CMP_STRIDE * DH, DH)], axis=2).astype(BF16)

    def lane_row(pairs):
        row = jnp.zeros((LANE,), F32)
        for off, v in pairs:
            row = row.at[off:off + v.shape[0]].set(v.astype(F32))
        return row.reshape(1, LANE)

    return dict(
        ffn_conv_w=_pad_halves(p['ffn_conv_w'], p['ffn_conv_w'].shape[-1] // 2), wcat=wcat,
        pe8=jnp.pad(p['cmp_pe'].reshape(2, 1, CMP_BLOCK * DH), ((0, 0), (0, 7), (0, 0))).astype(BF16),
        w1flat=w1.reshape(2, CMP_BLOCK * DH, DH).astype(BF16), w2=p['w_cmp2'].astype(BF16),
        ml_bias=lane_row([(SM_MI, p['mlstm_b_i']), (SM_MF, p['mlstm_b_f'])]),
        gd_dt=lane_row([(SM_GA, p['gdn_dt_bias'])]),
        gd_nega=lane_row([(SM_GA, -jnp.exp(p['gdn_a_log'].astype(F32)))]))


def _hist_rows(state):
    return jnp.pad(state, ((0, 0), (HALO - state.shape[1], 0), (0, 0)))


def _group_layer(x, grp, st, p, w, ws, layer):
    nb, nq, nqp, pos0 = grp['nb'], grp['nq'], grp['nqp'], grp['pos0']
    t = nb * nqp
    big = nqp >= 512
    tm = 1024 if big else t
    h = rmsnorm_bf16(x, p['norm1_g'], min(tm, 256))
    z = matmul(h, ws['w_all'], layer, tm, 1024 if big else 2048, name="in_proj")

    tb = 256 if big else nqp
    q_rot, rows, kvs, win_new = nsa_prep(z, grp['rope'], p['q_norm_g'], p['k_norm_g'], nqp, tb)
    pps = KC // PAGE
    if pos0 == 0:
        nblk_c = nqp // CMP_STRIDE
        ncp = nblk_c
        fs = compress_pages(rows.reshape(nb * nqp // PAGE, PAGE_SUB, LANE), w['wcat'], nb, nqp // KC, pps)
    else:
        n_pages = st['table'].shape[1]
        nblk_c = (n_pages + 1) * (PAGE // CMP_STRIDE)
        ncp = -(-nblk_c // LANE) * LANE
        live = (jnp.arange(nqp * ROW_SUB) < nq * ROW_SUB)[None, :, None]
        newpage = jnp.pad(jnp.where(live, rows.reshape(nb, nqp * ROW_SUB, LANE), 0.0),
                          ((0, 0), (0, PAGE_SUB - nqp * ROW_SUB), (0, 0)))
        fs, kvs = compress_pages(st['cache'], w['wcat'], nb, ncp * CMP_STRIDE // KC, pps, table=st['table'],
                                 layer=layer, newpage=newpage)
        kvs = kvs.reshape(-1, 512)
    kvc = compress_finish(fs, w['pe8'], w['w1flat'], w['w2'])
    ns = nblk_c * CMP_STRIDE // SLC_BLOCK
    nsp = -(-ns // LANE) * LANE
    t_all = pos0 + nq
    nsel = min(N_SELECT, -(-t_all // SLC_BLOCK))
    qb = 128 if big else nqp
    o_cmp, sel = cmp_select(q_rot, kvc, nb, nqp, qb, pos0, ns, nsp, nsel)
    o_slc = slc_attention(q_rot, sel, expand_matrices(), kvs, nb, nqp, qb, pos0, nsp)
    win3 = win_new.reshape(nb, nqp, 512)
    win_all = jnp.concatenate([st['win'], win3], axis=1)
    o_win = win_attention(q_rot, win_all, nb, nqp, qb, pos0)
    o_nsa = nsa_combine(o_cmp, o_slc, o_win, z, tb)
    win_state = jnp.concatenate([st['win'], win3[:, :nq]], axis=1)[:, -WINDOW:]

    o_ml, m_c, m_n, m_m = mlstm_scan(z, w['ml_bias'], p['mlstm_norm_g'], st['m_c'], st['m_n'], st['m_m'],
                                     nb, nqp, nq)

    tbc = 512 if big else nqp
    qkv = gdn_prep(z, _hist_rows(st['g_conv']), p['gdn_conv_w'], nb, nqp, tbc)
    o_gd, g_s = gdn_scan(qkv, z, w['gd_dt'], w['gd_nega'], p['gdn_norm_g'], st['g_s'], nb, nqp, nq)
    z3 = z.reshape(nb, nqp, Z_COLS)
    g_conv_new = z3[:, nq - (GDN_CONV - 1):nq, Z_GQKV:Z_GQKV + 3 * GD_H * GD_D]

    mrg = merge_branches(o_nsa, o_ml, o_gd, ws['w_branch'], layer, z, tm, 512)
    x1 = matmul(mrg, ws['w_out'], layer, tm, 1024 if big else 2048, res=x, name="out_proj")

    h2 = rmsnorm_bf16(x1, p['norm2_g'], min(tm, 256))
    f = st['f_conv'].shape[-1] // 2
    act, tail_a, tail_b = ffn_up_act(h2, ws['w_up'], layer, w['ffn_conv_w'], _hist_rows(_pad_halves(st['f_conv'], f)),
                                     nb, nq, nqp, tm, 512)
    x2 = matmul(act, ws['w_down'], layer, tm, 512, res=x1, name="down_proj")
    rows_blk = min(tm, nqp)
    r0 = (nq - (FFN_CONV - 1)) % rows_blk - ((nq - 1) % rows_blk) // HALO * HALO
    f_conv_new = jnp.concatenate([tail_a[:, r0:r0 + FFN_CONV - 1, :f], tail_b[:, r0:r0 + FFN_CONV - 1, :f]], axis=-1)

    new_rows = rows.reshape(nb, nqp, 4, NSA_G, DH)[:, :nq]
    states = (new_rows, win_state.reshape(nb, WINDOW, 2, NSA_G, DH), m_c, m_n.reshape(nb, ML_H, ML_DK),
              m_m.reshape(nb, ML_H), g_conv_new, g_s, f_conv_new)
    return x2, states


def kernel(x_prompt, x_sample, cache_nsa_kv, page_table, cache_nsa_win, state_mlstm_c, state_mlstm_n,
           state_mlstm_m, state_gdn_conv, state_gdn_s, state_ffn_conv, norm1_g, w_in, q_norm_g, k_norm_g,
           w_cmp1, cmp_pe, w_cmp2, mlstm_b_i, mlstm_b_f, mlstm_norm_g, gdn_conv_w, gdn_a_log, gdn_dt_bias,
           gdn_norm_g, w_branch, w_out, norm2_g, w_up, ffn_conv_w, w_down):
    depth = w_in.shape[0]
    bp, sp, d = x_prompt.shape
    bs, ss, _ = x_sample.shape
    n_pages = page_table.shape[1]
    past = n_pages * PAGE
    ssp = -(-ss // 16) * 16
    assert sp % KC == 0 and ss >= GDN_CONV - 1 and cache_nsa_win.shape[2] == WINDOW and n_pages % (KC // PAGE) == 0
    grp_p = dict(nb=bp, nq=sp, nqp=sp, pos0=0, rope=rope_tables(0, sp))
    grp_s = dict(nb=bs, nq=ss, nqp=ssp, pos0=past, rope=rope_tables(past, ssp))
    cache = cache_nsa_kv.reshape(depth, cache_nsa_kv.shape[1], PAGE_SUB, LANE)
    xp = x_prompt.reshape(bp * sp, d)
    xs = jnp.pad(x_sample, ((0, 0), (0, ssp - ss), (0, 0))).reshape(bs * ssp, d)
    ws = _stacked_weights(w_in, w_up, w_down, w_branch, w_out)
    st_p, st_s = [], []
    for l in range(depth):
        p = dict(norm1_g=norm1_g[l], q_norm_g=q_norm_g[l], k_norm_g=k_norm_g[l],
                 w_cmp1=w_cmp1[l], cmp_pe=cmp_pe[l], w_cmp2=w_cmp2[l], mlstm_b_i=mlstm_b_i[l],
                 mlstm_b_f=mlstm_b_f[l], mlstm_norm_g=mlstm_norm_g[l], gdn_conv_w=gdn_conv_w[l],
                 gdn_a_log=gdn_a_log[l], gdn_dt_bias=gdn_dt_bias[l], gdn_norm_g=gdn_norm_g[l],
                 norm2_g=norm2_g[l], ffn_conv_w=ffn_conv_w[l])
        w = _layer_weights(p)
        zero_p = dict(win=jnp.zeros((bp, WINDOW, 512), F32),
                      m_c=jnp.zeros((bp, ML_H, ML_DK, ML_DV), F32), m_n=jnp.zeros((bp, ML_H, 1, ML_DK), F32),
                      m_m=jnp.zeros((bp, ML_H, 1, 1), F32),
                      g_conv=jnp.zeros((bp, GDN_CONV - 1, 3 * GD_H * GD_D), F32),
                      g_s=jnp.zeros((bp, GD_H, GD_D, GD_D), F32),
                      f_conv=jnp.zeros((bp, FFN_CONV - 1, state_ffn_conv.shape[-1]), F32))
        xp, s_p = _group_layer(xp, grp_p, zero_p, p, w, ws, l)
        st = dict(cache=cache, table=page_table, win=cache_nsa_win[l].reshape(bs, WINDOW, 512),
                  m_c=state_mlstm_c[l], m_n=state_mlstm_n[l].reshape(bs, ML_H, 1, ML_DK),
                  m_m=state_mlstm_m[l].reshape(bs, ML_H, 1, 1), g_conv=state_gdn_conv[l],
                  g_s=state_gdn_s[l], f_conv=state_ffn_conv[l])
        xs, s_s = _group_layer(xs, grp_s, st, p, w, ws, l)
        st_p.append(s_p)
        st_s.append(s_s)
    outs = [xp.reshape(bp, sp, d), xs.reshape(bs, ssp, d)[:, :ss]]
    for i in range(8):
        outs.append(jnp.stack([s[i] for s in st_p], axis=0))
        outs.append(jnp.stack([s[i] for s in st_s], axis=0))
    return tuple(outs)
```

```python
import functools
import math

import numpy as np
import jax
import jax.numpy as jnp
from jax import lax
from jax.experimental import pallas as pl
from jax.experimental.pallas import tpu as pltpu

F32 = jnp.float32
BF16 = jnp.bfloat16
HI = lax.Precision.HIGHEST

EPS = 1e-6
NEG_INF = -1e30
LANE = 128
SUBLANE = 8
PAGE = 128
DH = 128
NSA_HEADS = 8
NSA_G = 2
NSA_GROUP = NSA_HEADS // NSA_G
ROPE_DIM = DH // 4
ROPE_THETA = 500000.0
CMP_BLOCK = 32
CMP_STRIDE = 16
SLC_BLOCK = 64
N_SELECT = 16
WINDOW = 512
ML_H, ML_DK, ML_DV = 4, 128, 256
GD_H, GD_D = 8, 128
GDN_CONV = 4
FFN_CONV = 3
CHUNK = 64
FFN_PAD = 5632
KC = 1024
VMEM_LIMIT = 56 * 1024 * 1024

Z_Q, Z_MV, Z_MO, Z_GZ = 0, 1024, 2048, 3072
Z_GQKV, Z_BR = 4096, 7168
Z_MQ, Z_KV, Z_MK, Z_SM = 13312, 13824, 15360, 15872
Z_COLS = 16384
KV_W = 1536
ROW_W = 4 * NSA_G * DH
ROW_SUB = ROW_W // LANE
PAGE_SUB = PAGE * ROW_SUB
SM_NG, SM_MI, SM_MF, SM_GA, SM_GB = 0, 24, 28, 32, 40


def _cp(sem):
    return pltpu.CompilerParams(dimension_semantics=sem, vmem_limit_bytes=VMEM_LIMIT)


def _iota(shape, dim):
    return lax.broadcasted_iota(jnp.int32, shape, dim)


def _dot(a, b, **kw):
    return jnp.dot(a, b, preferred_element_type=F32, **kw)


def _dot_nt(a, b, **kw):
    return lax.dot_general(a, b, (((1,), (1,)), ((), ())), preferred_element_type=F32, **kw)


def _silu(x):
    return x * jax.nn.sigmoid(x)


def _split(x):
    hi = x.astype(BF16)
    return hi, (x - hi.astype(F32)).astype(BF16)


def _dot3(a, b):
    ah, al = _split(a)
    bh, bl = _split(b)
    return _dot(ah, bh) + (_dot(ah, bl) + _dot(al, bh))


def _rmsnorm_body(x_ref, g_ref, o_ref):
    x = x_ref[...]
    y = x * lax.rsqrt(jnp.mean(x * x, axis=-1, keepdims=True) + EPS)
    o_ref[...] = (y * g_ref[...]).astype(o_ref.dtype)


def rmsnorm_bf16(x, g, tm):
    t, d = x.shape
    return pl.pallas_call(
        _rmsnorm_body, grid=(t // tm,),
        in_specs=[pl.BlockSpec((tm, d), lambda i: (i, 0)), pl.BlockSpec((1, d), lambda i: (0, 0))],
        out_specs=pl.BlockSpec((tm, d), lambda i: (i, 0)),
        out_shape=jax.ShapeDtypeStruct((t, d), BF16),
        compiler_params=_cp(("parallel",)), name="rmsnorm")(x, g.reshape(1, d))


def _norm_mm_body(x_ref, g_ref, w_ref, o_ref, h_sc):
    @pl.when(pl.program_id(1) == 0)
    def _():
        x = x_ref[...]
        y = x * lax.rsqrt(jnp.mean(x * x, axis=-1, keepdims=True) + EPS)
        h_sc[...] = (y * g_ref[...]).astype(h_sc.dtype)

    o_ref[...] = _dot(h_sc[...], w_ref[...])


def norm_matmul(x, g, w, layer, tm, tn, name):
    m, k = x.shape
    n = w.shape[2]
    return pl.pallas_call(
        _norm_mm_body, grid=(m // tm, n // tn),
        in_specs=[pl.BlockSpec((tm, k), lambda i, j: (i, 0)), pl.BlockSpec((1, k), lambda i, j: (0, 0)),
                  pl.BlockSpec((None, k, tn), lambda i, j: (layer, 0, j))],
        out_specs=pl.BlockSpec((tm, tn), lambda i, j: (i, j)),
        out_shape=jax.ShapeDtypeStruct((m, n), F32),
        scratch_shapes=[pltpu.VMEM((tm, k), BF16)],
        compiler_params=_cp(("parallel", "arbitrary")), name=name)(x, g.reshape(1, k), w)


def _mm_res_body(a_ref, w_ref, r_ref, o_ref):
    o_ref[...] = r_ref[...] + _dot(a_ref[...], w_ref[...])


def matmul_residual(a, w, layer, tm, tn, res, name):
    m, k = a.shape
    n = w.shape[2]
    tile = pl.BlockSpec((tm, tn), lambda i, j: (i, j))
    return pl.pallas_call(
        _mm_res_body, grid=(m // tm, n // tn),
        in_specs=[pl.BlockSpec((tm, k), lambda i, j: (i, 0)),
                  pl.BlockSpec((None, k, tn), lambda i, j: (layer, 0, j)), tile],
        out_specs=tile, out_shape=jax.ShapeDtypeStruct((m, n), F32),
        compiler_params=_cp(("parallel", "parallel")), name=name)(a, w, res)


def _nsa_prep_body(zq_ref, zkv_ref, c_ref, s1_ref, s2_ref, qg_ref, kg_ref, q_out, rows_out, kvs_out, win_out):
    c = c_ref[...]
    s1 = s1_ref[...]
    s2 = s2_ref[...]
    tb = zq_ref.shape[0]

    def norm_rope(x, g):
        y = x * lax.rsqrt(jnp.mean(x * x, axis=-1, keepdims=True) + EPS) * g
        return y * c + pltpu.roll(y, LANE - ROPE_DIM // 2, 1) * s1 + pltpu.roll(y, ROPE_DIM // 2, 1) * s2

    for h in range(NSA_HEADS):
        sl = slice(h * DH, (h + 1) * DH)
        q_out[:, sl] = (norm_rope(zq_ref[:, sl], qg_ref[...]) * (DH ** -0.5)).astype(q_out.dtype)
    for br in range(3):
        for g in range(NSA_G):
            col = (br * NSA_G + g) * DH
            kk = norm_rope(zkv_ref[:, col:col + DH], kg_ref[br:br + 1, :])
            vv = zkv_ref[:, KV_W // 2 + col:KV_W // 2 + col + DH]
            if br < 2:
                rows_out[pl.ds(4 * br + g, tb, stride=ROW_SUB), :] = kk
                rows_out[pl.ds(4 * br + 2 + g, tb, stride=ROW_SUB), :] = vv
            else:
                win_out[:, g * DH:(g + 1) * DH] = kk
                win_out[:, 2 * DH + g * DH:2 * DH + (g + 1) * DH] = vv
            if br == 1:
                kvs_out[:, g * DH:(g + 1) * DH] = kk.astype(kvs_out.dtype)
                kvs_out[:, (2 + g) * DH:(3 + g) * DH] = vv.astype(kvs_out.dtype)


def nsa_prep(z, tabs, q_g, k_g, nqp, tb):
    t = z.shape[0]
    nblk = nqp // tb
    tab_spec = pl.BlockSpec((tb, LANE), lambda i: (i % nblk, 0))
    return pl.pallas_call(
        _nsa_prep_body, grid=(t // tb,),
        in_specs=[pl.BlockSpec((tb, 1024), lambda i: (i, Z_Q // 1024)),
                  pl.BlockSpec((tb, KV_W), lambda i: (i, Z_KV // KV_W)), tab_spec, tab_spec, tab_spec,
                  pl.BlockSpec((1, DH), lambda i: (0, 0)), pl.BlockSpec((3, DH), lambda i: (0, 0))],
        out_specs=[pl.BlockSpec((tb, 1024), lambda i: (i, 0)), pl.BlockSpec((tb * ROW_SUB, LANE), lambda i: (i, 0)),
                   pl.BlockSpec((tb, 512), lambda i: (i, 0)), pl.BlockSpec((tb, 512), lambda i: (i, 0))],
        out_shape=[jax.ShapeDtypeStruct((t, 1024), BF16), jax.ShapeDtypeStruct((t * ROW_SUB, LANE), F32),
                   jax.ShapeDtypeStruct((t, 512), BF16), jax.ShapeDtypeStruct((t, 512), F32)],
        compiler_params=_cp(("parallel",)), name="nsa_prep")(z, z, *tabs, q_g.reshape(1, DH), k_g)


def rope_tables(pos0, nqp):
    half = ROPE_DIM // 2
    pos = pos0 + jnp.arange(nqp, dtype=jnp.int32)
    inv = ROPE_THETA ** (-jnp.arange(half, dtype=F32) / half)
    ang = pos.astype(F32)[:, None] * inv[None, :]
    cos, sin = jnp.cos(ang), jnp.sin(ang)
    zeros = jnp.zeros((nqp, LANE - ROPE_DIM), F32)
    c = jnp.concatenate([cos, cos, jnp.ones((nqp, LANE - ROPE_DIM), F32)], axis=1)
    s1 = jnp.concatenate([-sin, jnp.zeros((nqp, half), F32), zeros], axis=1)
    s2 = jnp.concatenate([jnp.zeros((nqp, half), F32), sin, zeros], axis=1)
    return c, s1, s2


def _compress_body(*refs, pps, paged, new_step):
    if paged:
        refs = refs[1:]
    slots = refs[:pps]
    k = pps
    new_ref = refs[k] if paged else None
    k += 1 if paged else 0
    w_refs = refs[k:k + 2]
    o_ref = refs[k + 2]
    kvs_ref = refs[k + 3] if paged else None
    i = pl.program_id(1)
    nb = PAGE // CMP_STRIDE

    def rows(idx, start, size, stride):
        v = slots[idx][pl.ds(start, size, stride=stride), :]
        if paged and idx == 0:
            v = jnp.where(i == new_step, new_ref[pl.ds(start, size, stride=stride), :], v)
        return v

    for kind in range(2):
        xs = []
        for g in range(NSA_G):
            part = kind * 2 + g
            per_slot = [jnp.concatenate([rows(s, ROW_SUB * j + part, nb, CMP_STRIDE * ROW_SUB)
                                         for j in range(CMP_STRIDE)], axis=1) for s in range(pps)]
            xs.append(jnp.concatenate(per_slot, axis=0) if pps > 1 else per_slot[0])
        x = jnp.concatenate(xs, axis=0).astype(BF16)
        fs = _dot(x, w_refs[kind][...])
        for g in range(NSA_G):
            o_ref[:, (kind * 2 + g) * 2 * DH:(kind * 2 + g + 1) * 2 * DH] = fs[g * nb * pps:(g + 1) * nb * pps]
    if paged:
        for s in range(pps):
            for q in range(4):
                kvs_ref[s * PAGE:(s + 1) * PAGE, q * DH:(q + 1) * DH] = rows(s, 4 + q, PAGE, ROW_SUB).astype(kvs_ref.dtype)


def compress_pages(src, wcat, nb_batch, steps, pps, table=None, layer=None, newpage=None):
    nb = PAGE // CMP_STRIDE
    w_spec = pl.BlockSpec((CMP_STRIDE * DH, 2 * DH), lambda b, i, *_: (0, 0))
    out_spec = pl.BlockSpec((None, nb * pps, 1024), lambda b, i, *_: (b, i, 0))
    out_shape = jax.ShapeDtypeStruct((nb_batch, steps * pps * nb, 1024), F32)
    if table is None:
        body = functools.partial(_compress_body, pps=pps, paged=False, new_step=None)
        slot_specs = [pl.BlockSpec((None, PAGE_SUB, LANE), functools.partial(
            lambda b, i, s: ((b * steps + i) * pps + s, 0, 0), s=s)) for s in range(pps)]
        return pl.pallas_call(
            body, grid=(nb_batch, steps), in_specs=slot_specs + [w_spec, w_spec], out_specs=out_spec,
            out_shape=out_shape, compiler_params=_cp(("parallel", "parallel")),
            name="compress")(*([src] * pps), wcat[0], wcat[1])
    n_pages = table.shape[1]
    body = functools.partial(_compress_body, pps=pps, paged=True, new_step=n_pages // pps)
    slot_specs = [pl.BlockSpec((None, None, PAGE_SUB, LANE), functools.partial(
        lambda b, i, tbl, s: (layer, tbl[b, jnp.minimum(i * pps + s, n_pages - 1)], 0, 0), s=s)) for s in range(pps)]
    new_spec = pl.BlockSpec((None, PAGE_SUB, LANE), lambda b, i, tbl: (b, 0, 0))
    kvs_spec = pl.BlockSpec((None, pps * PAGE, 512), lambda b, i, tbl: (b, i, 0))
    gs = pltpu.PrefetchScalarGridSpec(num_scalar_prefetch=1, grid=(nb_batch, steps),
                                      in_specs=slot_specs + [new_spec, w_spec, w_spec],
                                      out_specs=[out_spec, kvs_spec])
    return pl.pallas_call(body, grid_spec=gs,
                          out_shape=[out_shape, jax.ShapeDtypeStruct((nb_batch, steps * pps * PAGE, 512), BF16)],
                          compiler_params=_cp(("parallel", "parallel")),
                          name="compress_paged")(table, *([src] * pps), newpage, wcat[0], wcat[1])


def _cfin_body(fs_ref, pe_ref, w1_ref, w2_ref, o_ref):
    ncp = fs_ref.shape[0]
    last = _iota((ncp, 1), 0) == ncp - 1
    for kind in range(2):
        bias = _dot(pe_ref[kind], w1_ref[kind])[0:1, :]
        for g in range(NSA_G):
            base = (kind * 2 + g) * 2 * DH
            second_next = jnp.where(last, 0.0, pltpu.roll(fs_ref[:, base + DH:base + 2 * DH], ncp - 1, 0))
            hid = fs_ref[:, base:base + DH] + second_next + bias
            o_ref[:, (kind * 2 + g) * DH:(kind * 2 + g + 1) * DH] = _dot(
                _silu(hid).astype(BF16), w2_ref[kind]).astype(o_ref.dtype)


def compress_finish(fs, pe8, w1flat, w2):
    b, ncp, _ = fs.shape
    return pl.pallas_call(
        _cfin_body, grid=(b,),
        in_specs=[pl.BlockSpec((None, ncp, 1024), lambda i: (i, 0, 0)), pl.BlockSpec(pe8.shape, lambda i: (0, 0, 0)),
                  pl.BlockSpec(w1flat.shape, lambda i: (0, 0, 0)), pl.BlockSpec(w2.shape, lambda i: (0, 0, 0))],
        out_specs=pl.BlockSpec((None, ncp, 512), lambda i: (i, 0, 0)),
        out_shape=jax.ShapeDtypeStruct((b, ncp, 512), BF16),
        compiler_params=_cp(("parallel",)), name="compress_finish")(fs, pe8, w1flat, w2)


def _stack_heads(q_ref, g):
    return jnp.concatenate([q_ref[:, (g * NSA_GROUP + h) * DH:(g * NSA_GROUP + h + 1) * DH]
                            for h in range(NSA_GROUP)], axis=0)


def _unstack_heads(o_ref, g, o, qb):
    for h in range(NSA_GROUP):
        o_ref[:, (g * NSA_GROUP + h) * DH:(g * NSA_GROUP + h + 1) * DH] = o[h * qb:(h + 1) * qb]


def _rank(score, column, blk, ns):
    rank = jnp.zeros(score.shape, F32)
    for j in range(ns):
        sj = column(j)
        rank = rank + jnp.where(sj > score, 1.0, 0.0) + jnp.where(sj == score, jnp.where(blk > j, 1.0, 0.0), 0.0)
    return rank


def _cmp_body(q_ref, kv_ref, m_ref, o_ref, sel_ref, *, qb, pos0, ncp, ns, nsp, nsel, packed):
    i = pl.program_id(1)
    t1 = pos0 + i * qb + _iota((qb, 1), 0)
    vis = (_iota((1, ncp), 1) * CMP_STRIDE + (CMP_BLOCK - 1)) <= t1
    bias = jnp.concatenate([jnp.where(vis, 0.0, NEG_INF)] * NSA_GROUP, axis=0)
    any_vis = jnp.concatenate([jnp.where(t1 >= CMP_BLOCK - 1, 1.0, 0.0)] * NSA_GROUP, axis=0)
    pcs = []
    for g in range(NSA_G):
        s = _dot_nt(_stack_heads(q_ref, g), kv_ref[:, g * DH:(g + 1) * DH]) + bias
        e = jnp.exp(s - jnp.max(s, axis=-1, keepdims=True))
        pc = e * (any_vis / jnp.sum(e, axis=-1, keepdims=True))
        _unstack_heads(o_ref, g, _dot(pc.astype(BF16), kv_ref[:, (2 + g) * DH:(3 + g) * DH]), qb)
        pcs.append((pc[0:qb] + pc[qb:2 * qb]) + (pc[2 * qb:3 * qb] + pc[3 * qb:4 * qb]))
    cur = t1 // SLC_BLOCK

    def scores(imp, blk):
        valid = (blk * SLC_BLOCK <= t1) & (blk < ns)
        forced = (blk == 0) | (blk == cur) | (blk == cur - 1)
        return jnp.where(valid, jnp.where(forced, jnp.inf, imp), -jnp.inf)

    if packed:
        half = LANE // 2
        lane = _iota((1, LANE), 1)
        low = lane < half
        blk = lane & (half - 1)
        score = scores(_dot(pcs[0], m_ref[0], precision=HI) + _dot(pcs[1], m_ref[1], precision=HI), blk)
        st = score.T
        row8 = _iota((SUBLANE, 1), 0)
        tiles = []
        for hf in range(2):
            for v in range(half // SUBLANE):
                if SUBLANE * v >= ns:
                    tiles.append(jnp.full((SUBLANE, qb), -1.0, F32))
                    continue
                tile = st[hf * half + SUBLANE * v:hf * half + SUBLANE * (v + 1), :]
                acc = jnp.zeros((SUBLANE, qb), F32)
                for j in range(ns):
                    sj = st[hf * half + j:hf * half + j + 1, :]
                    ge = jnp.where(sj >= tile, 1.0, 0.0)
                    gt = jnp.where(sj > tile, 1.0, 0.0)
                    if SUBLANE * v > j:
                        acc = acc + ge
                    elif SUBLANE * v + SUBLANE - 1 < j:
                        acc = acc + gt
                    else:
                        acc = acc + jnp.where(row8 + SUBLANE * v > j, ge, gt)
                tiles.append(jnp.where(acc < nsel, 0.0, -1.0))
        selm = jnp.concatenate(tiles, axis=0).T
        sel_ref[:, 0:LANE] = jnp.where(low, selm, -1.0).astype(sel_ref.dtype)
        sel_ref[:, LANE:2 * LANE] = jnp.where(low, pltpu.roll(selm, half, 1), -1.0).astype(sel_ref.dtype)
    else:
        blk = _iota((1, nsp), 1)
        for g in range(NSA_G):
            score = scores(_dot(pcs[g], m_ref[0], precision=HI), blk)
            rank = _rank(score, lambda j: score[:, j:j + 1], blk, ns)
            sel_ref[:, g * nsp:(g + 1) * nsp] = jnp.where(rank < nsel, 0.0, -1.0).astype(sel_ref.dtype)


def cmp_select(q, kvc, nb_batch, nqp, qb, pos0, ns, nsp, nsel):
    ncp = kvc.shape[1]
    nblk = nqp // qb
    packed = 2 * ns <= LANE
    mmat = importance_matrix(ncp, nsp, packed)
    body = functools.partial(_cmp_body, qb=qb, pos0=pos0, ncp=ncp, ns=ns, nsp=nsp, nsel=nsel, packed=packed)
    return pl.pallas_call(
        body, grid=(nb_batch, nblk),
        in_specs=[pl.BlockSpec((qb, 1024), lambda b, i: (b * nblk + i, 0)),
                  pl.BlockSpec((None, ncp, 512), lambda b, i: (b, 0, 0)),
                  pl.BlockSpec(mmat.shape, lambda b, i: (0, 0, 0))],
        out_specs=[pl.BlockSpec((qb, 1024), lambda b, i: (b * nblk + i, 0)),
                   pl.BlockSpec((qb, NSA_G * nsp), lambda b, i: (b * nblk + i, 0))],
        out_shape=[jax.ShapeDtypeStruct((nb_batch * nqp, 1024), F32),
                   jax.ShapeDtypeStruct((nb_batch * nqp, NSA_G * nsp), BF16)],
        compiler_params=_cp(("parallel", "parallel")), name="cmp_select")(q, kvc, mmat)


def importance_matrix(ncp, nsp, packed):
    m = np.zeros((2 if packed else 1, ncp, nsp), np.float32)
    for j in range(nsp // 2 if packed else nsp):
        for n, w in ((4 * j - 1, 1.0), (4 * j, 2.0), (4 * j + 1, 2.0), (4 * j + 2, 2.0), (4 * j + 3, 1.0)):
            if 0 <= n < ncp:
                m[0, n, j] = w
                if packed:
                    m[1, n, nsp // 2 + j] = w
    return jnp.asarray(m)


MASK_BIG = 2.0 ** 100


def expand_matrices(kc):
    blocks = kc // SLC_BLOCK
    e = np.zeros((LANE // blocks, LANE, kc), np.float32)
    for w in range(LANE // blocks):
        for key in range(kc):
            e[w, blocks * w + key // SLC_BLOCK, key] = MASK_BIG
    return jnp.asarray(e, BF16)


def _slc_body(q_ref, sel0_ref, sel1_ref, e_ref, kv_ref, o_ref, m_sc, l_sc, acc_sc, *, qb, pos0, kc):
    i = pl.program_id(1)
    c = pl.program_id(2)

    @pl.when(c == 0)
    def _():
        m_sc[...] = jnp.full(m_sc.shape, NEG_INF, F32)
        l_sc[...] = jnp.zeros(l_sc.shape, F32)
        acc_sc[...] = jnp.zeros(acc_sc.shape, F32)

    def step(diagonal):
        gs = range(NSA_G)
        sels = (sel0_ref, sel1_ref)
        ew = e_ref[c % e_ref.shape[0]]
        bias = [_dot(sels[g][...], ew) for g in gs]
        if diagonal:
            t1 = pos0 + i * qb + _iota((qb, 1), 0)
            causal = jnp.where(c * kc + _iota((1, kc), 1) <= t1, 0.0, NEG_INF)
            bias = [b + causal for b in bias]
        s = [_dot_nt(_stack_heads(q_ref, g), kv_ref[:, g * DH:(g + 1) * DH])
             + jnp.concatenate([bias[g]] * NSA_GROUP, axis=0) for g in gs]
        m_old = [m_sc[g] for g in gs]
        m_new = [jnp.maximum(m_old[g], jnp.max(s[g], axis=-1, keepdims=True)) for g in gs]
        p = [jnp.exp(s[g] - m_new[g]) for g in gs]
        alpha = [jnp.exp(m_old[g] - m_new[g]) for g in gs]
        pv = [_dot(p[g].astype(BF16), kv_ref[:, (2 + g) * DH:(3 + g) * DH]) for g in gs]
        for g in gs:
            l_sc[g] = alpha[g] * l_sc[g] + jnp.sum(p[g], axis=-1, keepdims=True)
            acc_sc[g] = alpha[g] * acc_sc[g] + pv[g]
            m_sc[g] = m_new[g]

    first_partial = (pos0 + i * qb) // kc
    last_needed = (pos0 + i * qb + qb - 1) // kc
    pl.when(c < first_partial)(functools.partial(step, False))
    pl.when((c >= first_partial) & (c <= last_needed))(functools.partial(step, True))

    @pl.when(c == pl.num_programs(2) - 1)
    def _():
        for g in range(NSA_G):
            _unstack_heads(o_ref, g, acc_sc[g] / l_sc[g], qb)


def slc_attention(q, sel, kvs, nb_batch, nqp, qb, pos0, nsp, kc):
    nblk = nqp // qb
    lanes_g = nsp // LANE
    chunks_per_seq = kvs.shape[0] // (nb_batch * kc)
    nchunk = (pos0 + nqp - 1) // kc + 1
    last = lambda i: (pos0 + i * qb + qb - 1) // kc
    emat = expand_matrices(kc)
    wpl = emat.shape[0]
    q_map = lambda b, i, c: (b * nblk + i, 0)
    sel_specs = [pl.BlockSpec((qb, LANE), functools.partial(
        lambda b, i, c, g: (b * nblk + i, g * lanes_g + jnp.minimum(c, last(i)) // wpl), g=g))
        for g in range(NSA_G)]
    return pl.pallas_call(
        functools.partial(_slc_body, qb=qb, pos0=pos0, kc=kc), grid=(nb_batch, nblk, nchunk),
        in_specs=[pl.BlockSpec((qb, 1024), q_map)] + sel_specs
                 + [pl.BlockSpec(emat.shape, lambda b, i, c: (0, 0, 0)),
                    pl.BlockSpec((kc, 512), lambda b, i, c: (b * chunks_per_seq + jnp.minimum(c, last(i)), 0))],
        out_specs=pl.BlockSpec((qb, 1024), q_map),
        out_shape=jax.ShapeDtypeStruct((nb_batch * nqp, 1024), F32),
        scratch_shapes=[pltpu.VMEM((NSA_G, NSA_GROUP * qb, 1), F32), pltpu.VMEM((NSA_G, NSA_GROUP * qb, 1), F32),
                        pltpu.VMEM((NSA_G, NSA_GROUP * qb, DH), F32)],
        compiler_params=_cp(("parallel", "parallel", "arbitrary")), name="slc_attention")(q, sel, sel, emat, kvs)


def _win_body(q_ref, w_ref, oc_ref, os_ref, zs_ref, o_ref, *, qb, pos0, span):
    i = pl.program_id(1)
    kv = w_ref[pl.ds(pl.multiple_of(i * qb, qb), span), :].astype(BF16)
    qpos = pos0 + i * qb + _iota((qb, 1), 0)
    kpos = pos0 - WINDOW + i * qb + _iota((1, span), 1)
    rel = qpos - kpos
    mask = (rel >= 0) & (rel <= WINDOW) & (kpos >= 0)
    bias = jnp.concatenate([jnp.where(mask, 0.0, NEG_INF)] * NSA_GROUP, axis=0)
    gate = jax.nn.sigmoid(zs_ref[...])
    for g in range(NSA_G):
        s = _dot_nt(_stack_heads(q_ref, g), kv[:, g * DH:(g + 1) * DH]) + bias
        e = jnp.exp(s - jnp.max(s, axis=-1, keepdims=True))
        p = e * (1.0 / jnp.sum(e, axis=-1, keepdims=True))
        ow = _dot(p.astype(BF16), kv[:, (2 + g) * DH:(3 + g) * DH])
        for hh in range(NSA_GROUP):
            h = g * NSA_GROUP + hh
            sl = slice(h * DH, (h + 1) * DH)
            c0 = SM_NG + 3 * h
            o = (gate[:, c0:c0 + 1] * oc_ref[:, sl] + gate[:, c0 + 1:c0 + 2] * os_ref[:, sl]
                 + gate[:, c0 + 2:c0 + 3] * ow[hh * qb:(hh + 1) * qb])
            o_ref[:, sl] = o.astype(o_ref.dtype)


def win_combine(q, win_all, o_cmp, o_slc, z, nb_batch, nqp, qb, pos0):
    nblk = nqp // qb
    wtot = win_all.shape[1]
    body = functools.partial(_win_body, qb=qb, pos0=pos0, span=WINDOW + qb)
    tok = pl.BlockSpec((qb, 1024), lambda b, i: (b * nblk + i, 0))
    return pl.pallas_call(
        body, grid=(nb_batch, nblk),
        in_specs=[tok, pl.BlockSpec((None, wtot, 512), lambda b, i: (b, 0, 0)), tok, tok,
                  pl.BlockSpec((qb, LANE), lambda b, i: (b * nblk + i, Z_SM // LANE))],
        out_specs=tok, out_shape=jax.ShapeDtypeStruct((nb_batch * nqp, 1024), BF16),
        compiler_params=_cp(("parallel", "parallel")), name="win_combine")(q, win_all, o_cmp, o_slc, z)


def _pad_rows(x, n):
    if x.shape[0] == n:
        return x
    return jnp.concatenate([x, jnp.zeros((n - x.shape[0], x.shape[1]), x.dtype)], axis=0)


def _cumsum_rows(x):
    n = x.shape[0]
    tri = jnp.where(_iota((n, n), 0) >= _iota((n, n), 1), 1.0, 0.0).astype(BF16)
    hi, lo = _split(x)
    lo2 = (x - hi.astype(F32) - lo.astype(F32)).astype(BF16)
    return _dot(tri, hi) + (_dot(tri, lo) + _dot(tri, lo2))


def _log_sigmoid(x):
    return jnp.minimum(x, 0.0) - jnp.log1p(jnp.exp(-jnp.abs(x)))


def _softplus(x):
    return jnp.maximum(x, 0.0) + jnp.log1p(jnp.exp(-jnp.abs(x)))


def _mlstm_body(q_ref, k_ref, v_ref, og_ref, zs_ref, b_ref, ng_ref, c0_ref, n0_ref, m0_ref,
                o_ref, c_out, n_out, m_out, c_sc, n_sc, m_sc, *, nvalid):
    ck = pl.program_id(1)
    n = CHUNK
    lin = q_ref.shape[0]

    @pl.when(ck == 0)
    def _():
        c_sc[...] = c0_ref[...]
        n_sc[...] = n0_ref[...]
        m_sc[...] = m0_ref[...]

    gates = _pad_rows(zs_ref[...], n) + b_ref[...]
    lane = _iota((1, LANE), 1)
    is_f = (lane >= SM_MF) & (lane < SM_MF + ML_H)
    lf_all = jnp.where(is_f, _log_sigmoid(gates), 0.0)
    ig_all = gates
    if nvalid < n:
        live = _iota((n, 1), 0) < nvalid
        lf_all = jnp.where(live, lf_all, 0.0)
        ig_all = jnp.where(live, gates, NEG_INF)
    comb = jnp.where(is_f, _cumsum_rows(lf_all), ig_all)
    comb_t = comb.T
    incl = _iota((n, n), 0) >= _iota((n, n), 1)
    hs = range(ML_H)
    sk = [slice(h * ML_DK, (h + 1) * ML_DK) for h in hs]
    sv = [slice(h * ML_DV, (h + 1) * ML_DV) for h in hs]
    c0 = [c_sc[h] for h in hs]
    n0 = [n_sc[h] for h in hs]
    m0 = [m_sc[h] for h in hs]
    q = [_pad_rows(q_ref[:, sk[h]], n) for h in hs]
    k = [_pad_rows(k_ref[:, sk[h]], n) * (ML_DK ** -0.5) for h in hs]
    q16 = [q[h].astype(BF16) for h in hs]
    v16 = [_pad_rows(v_ref[:, sv[h]], n).astype(BF16) for h in hs]
    ig = [comb[:, SM_MI + h:SM_MI + h + 1] for h in hs]
    bc = [comb[:, SM_MF + h:SM_MF + h + 1] for h in hs]
    dlog = [jnp.where(incl, bc[h] + (comb_t[SM_MI + h:SM_MI + h + 1, :] - comb_t[SM_MF + h:SM_MF + h + 1, :]),
                      -jnp.inf) for h in hs]
    qk = [_dot_nt(q16[h], k[h].astype(BF16)) for h in hs]
    qc = [_dot(q16[h], c0[h].astype(BF16)) for h in hs]
    inter = [bc[h] + m0[h] for h in hs]
    m_t = [jnp.maximum(inter[h], jnp.max(dlog[h], axis=1, keepdims=True)) for h in hs]
    a = [qk[h] * jnp.exp(dlog[h] - m_t[h]) for h in hs]
    w_inter = [jnp.exp(inter[h] - m_t[h]) for h in hs]
    num = [_dot(a[h].astype(BF16), v16[h]) + w_inter[h] * qc[h] for h in hs]
    den = [jnp.sum(a[h], axis=1, keepdims=True) + w_inter[h] * jnp.sum(q[h] * n0[h], axis=1, keepdims=True)
           for h in hs]
    hh = [num[h] / jnp.maximum(jnp.abs(den[h]), jnp.exp(-m_t[h])) for h in hs]
    m_new = [m_t[h][n - 1:n] for h in hs]
    kw = [k[h] * jnp.exp(bc[h][n - 1:n] - bc[h] + ig[h] - m_new[h]) for h in hs]
    decay = [jnp.exp(bc[h][n - 1:n] + m0[h] - m_new[h]) for h in hs]
    c_new = [decay[h] * c0[h] + _dot(kw[h].T.astype(BF16), v16[h]) for h in hs]
    n_new = [decay[h] * n0[h] + jnp.sum(kw[h], axis=0, keepdims=True) for h in hs]
    for h in hs:
        c_sc[h] = c_new[h]
        n_sc[h] = n_new[h]
        m_sc[h] = m_new[h]
        c_out[h] = c_new[h]
        n_out[h] = n_new[h]
        m_out[h] = m_new[h]
        hn = hh[h] * lax.rsqrt(jnp.mean(hh[h] * hh[h], axis=-1, keepdims=True) + EPS) * ng_ref[:, sv[h]]
        o = jax.nn.sigmoid(_pad_rows(og_ref[:, sv[h]], n)) * hn
        o_ref[:, sv[h]] = o[:lin].astype(o_ref.dtype)


def mlstm_scan(z, bias_row, norm_g, c0, n0, m0, nb_batch, nqp, nvalid):
    lin = min(CHUNK, nqp)
    nc = nqp // lin
    tok = lambda w, off: pl.BlockSpec((lin, w), lambda b, c: (b * nc + c, off // w))
    body = functools.partial(_mlstm_body, nvalid=min(nvalid, CHUNK))
    st = lambda shape: pl.BlockSpec((None, ML_H) + shape, lambda b, c: (b, 0, 0, 0))
    wq, wv = ML_H * ML_DK, ML_H * ML_DV
    return pl.pallas_call(
        body, grid=(nb_batch, nc),
        in_specs=[tok(wq, Z_MQ), tok(wq, Z_MK), tok(wv, Z_MV), tok(wv, Z_MO), tok(LANE, Z_SM),
                  pl.BlockSpec((1, LANE), lambda b, c: (0, 0)), pl.BlockSpec((1, wv), lambda b, c: (0, 0)),
                  st((ML_DK, ML_DV)), st((1, ML_DK)), st((1, 1))],
        out_specs=[pl.BlockSpec((lin, wv), lambda b, c: (b * nc + c, 0)),
                   st((ML_DK, ML_DV)), st((1, ML_DK)), st((1, 1))],
        out_shape=[jax.ShapeDtypeStruct((nb_batch * nqp, wv), BF16),
                   jax.ShapeDtypeStruct((nb_batch, ML_H, ML_DK, ML_DV), F32),
                   jax.ShapeDtypeStruct((nb_batch, ML_H, 1, ML_DK), F32),
                   jax.ShapeDtypeStruct((nb_batch, ML_H, 1, 1), F32)],
        scratch_shapes=[pltpu.VMEM((ML_H, ML_DK, ML_DV), F32), pltpu.VMEM((ML_H, 1, ML_DK), F32),
                        pltpu.VMEM((ML_H, 1, 1), F32)],
        compiler_params=_cp(("parallel", "arbitrary")),
        name="mlstm_scan")(z, z, z, z, z, bias_row, norm_g.reshape(1, wv), c0, n0, m0)


def _gdn_body(q_ref, k_ref, v_ref, gz_ref, zs_ref, b_ref, a_ref, ng_ref, s0_ref,
              o_ref, s_out, s_sc, *, nvalid):
    ck = pl.program_id(1)
    n = CHUNK
    lin = q_ref.shape[0]

    @pl.when(ck == 0)
    def _():
        s_sc[...] = s0_ref[...]

    zs = _pad_rows(zs_ref[...], n)
    lane = _iota((1, LANE), 1)
    is_a = (lane >= SM_GA) & (lane < SM_GA + GD_H)
    g_all = jnp.where(is_a, a_ref[...] * _softplus(zs + b_ref[...]), 0.0)
    beta_all = jax.nn.sigmoid(zs)
    if nvalid < n:
        live = _iota((n, 1), 0) < nvalid
        g_all = jnp.where(live, g_all, 0.0)
        beta_all = jnp.where(live, beta_all, 0.0)
    gc_all = _cumsum_rows(g_all)
    gc_t = gc_all.T
    incl = _iota((n, n), 0) >= _iota((n, n), 1)
    strict = _iota((n, n), 0) > _iota((n, n), 1)
    eye = jnp.where(_iota((n, n), 0) == _iota((n, n), 1), 1.0, 0.0)
    hs = range(GD_H)
    sl = [slice(h * GD_D, (h + 1) * GD_D) for h in hs]
    s0 = [s_sc[h] for h in hs]
    q16 = [_pad_rows(q_ref[:, sl[h]], n).astype(BF16) for h in hs]
    k = [_pad_rows(k_ref[:, sl[h]], n) for h in hs]
    k16 = [k[h].astype(BF16) for h in hs]
    gc = [gc_all[:, SM_GA + h:SM_GA + h + 1] for h in hs]
    beta = [beta_all[:, SM_GB + h:SM_GB + h + 1] for h in hs]
    dec = [jnp.where(incl, jnp.exp(jnp.where(incl, gc[h] - gc_t[SM_GA + h:SM_GA + h + 1, :], 0.0)), 0.0) for h in hs]
    egc = [jnp.exp(gc[h]) for h in hs]
    kb = [k[h] * beta[h] for h in hs]
    xp = [-jnp.where(strict, _dot_nt(kb[h].astype(BF16), k16[h]) * dec[h], 0.0) for h in hs]
    tinv = [eye + xp[h] for h in hs]
    for _ in range(int(math.log2(n)) - 1):
        xp = [_dot3(xp[h], xp[h]) for h in hs]
        tinv = [tinv[h] + _dot3(tinv[h], xp[h]) for h in hs]
    rhs = [jnp.concatenate([_pad_rows(v_ref[:, sl[h]], n) * beta[h], kb[h] * egc[h]], axis=1) for h in hs]
    sol = [_dot3(tinv[h], rhs[h]) for h in hs]
    s016 = [s0[h].astype(BF16) for h in hs]
    u16 = [(sol[h][:, :GD_D] - _dot(sol[h][:, GD_D:].astype(BF16), s016[h])).astype(BF16) for h in hs]
    qk = [(_dot_nt(q16[h], k16[h]) * dec[h]).astype(BF16) for h in hs]
    o = [_dot(qk[h], u16[h]) + egc[h] * _dot(q16[h], s016[h]) for h in hs]
    kw = [(k[h] * jnp.exp(gc[h][n - 1:n] - gc[h])).T.astype(BF16) for h in hs]
    s_new = [jnp.exp(gc[h][n - 1:n]) * s0[h] + _dot(kw[h], u16[h]) for h in hs]
    for h in hs:
        s_sc[h] = s_new[h]
        s_out[h] = s_new[h]
        on = o[h] * lax.rsqrt(jnp.mean(o[h] * o[h], axis=-1, keepdims=True) + EPS) * ng_ref[...]
        o_ref[:, sl[h]] = (on * _silu(_pad_rows(gz_ref[:, sl[h]], n)))[:lin].astype(o_ref.dtype)


def gdn_scan(qkv, z, dt_row, nega_row, norm_g, s0, nb_batch, nqp, nvalid):
    lin = min(CHUNK, nqp)
    nc = nqp // lin
    w = GD_H * GD_D
    body = functools.partial(_gdn_body, nvalid=min(nvalid, CHUNK))
    tokq = lambda part: pl.BlockSpec((lin, w), lambda b, c: (b * nc + c, part))
    st = pl.BlockSpec((None, GD_H, GD_D, GD_D), lambda b, c: (b, 0, 0, 0))
    row = pl.BlockSpec((1, LANE), lambda b, c: (0, 0))
    return pl.pallas_call(
        body, grid=(nb_batch, nc),
        in_specs=[tokq(0), tokq(1), tokq(2),
                  pl.BlockSpec((lin, w), lambda b, c: (b * nc + c, Z_GZ // w)),
                  pl.BlockSpec((lin, LANE), lambda b, c: (b * nc + c, Z_SM // LANE)),
                  row, row, row, st],
        out_specs=[pl.BlockSpec((lin, w), lambda b, c: (b * nc + c, 0)), st],
        out_shape=[jax.ShapeDtypeStruct((nb_batch * nqp, w), BF16),
                   jax.ShapeDtypeStruct((nb_batch, GD_H, GD_D, GD_D), F32)],
        scratch_shapes=[pltpu.VMEM((GD_H, GD_D, GD_D), F32)],
        compiler_params=_cp(("parallel", "arbitrary")),
        name="gdn_scan")(qkv, qkv, qkv, z, z, dt_row, nega_row, norm_g.reshape(1, GD_D), s0)


HALO = 8
FFN_SUBTILES = 8


def _conv_apply(buf, x, w_ref, width):
    tb = x.shape[0]
    buf[HALO:HALO + tb, :] = x
    y = w_ref[width - 1:width, :] * x
    for kk in range(1, width):
        y = y + w_ref[width - 1 - kk:width - kk, :] * buf[HALO - kk:HALO - kk + tb, :]
    return y


def _conv_rows(buf, x, hist_ref, w_ref, width, first):
    @pl.when(first)
    def _():
        buf[0:HALO, :] = hist_ref[...]

    return _conv_apply(buf, x, w_ref, width)


def _gdn_prep_body(z_ref, hist_ref, w_ref, o_ref, buf):
    ct = pl.program_id(1)
    tb = z_ref.shape[0]
    x = z_ref[...]
    y = _silu(_conv_rows(buf, x, hist_ref, w_ref, GDN_CONV, pl.program_id(2) == 0))
    buf[0:HALO, :] = x[tb - HALO:tb, :]
    part = ct // (GD_H * GD_D // y.shape[1])
    qscale = jnp.where(part == 0, GD_D ** -0.5, 1.0)
    for hh in range(y.shape[1] // GD_D):
        sl = slice(hh * GD_D, (hh + 1) * GD_D)
        yh = y[:, sl]
        nrm = yh * lax.rsqrt(jnp.sum(yh * yh, axis=-1, keepdims=True) + EPS) * qscale
        o_ref[:, sl] = jnp.where(part == 2, yh, nrm)


def gdn_prep(z, hist, w, nb_batch, nqp, tb, tc=512):
    nblk = nqp // tb
    ncol = 3 * GD_H * GD_D
    return pl.pallas_call(
        _gdn_prep_body, grid=(nb_batch, ncol // tc, nblk),
        in_specs=[pl.BlockSpec((tb, tc), lambda b, c, i: (b * nblk + i, Z_GQKV // tc + c)),
                  pl.BlockSpec((None, HALO, tc), lambda b, c, i: (b, 0, c)),
                  pl.BlockSpec((GDN_CONV, tc), lambda b, c, i: (0, c))],
        out_specs=pl.BlockSpec((tb, tc), lambda b, c, i: (b * nblk + i, c)),
        out_shape=jax.ShapeDtypeStruct((nb_batch * nqp, ncol), F32),
        scratch_shapes=[pltpu.VMEM((HALO + tb, tc), F32)],
        compiler_params=_cp(("parallel", "parallel", "arbitrary")), name="gdn_prep")(z, hist, w)


def _ffn_up_body(h_ref, wa_ref, wb_ref, ca_ref, cb_ref, ha_ref, hb_ref, act_ref, ta_ref, tb_ref, bufa, bufb,
                 *, bps, nseq, tail0):
    i = pl.program_id(1)
    tm = h_ref.shape[0]
    halves = ((wa_ref, ca_ref, ha_ref, ta_ref, bufa), (wb_ref, cb_ref, hb_ref, tb_ref, bufb))
    if nseq == 1:
        nsub = FFN_SUBTILES if tm % (FFN_SUBTILES * HALO) == 0 else 1
        rs = tm // nsub

        @pl.when((i % bps) == 0)
        def _():
            for _, _, hist_ref, _, buf in halves:
                buf[0:HALO, :] = hist_ref[0]

        def conv_sub(buf, c_ref, off):
            y = c_ref[FFN_CONV - 1:FFN_CONV, :] * buf[HALO + off:HALO + off + rs, :]
            for kk in range(1, FFN_CONV):
                y = y + c_ref[FFN_CONV - 1 - kk:FFN_CONV - kk, :] * buf[HALO + off - kk:HALO + off - kk + rs, :]
            return y

        for r in range(nsub + 1):
            if r < nsub:
                hr = h_ref[r * rs:(r + 1) * rs, :]
                for w_ref, _, _, _, buf in halves:
                    buf[HALO + r * rs:HALO + (r + 1) * rs, :] = _dot(hr, w_ref[...])
            if r > 0:
                off = (r - 1) * rs
                act_ref[off:off + rs, :] = (_silu(conv_sub(bufa, ca_ref, off))
                                            * conv_sub(bufb, cb_ref, off)).astype(act_ref.dtype)
        for _, _, _, t_ref, buf in halves:
            t_ref[0] = buf[HALO + tail0:HALO + tail0 + HALO, :]
            buf[0:HALO, :] = buf[tm:tm + HALO, :]
    else:
        rows = tm // nseq
        h = h_ref[...]
        ys = []
        for w_ref, c_ref, hist_ref, t_ref, buf in halves:
            u = _dot(h, w_ref[...])
            parts = []
            for s in range(nseq):
                buf[0:HALO, :] = hist_ref[s]
                parts.append(_conv_apply(buf, u[s * rows:(s + 1) * rows], c_ref, FFN_CONV))
                t_ref[s] = u[s * rows + tail0:s * rows + tail0 + HALO, :]
            ys.append(jnp.concatenate(parts, axis=0))
        act_ref[...] = (_silu(ys[0]) * ys[1]).astype(act_ref.dtype)


def ffn_up_act(h, w_up, layer, conv_w, hist, nb_batch, nq, nqp, tm, tn):
    t, d = h.shape
    nj = FFN_PAD // tn
    bps = max(1, nqp // tm)
    nseq = max(1, tm // nqp)
    rows = tm // nseq
    assert (nq - 1) // rows == bps - 1 and (nq - 1) % HALO >= FFN_CONV - 2
    tail0 = ((nq - 1) % rows) // HALO * HALO
    body = functools.partial(_ffn_up_body, bps=bps, nseq=nseq, tail0=tail0)
    col = lambda half: (lambda j, i: (0, half * nj + j))
    seq = lambda half: (lambda j, i: (i // bps, 0, half * nj + j))
    tail = jax.ShapeDtypeStruct((nb_batch, HALO, FFN_PAD), F32)
    return pl.pallas_call(
        body, grid=(nj, t // tm),
        in_specs=[pl.BlockSpec((tm, d), lambda j, i: (i, 0)),
                  pl.BlockSpec((None, d, tn), lambda j, i: (layer, 0, j)),
                  pl.BlockSpec((None, d, tn), lambda j, i: (layer, 0, nj + j)),
                  pl.BlockSpec((FFN_CONV, tn), col(0)), pl.BlockSpec((FFN_CONV, tn), col(1)),
                  pl.BlockSpec((nseq, HALO, tn), seq(0)), pl.BlockSpec((nseq, HALO, tn), seq(1))],
        out_specs=[pl.BlockSpec((tm, tn), lambda j, i: (i, j)),
                   pl.BlockSpec((nseq, HALO, tn), lambda j, i: (i // bps, 0, j)),
                   pl.BlockSpec((nseq, HALO, tn), lambda j, i: (i // bps, 0, j))],
        out_shape=[jax.ShapeDtypeStruct((t, FFN_PAD), BF16), tail, tail],
        scratch_shapes=[pltpu.VMEM((HALO + tm, tn), F32), pltpu.VMEM((HALO + tm, tn), F32)],
        compiler_params=_cp(("parallel", "arbitrary")),
        name="ffn_up_act")(h, w_up, w_up, conv_w, conv_w, hist, hist)


def _merge_body(oa_ref, ob_ref, oc_ref, w_ref, g0_ref, g1_ref, g2_ref, o_ref):
    acc = jax.nn.sigmoid(g0_ref[...]) * _dot(oa_ref[...], w_ref[0])
    acc = acc + jax.nn.sigmoid(g1_ref[...]) * _dot(ob_ref[...], w_ref[1])
    acc = acc + jax.nn.sigmoid(g2_ref[...]) * _dot(oc_ref[...], w_ref[2])
    o_ref[...] = acc.astype(o_ref.dtype)


def merge_branches(oa, ob, oc, wb, layer, z, tm, tn):
    t = oa.shape[0]
    d = wb.shape[3]
    tok = pl.BlockSpec((tm, 1024), lambda i, j: (i, 0))
    gate = lambda nbr: pl.BlockSpec((tm, tn), lambda i, j: (i, (Z_BR + nbr * d) // tn + j))
    return pl.pallas_call(
        _merge_body, grid=(t // tm, d // tn),
        in_specs=[tok, tok, tok, pl.BlockSpec((None, 3, 1024, tn), lambda i, j: (layer, 0, 0, j)),
                  gate(0), gate(1), gate(2)],
        out_specs=pl.BlockSpec((tm, tn), lambda i, j: (i, j)),
        out_shape=jax.ShapeDtypeStruct((t, d), BF16),
        compiler_params=_cp(("parallel", "parallel")), name="merge_branches")(oa, ob, oc, wb, z, z, z)


def _pad_halves(a, f):
    pad = ((0, 0),) * (a.ndim - 1) + ((0, FFN_PAD - f),)
    return jnp.concatenate([jnp.pad(a[..., :f], pad), jnp.pad(a[..., f:], pad)], axis=-1)


def _stacked_weights(w_in, w_up, w_down, w_branch, w_out):
    offs = np.cumsum([0, 1024, 1536, 24, 512, 512, 1024, 4, 4, 1024, 3072, 8, 8, 1024, 6144])
    lead = w_in.shape[:2]
    seg = lambda i: w_in[:, :, offs[i]:offs[i + 1]]
    kv = seg(1).reshape(lead + (3, 2, NSA_G * DH))
    small = jnp.concatenate([seg(2), seg(6), seg(7), seg(10), seg(11)], axis=-1)
    small = jnp.pad(small, ((0, 0), (0, 0), (0, Z_COLS - Z_SM - small.shape[-1])))
    gap = jnp.zeros(lead + (Z_KV - Z_MQ - ML_H * ML_DK,), w_in.dtype)
    w_all = jnp.concatenate([seg(0), seg(5), seg(8), seg(12), seg(9), seg(13), seg(3), gap,
                             kv[:, :, :, 0].reshape(lead + (KV_W // 2,)), kv[:, :, :, 1].reshape(lead + (KV_W // 2,)),
                             seg(4), small], axis=-1)
    assert w_all.shape[-1] == Z_COLS
    f = w_down.shape[1]
    return dict(w_all=w_all.astype(BF16), w_up=_pad_halves(w_up, f).astype(BF16),
                w_down=jnp.pad(w_down, ((0, 0), (0, FFN_PAD - f), (0, 0))).astype(BF16),
                w_branch=w_branch.astype(BF16), w_out=w_out.astype(BF16))


def _layer_weights(p):
    w1 = p['w_cmp1']
    wcat = jnp.concatenate([w1[:, :CMP_STRIDE].reshape(2, CMP_STRIDE * DH, DH),
                            w1[:, CMP_STRIDE:].reshape(2, CMP_STRIDE * DH, DH)], axis=2).astype(BF16)

    def lane_row(pairs):
        row = jnp.zeros((LANE,), F32)
        for off, v in pairs:
            row = row.at[off:off + v.shape[0]].set(v.astype(F32))
        return row.reshape(1, LANE)

    return dict(
        ffn_conv_w=_pad_halves(p['ffn_conv_w'], p['ffn_conv_w'].shape[-1] // 2), wcat=wcat,
        pe8=jnp.pad(p['cmp_pe'].reshape(2, 1, CMP_BLOCK * DH), ((0, 0), (0, 7), (0, 0))).astype(BF16),
        w1flat=w1.reshape(2, CMP_BLOCK * DH, DH).astype(BF16), w2=p['w_cmp2'].astype(BF16),
        ml_bias=lane_row([(SM_MI, p['mlstm_b_i']), (SM_MF, p['mlstm_b_f'])]),
        gd_dt=lane_row([(SM_GA, p['gdn_dt_bias'])]),
        gd_nega=lane_row([(SM_GA, -jnp.exp(p['gdn_a_log'].astype(F32)))]))


def _hist_rows(state):
    return jnp.pad(state, ((0, 0), (HALO - state.shape[1], 0), (0, 0)))


def _group_layer(x, grp, st, p, w, ws, layer):
    nb, nq, nqp, pos0 = grp['nb'], grp['nq'], grp['nqp'], grp['pos0']
    t = nb * nqp
    big = nqp >= 512
    tm = 1024 if big else t
    z = norm_matmul(x, p['norm1_g'], ws['w_all'], layer, tm, 1024 if big else 2048, name="in_proj")

    tb = 256 if big else nqp
    q_rot, rows, kvs, win_new = nsa_prep(z, grp['rope'], p['q_norm_g'], p['k_norm_g'], nqp, tb)
    pps = KC // PAGE
    if pos0 == 0:
        nblk_c = nqp // CMP_STRIDE
        ncp = nblk_c
        fs = compress_pages(rows.reshape(nb * nqp // PAGE, PAGE_SUB, LANE), w['wcat'], nb, nqp // KC, pps)
    else:
        n_pages = st['table'].shape[1]
        nblk_c = (n_pages + 1) * (PAGE // CMP_STRIDE)
        ncp = -(-nblk_c // LANE) * LANE
        live = (jnp.arange(nqp * ROW_SUB) < nq * ROW_SUB)[None, :, None]
        newpage = jnp.pad(jnp.where(live, rows.reshape(nb, nqp * ROW_SUB, LANE), 0.0),
                          ((0, 0), (0, PAGE_SUB - nqp * ROW_SUB), (0, 0)))
        fs, kvs = compress_pages(st['cache'], w['wcat'], nb, ncp * CMP_STRIDE // KC, pps, table=st['table'],
                                 layer=layer, newpage=newpage)
        kvs = kvs.reshape(-1, 512)
    kvc = compress_finish(fs, w['pe8'], w['w1flat'], w['w2'])
    ns = nblk_c * CMP_STRIDE // SLC_BLOCK
    nsp = -(-ns // LANE) * LANE
    t_all = pos0 + nq
    nsel = min(N_SELECT, -(-t_all // SLC_BLOCK))
    qb = 128 if big else nqp
    o_cmp, sel = cmp_select(q_rot, kvc, nb, nqp, qb, pos0, ns, nsp, nsel)
    o_slc = slc_attention(q_rot, sel, kvs, nb, nqp, qb, pos0, nsp, KC if big else 2 * KC)
    win3 = win_new.reshape(nb, nqp, 512)
    win_all = jnp.concatenate([st['win'], win3], axis=1)
    o_nsa = win_combine(q_rot, win_all, o_cmp, o_slc, z, nb, nqp, qb, pos0)
    win_state = jnp.concatenate([st['win'], win3[:, :nq]], axis=1)[:, -WINDOW:]

    o_ml, m_c, m_n, m_m = mlstm_scan(z, w['ml_bias'], p['mlstm_norm_g'], st['m_c'], st['m_n'], st['m_m'],
                                     nb, nqp, nq)

    tbc = 512 if big else nqp
    qkv = gdn_prep(z, _hist_rows(st['g_conv']), p['gdn_conv_w'], nb, nqp, tbc)
    o_gd, g_s = gdn_scan(qkv, z, w['gd_dt'], w['gd_nega'], p['gdn_norm_g'], st['g_s'], nb, nqp, nq)
    z3 = z.reshape(nb, nqp, Z_COLS)
    g_conv_new = z3[:, nq - (GDN_CONV - 1):nq, Z_GQKV:Z_GQKV + 3 * GD_H * GD_D]

    mrg = merge_branches(o_nsa, o_ml, o_gd, ws['w_branch'], layer, z, tm, 512)
    x1 = matmul_residual(mrg, ws['w_out'], layer, tm, 1024 if big else 2048, x, name="out_proj")

    h2 = rmsnorm_bf16(x1, p['norm2_g'], min(tm, 256))
    f = st['f_conv'].shape[-1] // 2
    act, tail_a, tail_b = ffn_up_act(h2, ws['w_up'], layer, w['ffn_conv_w'], _hist_rows(_pad_halves(st['f_conv'], f)),
                                     nb, nq, nqp, tm, 512)
    x2 = matmul_residual(act, ws['w_down'], layer, tm, 512, x1, name="down_proj")
    rows_blk = min(tm, nqp)
    r0 = (nq - (FFN_CONV - 1)) % rows_blk - ((nq - 1) % rows_blk) // HALO * HALO
    f_conv_new = jnp.concatenate([tail_a[:, r0:r0 + FFN_CONV - 1, :f], tail_b[:, r0:r0 + FFN_CONV - 1, :f]], axis=-1)

    new_rows = rows.reshape(nb, nqp, 4, NSA_G, DH)[:, :nq]
    states = (new_rows, win_state.reshape(nb, WINDOW, 2, NSA_G, DH), m_c, m_n.reshape(nb, ML_H, ML_DK),
              m_m.reshape(nb, ML_H), g_conv_new, g_s, f_conv_new)
    return x2, states


def kernel(x_prompt, x_sample, cache_nsa_kv, page_table, cache_nsa_win, state_mlstm_c, state_mlstm_n,
           state_mlstm_m, state_gdn_conv, state_gdn_s, state_ffn_conv, norm1_g, w_in, q_norm_g, k_norm_g,
           w_cmp1, cmp_pe, w_cmp2, mlstm_b_i, mlstm_b_f, mlstm_norm_g, gdn_conv_w, gdn_a_log, gdn_dt_bias,
           gdn_norm_g, w_branch, w_out, norm2_g, w_up, ffn_conv_w, w_down):
    depth = w_in.shape[0]
    bp, sp, d = x_prompt.shape
    bs, ss, _ = x_sample.shape
    n_pages = page_table.shape[1]
    past = n_pages * PAGE
    ssp = -(-ss // 16) * 16
    assert sp % KC == 0 and ss >= GDN_CONV - 1 and cache_nsa_win.shape[2] == WINDOW and n_pages % (KC // PAGE) == 0
    grp_p = dict(nb=bp, nq=sp, nqp=sp, pos0=0, rope=rope_tables(0, sp))
    grp_s = dict(nb=bs, nq=ss, nqp=ssp, pos0=past, rope=rope_tables(past, ssp))
    cache = cache_nsa_kv.reshape(depth, cache_nsa_kv.shape[1], PAGE_SUB, LANE)
    xp = x_prompt.reshape(bp * sp, d)
    xs = jnp.pad(x_sample, ((0, 0), (0, ssp - ss), (0, 0))).reshape(bs * ssp, d)
    ws = _stacked_weights(w_in, w_up, w_down, w_branch, w_out)
    st_p, st_s = [], []
    for l in range(depth):
        p = dict(norm1_g=norm1_g[l], q_norm_g=q_norm_g[l], k_norm_g=k_norm_g[l],
                 w_cmp1=w_cmp1[l], cmp_pe=cmp_pe[l], w_cmp2=w_cmp2[l], mlstm_b_i=mlstm_b_i[l],
                 mlstm_b_f=mlstm_b_f[l], mlstm_norm_g=mlstm_norm_g[l], gdn_conv_w=gdn_conv_w[l],
                 gdn_a_log=gdn_a_log[l], gdn_dt_bias=gdn_dt_bias[l], gdn_norm_g=gdn_norm_g[l],
                 norm2_g=norm2_g[l], ffn_conv_w=ffn_conv_w[l])
        w = _layer_weights(p)
        zero_p = dict(win=jnp.zeros((bp, WINDOW, 512), F32),
                      m_c=jnp.zeros((bp, ML_H, ML_DK, ML_DV), F32), m_n=jnp.zeros((bp, ML_H, 1, ML_DK), F32),
                      m_m=jnp.zeros((bp, ML_H, 1, 1), F32),
                      g_conv=jnp.zeros((bp, GDN_CONV - 1, 3 * GD_H * GD_D), F32),
                      g_s=jnp.zeros((bp, GD_H, GD_D, GD_D), F32),
                      f_conv=jnp.zeros((bp, FFN_CONV - 1, state_ffn_conv.shape[-1]), F32))
        xp, s_p = _group_layer(xp, grp_p, zero_p, p, w, ws, l)
        st = dict(cache=cache, table=page_table, win=cache_nsa_win[l].reshape(bs, WINDOW, 512),
                  m_c=state_mlstm_c[l], m_n=state_mlstm_n[l].reshape(bs, ML_H, 1, ML_DK),
                  m_m=state_mlstm_m[l].reshape(bs, ML_H, 1, 1), g_conv=state_gdn_conv[l],
                  g_s=state_gdn_s[l], f_conv=state_ffn_conv[l])
        xs, s_s = _group_layer(xs, grp_s, st, p, w, ws, l)
        st_p.append(s_p)
        st_s.append(s_s)
    outs = [xp.reshape(bp, sp, d), xs.reshape(bs, ssp, d)[:, :ss]]
    for i in range(8):
        outs.append(jnp.stack([s[i] for s in st_p], axis=0))
        outs.append(jnp.stack([s[i] for s in st_s], axis=0))
    return tuple(outs)
```

```python
import functools
import math

import numpy as np
import jax
import jax.numpy as jnp
from jax import lax
from jax.experimental import pallas as pl
from jax.experimental.pallas import tpu as pltpu

F32 = jnp.float32
BF16 = jnp.bfloat16
HI = lax.Precision.HIGHEST

EPS = 1e-6
NEG_INF = -1e30
LANE = 128
SUBLANE = 8
PAGE = 128
DH = 128
NSA_HEADS = 8
NSA_G = 2
NSA_GROUP = NSA_HEADS // NSA_G
ROPE_DIM = DH // 4
ROPE_THETA = 500000.0
CMP_BLOCK = 32
CMP_STRIDE = 16
SLC_BLOCK = 64
N_SELECT = 16
WINDOW = 512
ML_H, ML_DK, ML_DV = 4, 128, 256
GD_H, GD_D = 8, 128
GDN_CONV = 4
FFN_CONV = 3
CHUNK = 64
FFN_PAD = 5632
KC = 1024
VMEM_LIMIT = 56 * 1024 * 1024

Z_Q, Z_MV, Z_MO, Z_GZ = 0, 1024, 2048, 3072
Z_GQKV, Z_BR = 4096, 7168
Z_MQ, Z_KV, Z_MK, Z_SM = 13312, 13824, 15360, 15872
Z_COLS = 16384
KV_W = 1536
ROW_W = 4 * NSA_G * DH
ROW_SUB = ROW_W // LANE
PAGE_SUB = PAGE * ROW_SUB
SM_NG, SM_MI, SM_MF, SM_GA, SM_GB = 0, 24, 28, 32, 40


def _cp(sem):
    return pltpu.CompilerParams(dimension_semantics=sem, vmem_limit_bytes=VMEM_LIMIT)


def _iota(shape, dim):
    return lax.broadcasted_iota(jnp.int32, shape, dim)


def _dot(a, b, **kw):
    return jnp.dot(a, b, preferred_element_type=F32, **kw)


def _dot_nt(a, b, **kw):
    return lax.dot_general(a, b, (((1,), (1,)), ((), ())), preferred_element_type=F32, **kw)


def _silu(x):
    return x * jax.nn.sigmoid(x)


def _split(x):
    hi = x.astype(BF16)
    return hi, (x - hi.astype(F32)).astype(BF16)


def _dot3(a, b):
    ah, al = _split(a)
    bh, bl = _split(b)
    return _dot(ah, bh) + (_dot(ah, bl) + _dot(al, bh))


def _norm_mm_body(x_ref, g_ref, w_ref, o_ref, h_sc):
    @pl.when(pl.program_id(1) == 0)
    def _():
        x = x_ref[...]
        y = x * lax.rsqrt(jnp.mean(x * x, axis=-1, keepdims=True) + EPS)
        h_sc[...] = (y * g_ref[...]).astype(h_sc.dtype)

    o_ref[...] = _dot(h_sc[...], w_ref[...])


def norm_matmul(x, g, w, layer, tm, tn, name):
    m, k = x.shape
    n = w.shape[2]
    return pl.pallas_call(
        _norm_mm_body, grid=(m // tm, n // tn),
        in_specs=[pl.BlockSpec((tm, k), lambda i, j: (i, 0)), pl.BlockSpec((1, k), lambda i, j: (0, 0)),
                  pl.BlockSpec((None, k, tn), lambda i, j: (layer, 0, j))],
        out_specs=pl.BlockSpec((tm, tn), lambda i, j: (i, j)),
        out_shape=jax.ShapeDtypeStruct((m, n), F32),
        scratch_shapes=[pltpu.VMEM((tm, k), BF16)],
        compiler_params=_cp(("parallel", "arbitrary")), name=name)(x, g.reshape(1, k), w)


def _mm_res_norm_body(a_ref, w_ref, r_ref, g_ref, o_ref, h_ref):
    x = r_ref[...] + _dot(a_ref[...], w_ref[...])
    o_ref[...] = x
    y = x * lax.rsqrt(jnp.mean(x * x, axis=-1, keepdims=True) + EPS)
    h_ref[...] = (y * g_ref[...]).astype(h_ref.dtype)


def matmul_residual_norm(a, w, layer, tm, res, g, name):
    m, k = a.shape
    n = w.shape[2]
    row = lambda width: pl.BlockSpec((tm, width), lambda i: (i, 0))
    return pl.pallas_call(
        _mm_res_norm_body, grid=(m // tm,),
        in_specs=[row(k), pl.BlockSpec((None, k, n), lambda i: (layer, 0, 0)), row(n),
                  pl.BlockSpec((1, n), lambda i: (0, 0))],
        out_specs=[row(n), row(n)],
        out_shape=[jax.ShapeDtypeStruct((m, n), F32), jax.ShapeDtypeStruct((m, n), BF16)],
        compiler_params=_cp(("parallel",)), name=name)(a, w, res, g.reshape(1, n))


def _mm_res_body(a_ref, w_ref, r_ref, o_ref):
    o_ref[...] = r_ref[...] + _dot(a_ref[...], w_ref[...])


def matmul_residual(a, w, layer, tm, tn, res, name):
    m, k = a.shape
    n = w.shape[2]
    tile = pl.BlockSpec((tm, tn), lambda i, j: (i, j))
    return pl.pallas_call(
        _mm_res_body, grid=(m // tm, n // tn),
        in_specs=[pl.BlockSpec((tm, k), lambda i, j: (i, 0)),
                  pl.BlockSpec((None, k, tn), lambda i, j: (layer, 0, j)), tile],
        out_specs=tile, out_shape=jax.ShapeDtypeStruct((m, n), F32),
        compiler_params=_cp(("parallel", "parallel")), name=name)(a, w, res)


def _nsa_prep_body(zq_ref, zkv_ref, c_ref, s1_ref, s2_ref, qg_ref, kg_ref, q_out, rows_out, kvs_out, win_out):
    c = c_ref[...]
    s1 = s1_ref[...]
    s2 = s2_ref[...]
    tb = zq_ref.shape[0]

    def norm_rope(x, g):
        y = x * lax.rsqrt(jnp.mean(x * x, axis=-1, keepdims=True) + EPS) * g
        return y * c + pltpu.roll(y, LANE - ROPE_DIM // 2, 1) * s1 + pltpu.roll(y, ROPE_DIM // 2, 1) * s2

    for h in range(NSA_HEADS):
        sl = slice(h * DH, (h + 1) * DH)
        q_out[:, sl] = (norm_rope(zq_ref[:, sl], qg_ref[...]) * (DH ** -0.5)).astype(q_out.dtype)
    for br in range(3):
        for g in range(NSA_G):
            col = (br * NSA_G + g) * DH
            kk = norm_rope(zkv_ref[:, col:col + DH], kg_ref[br:br + 1, :])
            vv = zkv_ref[:, KV_W // 2 + col:KV_W // 2 + col + DH]
            if br < 2:
                rows_out[pl.ds(4 * br + g, tb, stride=ROW_SUB), :] = kk
                rows_out[pl.ds(4 * br + 2 + g, tb, stride=ROW_SUB), :] = vv
            else:
                win_out[:, g * DH:(g + 1) * DH] = kk
                win_out[:, 2 * DH + g * DH:2 * DH + (g + 1) * DH] = vv
            if br == 1:
                kvs_out[:, g * DH:(g + 1) * DH] = kk.astype(kvs_out.dtype)
                kvs_out[:, (2 + g) * DH:(3 + g) * DH] = vv.astype(kvs_out.dtype)


def nsa_prep(z, tabs, q_g, k_g, nqp, tb):
    t = z.shape[0]
    nblk = nqp // tb
    tab_spec = pl.BlockSpec((tb, LANE), lambda i: (i % nblk, 0))
    return pl.pallas_call(
        _nsa_prep_body, grid=(t // tb,),
        in_specs=[pl.BlockSpec((tb, 1024), lambda i: (i, Z_Q // 1024)),
                  pl.BlockSpec((tb, KV_W), lambda i: (i, Z_KV // KV_W)), tab_spec, tab_spec, tab_spec,
                  pl.BlockSpec((1, DH), lambda i: (0, 0)), pl.BlockSpec((3, DH), lambda i: (0, 0))],
        out_specs=[pl.BlockSpec((tb, 1024), lambda i: (i, 0)), pl.BlockSpec((tb * ROW_SUB, LANE), lambda i: (i, 0)),
                   pl.BlockSpec((tb, 512), lambda i: (i, 0)), pl.BlockSpec((tb, 512), lambda i: (i, 0))],
        out_shape=[jax.ShapeDtypeStruct((t, 1024), BF16), jax.ShapeDtypeStruct((t * ROW_SUB, LANE), F32),
                   jax.ShapeDtypeStruct((t, 512), BF16), jax.ShapeDtypeStruct((t, 512), F32)],
        compiler_params=_cp(("parallel",)), name="nsa_prep")(z, z, *tabs, q_g.reshape(1, DH), k_g)


def rope_tables(pos0, nqp):
    half = ROPE_DIM // 2
    pos = pos0 + jnp.arange(nqp, dtype=jnp.int32)
    inv = ROPE_THETA ** (-jnp.arange(half, dtype=F32) / half)
    ang = pos.astype(F32)[:, None] * inv[None, :]
    cos, sin = jnp.cos(ang), jnp.sin(ang)
    zeros = jnp.zeros((nqp, LANE - ROPE_DIM), F32)
    c = jnp.concatenate([cos, cos, jnp.ones((nqp, LANE - ROPE_DIM), F32)], axis=1)
    s1 = jnp.concatenate([-sin, jnp.zeros((nqp, half), F32), zeros], axis=1)
    s2 = jnp.concatenate([jnp.zeros((nqp, half), F32), sin, zeros], axis=1)
    return c, s1, s2


def _compress_body(*refs, pps, paged, new_step):
    if paged:
        refs = refs[1:]
    slots = refs[:pps]
    k = pps
    new_ref = refs[k] if paged else None
    k += 1 if paged else 0
    w_refs = refs[k:k + 2]
    o_ref = refs[k + 2]
    kvs_ref = refs[k + 3] if paged else None
    i = pl.program_id(1)
    nb = PAGE // CMP_STRIDE

    def rows(idx, start, size, stride):
        v = slots[idx][pl.ds(start, size, stride=stride), :]
        if paged and idx == 0:
            v = jnp.where(i == new_step, new_ref[pl.ds(start, size, stride=stride), :], v)
        return v

    for kind in range(2):
        xs = []
        for g in range(NSA_G):
            part = kind * 2 + g
            per_slot = [jnp.concatenate([rows(s, ROW_SUB * j + part, nb, CMP_STRIDE * ROW_SUB)
                                         for j in range(CMP_STRIDE)], axis=1) for s in range(pps)]
            xs.append(jnp.concatenate(per_slot, axis=0) if pps > 1 else per_slot[0])
        x = jnp.concatenate(xs, axis=0).astype(BF16)
        fs = _dot(x, w_refs[kind][...])
        for g in range(NSA_G):
            o_ref[:, (kind * 2 + g) * 2 * DH:(kind * 2 + g + 1) * 2 * DH] = fs[g * nb * pps:(g + 1) * nb * pps]
    if paged:
        for s in range(pps):
            for q in range(4):
                kvs_ref[s * PAGE:(s + 1) * PAGE, q * DH:(q + 1) * DH] = rows(s, 4 + q, PAGE, ROW_SUB).astype(kvs_ref.dtype)


def compress_pages(src, wcat, nb_batch, steps, pps, table=None, layer=None, newpage=None):
    nb = PAGE // CMP_STRIDE
    w_spec = pl.BlockSpec((CMP_STRIDE * DH, 2 * DH), lambda b, i, *_: (0, 0))
    out_spec = pl.BlockSpec((None, nb * pps, 1024), lambda b, i, *_: (b, i, 0))
    out_shape = jax.ShapeDtypeStruct((nb_batch, steps * pps * nb, 1024), F32)
    if table is None:
        body = functools.partial(_compress_body, pps=pps, paged=False, new_step=None)
        slot_specs = [pl.BlockSpec((None, PAGE_SUB, LANE), functools.partial(
            lambda b, i, s: ((b * steps + i) * pps + s, 0, 0), s=s)) for s in range(pps)]
        return pl.pallas_call(
            body, grid=(nb_batch, steps), in_specs=slot_specs + [w_spec, w_spec], out_specs=out_spec,
            out_shape=out_shape, compiler_params=_cp(("parallel", "parallel")),
            name="compress")(*([src] * pps), wcat[0], wcat[1])
    n_pages = table.shape[1]
    body = functools.partial(_compress_body, pps=pps, paged=True, new_step=n_pages // pps)
    slot_specs = [pl.BlockSpec((None, None, PAGE_SUB, LANE), functools.partial(
        lambda b, i, tbl, s: (layer, tbl[b, jnp.minimum(i * pps + s, n_pages - 1)], 0, 0), s=s)) for s in range(pps)]
    new_spec = pl.BlockSpec((None, PAGE_SUB, LANE), lambda b, i, tbl: (b, 0, 0))
    kvs_spec = pl.BlockSpec((None, pps * PAGE, 512), lambda b, i, tbl: (b, i, 0))
    gs = pltpu.PrefetchScalarGridSpec(num_scalar_prefetch=1, grid=(nb_batch, steps),
                                      in_specs=slot_specs + [new_spec, w_spec, w_spec],
                                      out_specs=[out_spec, kvs_spec])
    return pl.pallas_call(body, grid_spec=gs,
                          out_shape=[out_shape, jax.ShapeDtypeStruct((nb_batch, steps * pps * PAGE, 512), BF16)],
                          compiler_params=_cp(("parallel", "parallel")),
                          name="compress_paged")(table, *([src] * pps), newpage, wcat[0], wcat[1])


def _cfin_body(fs_ref, pe_ref, w1_ref, w2_ref, o_ref):
    ncp = fs_ref.shape[0]
    last = _iota((ncp, 1), 0) == ncp - 1
    for kind in range(2):
        bias = _dot(pe_ref[kind], w1_ref[kind])[0:1, :]
        for g in range(NSA_G):
            base = (kind * 2 + g) * 2 * DH
            second_next = jnp.where(last, 0.0, pltpu.roll(fs_ref[:, base + DH:base + 2 * DH], ncp - 1, 0))
            hid = fs_ref[:, base:base + DH] + second_next + bias
            o_ref[:, (kind * 2 + g) * DH:(kind * 2 + g + 1) * DH] = _dot(
                _silu(hid).astype(BF16), w2_ref[kind]).astype(o_ref.dtype)


def compress_finish(fs, pe8, w1flat, w2):
    b, ncp, _ = fs.shape
    return pl.pallas_call(
        _cfin_body, grid=(b,),
        in_specs=[pl.BlockSpec((None, ncp, 1024), lambda i: (i, 0, 0)), pl.BlockSpec(pe8.shape, lambda i: (0, 0, 0)),
                  pl.BlockSpec(w1flat.shape, lambda i: (0, 0, 0)), pl.BlockSpec(w2.shape, lambda i: (0, 0, 0))],
        out_specs=pl.BlockSpec((None, ncp, 512), lambda i: (i, 0, 0)),
        out_shape=jax.ShapeDtypeStruct((b, ncp, 512), BF16),
        compiler_params=_cp(("parallel",)), name="compress_finish")(fs, pe8, w1flat, w2)


def _stack_heads(q_ref, g):
    return jnp.concatenate([q_ref[:, (g * NSA_GROUP + h) * DH:(g * NSA_GROUP + h + 1) * DH]
                            for h in range(NSA_GROUP)], axis=0)


def _unstack_heads(o_ref, g, o, qb):
    for h in range(NSA_GROUP):
        o_ref[:, (g * NSA_GROUP + h) * DH:(g * NSA_GROUP + h + 1) * DH] = o[h * qb:(h + 1) * qb]


def _rank(score, column, blk, ns):
    rank = jnp.zeros(score.shape, F32)
    for j in range(ns):
        sj = column(j)
        rank = rank + jnp.where(sj > score, 1.0, 0.0) + jnp.where(sj == score, jnp.where(blk > j, 1.0, 0.0), 0.0)
    return rank


def _cmp_body(q_ref, kv_ref, m_ref, o_ref, sel_ref, *, qb, pos0, ncp, ns, nsp, nsel, packed):
    i = pl.program_id(1)
    t1 = pos0 + i * qb + _iota((qb, 1), 0)
    vis = (_iota((1, ncp), 1) * CMP_STRIDE + (CMP_BLOCK - 1)) <= t1
    bias = jnp.concatenate([jnp.where(vis, 0.0, NEG_INF)] * NSA_GROUP, axis=0)
    any_vis = jnp.concatenate([jnp.where(t1 >= CMP_BLOCK - 1, 1.0, 0.0)] * NSA_GROUP, axis=0)
    pcs = []
    for g in range(NSA_G):
        s = _dot_nt(_stack_heads(q_ref, g), kv_ref[:, g * DH:(g + 1) * DH]) + bias
        e = jnp.exp(s - jnp.max(s, axis=-1, keepdims=True))
        pc = e * (any_vis / jnp.sum(e, axis=-1, keepdims=True))
        _unstack_heads(o_ref, g, _dot(pc.astype(BF16), kv_ref[:, (2 + g) * DH:(3 + g) * DH]), qb)
        pcs.append((pc[0:qb] + pc[qb:2 * qb]) + (pc[2 * qb:3 * qb] + pc[3 * qb:4 * qb]))
    cur = t1 // SLC_BLOCK

    def scores(imp, blk):
        valid = (blk * SLC_BLOCK <= t1) & (blk < ns)
        forced = (blk == 0) | (blk == cur) | (blk == cur - 1)
        return jnp.where(valid, jnp.where(forced, jnp.inf, imp), -jnp.inf)

    if packed:
        half = LANE // 2
        lane = _iota((1, LANE), 1)
        low = lane < half
        blk = lane & (half - 1)
        score = scores(_dot(pcs[0], m_ref[0], precision=HI) + _dot(pcs[1], m_ref[1], precision=HI), blk)
        st = score.T
        row8 = _iota((SUBLANE, 1), 0)
        tiles = []
        for hf in range(2):
            for v in range(half // SUBLANE):
                if SUBLANE * v >= ns:
                    tiles.append(jnp.full((SUBLANE, qb), -1.0, F32))
                    continue
                tile = st[hf * half + SUBLANE * v:hf * half + SUBLANE * (v + 1), :]
                acc = jnp.zeros((SUBLANE, qb), F32)
                for j in range(ns):
                    sj = st[hf * half + j:hf * half + j + 1, :]
                    ge = jnp.where(sj >= tile, 1.0, 0.0)
                    gt = jnp.where(sj > tile, 1.0, 0.0)
                    if SUBLANE * v > j:
                        acc = acc + ge
                    elif SUBLANE * v + SUBLANE - 1 < j:
                        acc = acc + gt
                    else:
                        acc = acc + jnp.where(row8 + SUBLANE * v > j, ge, gt)
                tiles.append(jnp.where(acc < nsel, 0.0, -1.0))
        selm = jnp.concatenate(tiles, axis=0).T
        sel_ref[:, 0:LANE] = jnp.where(low, selm, -1.0).astype(sel_ref.dtype)
        sel_ref[:, LANE:2 * LANE] = jnp.where(low, pltpu.roll(selm, half, 1), -1.0).astype(sel_ref.dtype)
    else:
        blk = _iota((1, nsp), 1)
        for g in range(NSA_G):
            score = scores(_dot(pcs[g], m_ref[0], precision=HI), blk)
            rank = _rank(score, lambda j: score[:, j:j + 1], blk, ns)
            sel_ref[:, g * nsp:(g + 1) * nsp] = jnp.where(rank < nsel, 0.0, -1.0).astype(sel_ref.dtype)


def cmp_select(q, kvc, nb_batch, nqp, qb, pos0, ns, nsp, nsel):
    ncp = kvc.shape[1]
    nblk = nqp // qb
    packed = 2 * ns <= LANE
    mmat = importance_matrix(ncp, nsp, packed)
    body = functools.partial(_cmp_body, qb=qb, pos0=pos0, ncp=ncp, ns=ns, nsp=nsp, nsel=nsel, packed=packed)
    return pl.pallas_call(
        body, grid=(nb_batch, nblk),
        in_specs=[pl.BlockSpec((qb, 1024), lambda b, i: (b * nblk + i, 0)),
                  pl.BlockSpec((None, ncp, 512), lambda b, i: (b, 0, 0)),
                  pl.BlockSpec(mmat.shape, lambda b, i: (0, 0, 0))],
        out_specs=[pl.BlockSpec((qb, 1024), lambda b, i: (b * nblk + i, 0)),
                   pl.BlockSpec((qb, NSA_G * nsp), lambda b, i: (b * nblk + i, 0))],
        out_shape=[jax.ShapeDtypeStruct((nb_batch * nqp, 1024), F32),
                   jax.ShapeDtypeStruct((nb_batch * nqp, NSA_G * nsp), BF16)],
        compiler_params=_cp(("parallel", "parallel")), name="cmp_select")(q, kvc, mmat)


def importance_matrix(ncp, nsp, packed):
    m = np.zeros((2 if packed else 1, ncp, nsp), np.float32)
    for j in range(nsp // 2 if packed else nsp):
        for n, w in ((4 * j - 1, 1.0), (4 * j, 2.0), (4 * j + 1, 2.0), (4 * j + 2, 2.0), (4 * j + 3, 1.0)):
            if 0 <= n < ncp:
                m[0, n, j] = w
                if packed:
                    m[1, n, nsp // 2 + j] = w
    return jnp.asarray(m)


MASK_BIG = 2.0 ** 100


def expand_matrices(kc):
    blocks = kc // SLC_BLOCK
    e = np.zeros((LANE // blocks, LANE, kc), np.float32)
    for w in range(LANE // blocks):
        for key in range(kc):
            e[w, blocks * w + key // SLC_BLOCK, key] = MASK_BIG
    return jnp.asarray(e, BF16)


def _slc_body(q_ref, sel0_ref, sel1_ref, e_ref, kv_ref, o_ref, m_sc, l_sc, acc_sc, *, qb, pos0, kc):
    i = pl.program_id(1)
    c = pl.program_id(2)

    @pl.when(c == 0)
    def _():
        m_sc[...] = jnp.full(m_sc.shape, NEG_INF, F32)
        l_sc[...] = jnp.zeros(l_sc.shape, F32)
        acc_sc[...] = jnp.zeros(acc_sc.shape, F32)

    def step(diagonal):
        gs = range(NSA_G)
        sels = (sel0_ref, sel1_ref)
        ew = e_ref[c % e_ref.shape[0]]
        bias = [_dot(sels[g][...], ew) for g in gs]
        if diagonal:
            t1 = pos0 + i * qb + _iota((qb, 1), 0)
            causal = jnp.where(c * kc + _iota((1, kc), 1) <= t1, 0.0, NEG_INF)
            bias = [b + causal for b in bias]
        s = [_dot_nt(_stack_heads(q_ref, g), kv_ref[:, g * DH:(g + 1) * DH])
             + jnp.concatenate([bias[g]] * NSA_GROUP, axis=0) for g in gs]
        m_old = [m_sc[g] for g in gs]
        m_new = [jnp.maximum(m_old[g], jnp.max(s[g], axis=-1, keepdims=True)) for g in gs]
        p = [jnp.exp(s[g] - m_new[g]) for g in gs]
        alpha = [jnp.exp(m_old[g] - m_new[g]) for g in gs]
        pv = [_dot(p[g].astype(BF16), kv_ref[:, (2 + g) * DH:(3 + g) * DH]) for g in gs]
        for g in gs:
            l_sc[g] = alpha[g] * l_sc[g] + jnp.sum(p[g], axis=-1, keepdims=True)
            acc_sc[g] = alpha[g] * acc_sc[g] + pv[g]
            m_sc[g] = m_new[g]

    first_partial = (pos0 + i * qb) // kc
    last_needed = (pos0 + i * qb + qb - 1) // kc
    pl.when(c < first_partial)(functools.partial(step, False))
    pl.when((c >= first_partial) & (c <= last_needed))(functools.partial(step, True))

    @pl.when(c == pl.num_programs(2) - 1)
    def _():
        for g in range(NSA_G):
            _unstack_heads(o_ref, g, acc_sc[g] / l_sc[g], qb)


def slc_attention(q, sel, kvs, nb_batch, nqp, qb, pos0, nsp, kc):
    nblk = nqp // qb
    lanes_g = nsp // LANE
    chunks_per_seq = kvs.shape[0] // (nb_batch * kc)
    nchunk = (pos0 + nqp - 1) // kc + 1
    last = lambda i: (pos0 + i * qb + qb - 1) // kc
    emat = expand_matrices(kc)
    wpl = emat.shape[0]
    q_map = lambda b, i, c: (b * nblk + i, 0)
    sel_specs = [pl.BlockSpec((qb, LANE), functools.partial(
        lambda b, i, c, g: (b * nblk + i, g * lanes_g + jnp.minimum(c, last(i)) // wpl), g=g))
        for g in range(NSA_G)]
    return pl.pallas_call(
        functools.partial(_slc_body, qb=qb, pos0=pos0, kc=kc), grid=(nb_batch, nblk, nchunk),
        in_specs=[pl.BlockSpec((qb, 1024), q_map)] + sel_specs
                 + [pl.BlockSpec(emat.shape, lambda b, i, c: (0, 0, 0)),
                    pl.BlockSpec((kc, 512), lambda b, i, c: (b * chunks_per_seq + jnp.minimum(c, last(i)), 0))],
        out_specs=pl.BlockSpec((qb, 1024), q_map),
        out_shape=jax.ShapeDtypeStruct((nb_batch * nqp, 1024), F32),
        scratch_shapes=[pltpu.VMEM((NSA_G, NSA_GROUP * qb, 1), F32), pltpu.VMEM((NSA_G, NSA_GROUP * qb, 1), F32),
                        pltpu.VMEM((NSA_G, NSA_GROUP * qb, DH), F32)],
        compiler_params=_cp(("parallel", "parallel", "arbitrary")), name="slc_attention")(q, sel, sel, emat, kvs)


def _win_body(q_ref, w_ref, oc_ref, os_ref, zs_ref, o_ref, *, qb, pos0, span):
    i = pl.program_id(1)
    kv = w_ref[pl.ds(pl.multiple_of(i * qb, qb), span), :].astype(BF16)
    qpos = pos0 + i * qb + _iota((qb, 1), 0)
    kpos = pos0 - WINDOW + i * qb + _iota((1, span), 1)
    rel = qpos - kpos
    mask = (rel >= 0) & (rel <= WINDOW) & (kpos >= 0)
    bias = jnp.concatenate([jnp.where(mask, 0.0, NEG_INF)] * NSA_GROUP, axis=0)
    gate = jax.nn.sigmoid(zs_ref[...])
    for g in range(NSA_G):
        s = _dot_nt(_stack_heads(q_ref, g), kv[:, g * DH:(g + 1) * DH]) + bias
        e = jnp.exp(s - jnp.max(s, axis=-1, keepdims=True))
        p = e * (1.0 / jnp.sum(e, axis=-1, keepdims=True))
        ow = _dot(p.astype(BF16), kv[:, (2 + g) * DH:(3 + g) * DH])
        for hh in range(NSA_GROUP):
            h = g * NSA_GROUP + hh
            sl = slice(h * DH, (h + 1) * DH)
            c0 = SM_NG + 3 * h
            o = (gate[:, c0:c0 + 1] * oc_ref[:, sl] + gate[:, c0 + 1:c0 + 2] * os_ref[:, sl]
                 + gate[:, c0 + 2:c0 + 3] * ow[hh * qb:(hh + 1) * qb])
            o_ref[:, sl] = o.astype(o_ref.dtype)


def win_combine(q, win_all, o_cmp, o_slc, z, nb_batch, nqp, qb, pos0):
    nblk = nqp // qb
    wtot = win_all.shape[1]
    body = functools.partial(_win_body, qb=qb, pos0=pos0, span=WINDOW + qb)
    tok = pl.BlockSpec((qb, 1024), lambda b, i: (b * nblk + i, 0))
    return pl.pallas_call(
        body, grid=(nb_batch, nblk),
        in_specs=[tok, pl.BlockSpec((None, wtot, 512), lambda b, i: (b, 0, 0)), tok, tok,
                  pl.BlockSpec((qb, LANE), lambda b, i: (b * nblk + i, Z_SM // LANE))],
        out_specs=tok, out_shape=jax.ShapeDtypeStruct((nb_batch * nqp, 1024), BF16),
        compiler_params=_cp(("parallel", "parallel")), name="win_combine")(q, win_all, o_cmp, o_slc, z)


def _pad_rows(x, n):
    if x.shape[0] == n:
        return x
    return jnp.concatenate([x, jnp.zeros((n - x.shape[0], x.shape[1]), x.dtype)], axis=0)


def _cumsum_rows(x):
    n = x.shape[0]
    tri = jnp.where(_iota((n, n), 0) >= _iota((n, n), 1), 1.0, 0.0).astype(BF16)
    hi, lo = _split(x)
    lo2 = (x - hi.astype(F32) - lo.astype(F32)).astype(BF16)
    return _dot(tri, hi) + (_dot(tri, lo) + _dot(tri, lo2))


def _log_sigmoid(x):
    return jnp.minimum(x, 0.0) - jnp.log1p(jnp.exp(-jnp.abs(x)))


def _softplus(x):
    return jnp.maximum(x, 0.0) + jnp.log1p(jnp.exp(-jnp.abs(x)))


def _mlstm_body(q_ref, k_ref, v_ref, og_ref, zs_ref, b_ref, ng_ref, c0_ref, n0_ref, m0_ref,
                o_ref, c_out, n_out, m_out, c_sc, n_sc, m_sc, *, nvalid):
    ck = pl.program_id(1)
    n = CHUNK
    lin = q_ref.shape[0]

    @pl.when(ck == 0)
    def _():
        c_sc[...] = c0_ref[...]
        n_sc[...] = n0_ref[...]
        m_sc[...] = m0_ref[...]

    gates = _pad_rows(zs_ref[...], n) + b_ref[...]
    lane = _iota((1, LANE), 1)
    is_f = (lane >= SM_MF) & (lane < SM_MF + ML_H)
    lf_all = jnp.where(is_f, _log_sigmoid(gates), 0.0)
    ig_all = gates
    if nvalid < n:
        live = _iota((n, 1), 0) < nvalid
        lf_all = jnp.where(live, lf_all, 0.0)
        ig_all = jnp.where(live, gates, NEG_INF)
    comb = jnp.where(is_f, _cumsum_rows(lf_all), ig_all)
    comb_t = comb.T
    incl = _iota((n, n), 0) >= _iota((n, n), 1)
    hs = range(ML_H)
    sk = [slice(h * ML_DK, (h + 1) * ML_DK) for h in hs]
    sv = [slice(h * ML_DV, (h + 1) * ML_DV) for h in hs]
    c0 = [c_sc[h] for h in hs]
    n0 = [n_sc[h] for h in hs]
    m0 = [m_sc[h] for h in hs]
    q = [_pad_rows(q_ref[:, sk[h]], n) for h in hs]
    k = [_pad_rows(k_ref[:, sk[h]], n) * (ML_DK ** -0.5) for h in hs]
    q16 = [q[h].astype(BF16) for h in hs]
    v16 = [_pad_rows(v_ref[:, sv[h]], n).astype(BF16) for h in hs]
    ig = [comb[:, SM_MI + h:SM_MI + h + 1] for h in hs]
    bc = [comb[:, SM_MF + h:SM_MF + h + 1] for h in hs]
    dlog = [jnp.where(incl, bc[h] + (comb_t[SM_MI + h:SM_MI + h + 1, :] - comb_t[SM_MF + h:SM_MF + h + 1, :]),
                      -jnp.inf) for h in hs]
    qk = [_dot_nt(q16[h], k[h].astype(BF16)) for h in hs]
    qc = [_dot(q16[h], c0[h].astype(BF16)) for h in hs]
    inter = [bc[h] + m0[h] for h in hs]
    m_t = [jnp.maximum(inter[h], jnp.max(dlog[h], axis=1, keepdims=True)) for h in hs]
    a = [qk[h] * jnp.exp(dlog[h] - m_t[h]) for h in hs]
    w_inter = [jnp.exp(inter[h] - m_t[h]) for h in hs]
    num = [_dot(a[h].astype(BF16), v16[h]) + w_inter[h] * qc[h] for h in hs]
    den = [jnp.sum(a[h], axis=1, keepdims=True) + w_inter[h] * jnp.sum(q[h] * n0[h], axis=1, keepdims=True)
           for h in hs]
    hh = [num[h] / jnp.maximum(jnp.abs(den[h]), jnp.exp(-m_t[h])) for h in hs]
    m_new = [m_t[h][n - 1:n] for h in hs]
    kw = [k[h] * jnp.exp(bc[h][n - 1:n] - bc[h] + ig[h] - m_new[h]) for h in hs]
    decay = [jnp.exp(bc[h][n - 1:n] + m0[h] - m_new[h]) for h in hs]
    c_new = [decay[h] * c0[h] + _dot(kw[h].T.astype(BF16), v16[h]) for h in hs]
    n_new = [decay[h] * n0[h] + jnp.sum(kw[h], axis=0, keepdims=True) for h in hs]
    for h in hs:
        c_sc[h] = c_new[h]
        n_sc[h] = n_new[h]
        m_sc[h] = m_new[h]
        c_out[h] = c_new[h]
        n_out[h] = n_new[h]
        m_out[h] = m_new[h]
        hn = hh[h] * lax.rsqrt(jnp.mean(hh[h] * hh[h], axis=-1, keepdims=True) + EPS) * ng_ref[:, sv[h]]
        o = jax.nn.sigmoid(_pad_rows(og_ref[:, sv[h]], n)) * hn
        o_ref[:, sv[h]] = o[:lin].astype(o_ref.dtype)


def mlstm_scan(z, bias_row, norm_g, c0, n0, m0, nb_batch, nqp, nvalid):
    lin = min(CHUNK, nqp)
    nc = nqp // lin
    tok = lambda w, off: pl.BlockSpec((lin, w), lambda b, c: (b * nc + c, off // w))
    body = functools.partial(_mlstm_body, nvalid=min(nvalid, CHUNK))
    st = lambda shape: pl.BlockSpec((None, ML_H) + shape, lambda b, c: (b, 0, 0, 0))
    wq, wv = ML_H * ML_DK, ML_H * ML_DV
    return pl.pallas_call(
        body, grid=(nb_batch, nc),
        in_specs=[tok(wq, Z_MQ), tok(wq, Z_MK), tok(wv, Z_MV), tok(wv, Z_MO), tok(LANE, Z_SM),
                  pl.BlockSpec((1, LANE), lambda b, c: (0, 0)), pl.BlockSpec((1, wv), lambda b, c: (0, 0)),
                  st((ML_DK, ML_DV)), st((1, ML_DK)), st((1, 1))],
        out_specs=[pl.BlockSpec((lin, wv), lambda b, c: (b * nc + c, 0)),
                   st((ML_DK, ML_DV)), st((1, ML_DK)), st((1, 1))],
        out_shape=[jax.ShapeDtypeStruct((nb_batch * nqp, wv), BF16),
                   jax.ShapeDtypeStruct((nb_batch, ML_H, ML_DK, ML_DV), F32),
                   jax.ShapeDtypeStruct((nb_batch, ML_H, 1, ML_DK), F32),
                   jax.ShapeDtypeStruct((nb_batch, ML_H, 1, 1), F32)],
        scratch_shapes=[pltpu.VMEM((ML_H, ML_DK, ML_DV), F32), pltpu.VMEM((ML_H, 1, ML_DK), F32),
                        pltpu.VMEM((ML_H, 1, 1), F32)],
        compiler_params=_cp(("parallel", "arbitrary")),
        name="mlstm_scan")(z, z, z, z, z, bias_row, norm_g.reshape(1, wv), c0, n0, m0)


def _gdn_body(q_ref, k_ref, v_ref, gz_ref, zs_ref, b_ref, a_ref, ng_ref, s0_ref,
              o_ref, s_out, s_sc, *, nvalid):
    ck = pl.program_id(1)
    n = CHUNK
    lin = q_ref.shape[0]

    @pl.when(ck == 0)
    def _():
        s_sc[...] = s0_ref[...]

    zs = _pad_rows(zs_ref[...], n)
    lane = _iota((1, LANE), 1)
    is_a = (lane >= SM_GA) & (lane < SM_GA + GD_H)
    g_all = jnp.where(is_a, a_ref[...] * _softplus(zs + b_ref[...]), 0.0)
    beta_all = jax.nn.sigmoid(zs)
    if nvalid < n:
        live = _iota((n, 1), 0) < nvalid
        g_all = jnp.where(live, g_all, 0.0)
        beta_all = jnp.where(live, beta_all, 0.0)
    gc_all = _cumsum_rows(g_all)
    gc_t = gc_all.T
    incl = _iota((n, n), 0) >= _iota((n, n), 1)
    strict = _iota((n, n), 0) > _iota((n, n), 1)
    eye = jnp.where(_iota((n, n), 0) == _iota((n, n), 1), 1.0, 0.0)
    hs = range(GD_H)
    sl = [slice(h * GD_D, (h + 1) * GD_D) for h in hs]
    s0 = [s_sc[h] for h in hs]
    q16 = [_pad_rows(q_ref[:, sl[h]], n).astype(BF16) for h in hs]
    k = [_pad_rows(k_ref[:, sl[h]], n) for h in hs]
    k16 = [k[h].astype(BF16) for h in hs]
    gc = [gc_all[:, SM_GA + h:SM_GA + h + 1] for h in hs]
    beta = [beta_all[:, SM_GB + h:SM_GB + h + 1] for h in hs]
    dec = [jnp.where(incl, jnp.exp(jnp.where(incl, gc[h] - gc_t[SM_GA + h:SM_GA + h + 1, :], 0.0)), 0.0) for h in hs]
    egc = [jnp.exp(gc[h]) for h in hs]
    kb = [k[h] * beta[h] for h in hs]
    xp = [-jnp.where(strict, _dot_nt(kb[h].astype(BF16), k16[h]) * dec[h], 0.0) for h in hs]
    tinv = [eye + xp[h] for h in hs]
    for _ in range(int(math.log2(n)) - 1):
        xp = [_dot3(xp[h], xp[h]) for h in hs]
        tinv = [tinv[h] + _dot3(tinv[h], xp[h]) for h in hs]
    rhs = [jnp.concatenate([_pad_rows(v_ref[:, sl[h]], n) * beta[h], kb[h] * egc[h]], axis=1) for h in hs]
    sol = [_dot3(tinv[h], rhs[h]) for h in hs]
    s016 = [s0[h].astype(BF16) for h in hs]
    u16 = [(sol[h][:, :GD_D] - _dot(sol[h][:, GD_D:].astype(BF16), s016[h])).astype(BF16) for h in hs]
    qk = [(_dot_nt(q16[h], k16[h]) * dec[h]).astype(BF16) for h in hs]
    o = [_dot(qk[h], u16[h]) + egc[h] * _dot(q16[h], s016[h]) for h in hs]
    kw = [(k[h] * jnp.exp(gc[h][n - 1:n] - gc[h])).T.astype(BF16) for h in hs]
    s_new = [jnp.exp(gc[h][n - 1:n]) * s0[h] + _dot(kw[h], u16[h]) for h in hs]
    for h in hs:
        s_sc[h] = s_new[h]
        s_out[h] = s_new[h]
        on = o[h] * lax.rsqrt(jnp.mean(o[h] * o[h], axis=-1, keepdims=True) + EPS) * ng_ref[...]
        o_ref[:, sl[h]] = (on * _silu(_pad_rows(gz_ref[:, sl[h]], n)))[:lin].astype(o_ref.dtype)


def gdn_scan(qkv, z, dt_row, nega_row, norm_g, s0, nb_batch, nqp, nvalid):
    lin = min(CHUNK, nqp)
    nc = nqp // lin
    w = GD_H * GD_D
    body = functools.partial(_gdn_body, nvalid=min(nvalid, CHUNK))
    tokq = lambda part: pl.BlockSpec((lin, w), lambda b, c: (b * nc + c, part))
    st = pl.BlockSpec((None, GD_H, GD_D, GD_D), lambda b, c: (b, 0, 0, 0))
    row = pl.BlockSpec((1, LANE), lambda b, c: (0, 0))
    return pl.pallas_call(
        body, grid=(nb_batch, nc),
        in_specs=[tokq(0), tokq(1), tokq(2),
                  pl.BlockSpec((lin, w), lambda b, c: (b * nc + c, Z_GZ // w)),
                  pl.BlockSpec((lin, LANE), lambda b, c: (b * nc + c, Z_SM // LANE)),
                  row, row, row, st],
        out_specs=[pl.BlockSpec((lin, w), lambda b, c: (b * nc + c, 0)), st],
        out_shape=[jax.ShapeDtypeStruct((nb_batch * nqp, w), BF16),
                   jax.ShapeDtypeStruct((nb_batch, GD_H, GD_D, GD_D), F32)],
        scratch_shapes=[pltpu.VMEM((GD_H, GD_D, GD_D), F32)],
        compiler_params=_cp(("parallel", "arbitrary")),
        name="gdn_scan")(qkv, qkv, qkv, z, z, dt_row, nega_row, norm_g.reshape(1, GD_D), s0)


HALO = 8
FFN_SUBTILES = 8


def _conv_apply(buf, x, w_ref, width):
    tb = x.shape[0]
    buf[HALO:HALO + tb, :] = x
    y = w_ref[width - 1:width, :] * x
    for kk in range(1, width):
        y = y + w_ref[width - 1 - kk:width - kk, :] * buf[HALO - kk:HALO - kk + tb, :]
    return y


def _conv_rows(buf, x, hist_ref, w_ref, width, first):
    @pl.when(first)
    def _():
        buf[0:HALO, :] = hist_ref[...]

    return _conv_apply(buf, x, w_ref, width)


def _gdn_prep_body(z_ref, hist_ref, w_ref, o_ref, buf):
    ct = pl.program_id(1)
    tb = z_ref.shape[0]
    x = z_ref[...]
    y = _silu(_conv_rows(buf, x, hist_ref, w_ref, GDN_CONV, pl.program_id(2) == 0))
    buf[0:HALO, :] = x[tb - HALO:tb, :]
    part = ct // (GD_H * GD_D // y.shape[1])
    qscale = jnp.where(part == 0, GD_D ** -0.5, 1.0)
    for hh in range(y.shape[1] // GD_D):
        sl = slice(hh * GD_D, (hh + 1) * GD_D)
        yh = y[:, sl]
        nrm = yh * lax.rsqrt(jnp.sum(yh * yh, axis=-1, keepdims=True) + EPS) * qscale
        o_ref[:, sl] = jnp.where(part == 2, yh, nrm)


def gdn_prep(z, hist, w, nb_batch, nqp, tb, tc=512):
    nblk = nqp // tb
    ncol = 3 * GD_H * GD_D
    return pl.pallas_call(
        _gdn_prep_body, grid=(nb_batch, ncol // tc, nblk),
        in_specs=[pl.BlockSpec((tb, tc), lambda b, c, i: (b * nblk + i, Z_GQKV // tc + c)),
                  pl.BlockSpec((None, HALO, tc), lambda b, c, i: (b, 0, c)),
                  pl.BlockSpec((GDN_CONV, tc), lambda b, c, i: (0, c))],
        out_specs=pl.BlockSpec((tb, tc), lambda b, c, i: (b * nblk + i, c)),
        out_shape=jax.ShapeDtypeStruct((nb_batch * nqp, ncol), F32),
        scratch_shapes=[pltpu.VMEM((HALO + tb, tc), F32)],
        compiler_params=_cp(("parallel", "parallel", "arbitrary")), name="gdn_prep")(z, hist, w)


def _ffn_up_body(h_ref, wa_ref, wb_ref, ca_ref, cb_ref, ha_ref, hb_ref, act_ref, ta_ref, tb_ref, bufa, bufb,
                 *, bps, nseq, tail0):
    i = pl.program_id(1)
    tm = h_ref.shape[0]
    halves = ((wa_ref, ca_ref, ha_ref, ta_ref, bufa), (wb_ref, cb_ref, hb_ref, tb_ref, bufb))
    if nseq == 1:
        nsub = FFN_SUBTILES if tm % (FFN_SUBTILES * HALO) == 0 else 1
        rs = tm // nsub

        @pl.when((i % bps) == 0)
        def _():
            for _, _, hist_ref, _, buf in halves:
                buf[0:HALO, :] = hist_ref[0]

        def conv_sub(buf, c_ref, off):
            y = c_ref[FFN_CONV - 1:FFN_CONV, :] * buf[HALO + off:HALO + off + rs, :]
            for kk in range(1, FFN_CONV):
                y = y + c_ref[FFN_CONV - 1 - kk:FFN_CONV - kk, :] * buf[HALO + off - kk:HALO + off - kk + rs, :]
            return y

        for r in range(nsub + 1):
            if r < nsub:
                hr = h_ref[r * rs:(r + 1) * rs, :]
                for w_ref, _, _, _, buf in halves:
                    buf[HALO + r * rs:HALO + (r + 1) * rs, :] = _dot(hr, w_ref[...])
            if r > 0:
                off = (r - 1) * rs
                act_ref[off:off + rs, :] = (_silu(conv_sub(bufa, ca_ref, off))
                                            * conv_sub(bufb, cb_ref, off)).astype(act_ref.dtype)
        for _, _, _, t_ref, buf in halves:
            t_ref[0] = buf[HALO + tail0:HALO + tail0 + HALO, :]
            buf[0:HALO, :] = buf[tm:tm + HALO, :]
    else:
        rows = tm // nseq
        h = h_ref[...]
        ys = []
        for w_ref, c_ref, hist_ref, t_ref, buf in halves:
            u = _dot(h, w_ref[...])
            parts = []
            for s in range(nseq):
                buf[0:HALO, :] = hist_ref[s]
                parts.append(_conv_apply(buf, u[s * rows:(s + 1) * rows], c_ref, FFN_CONV))
                t_ref[s] = u[s * rows + tail0:s * rows + tail0 + HALO, :]
            ys.append(jnp.concatenate(parts, axis=0))
        act_ref[...] = (_silu(ys[0]) * ys[1]).astype(act_ref.dtype)


def ffn_up_act(h, w_up, layer, conv_w, hist, nb_batch, nq, nqp, tm, tn):
    t, d = h.shape
    nj = FFN_PAD // tn
    bps = max(1, nqp // tm)
    nseq = max(1, tm // nqp)
    rows = tm // nseq
    assert (nq - 1) // rows == bps - 1 and (nq - 1) % HALO >= FFN_CONV - 2
    tail0 = ((nq - 1) % rows) // HALO * HALO
    body = functools.partial(_ffn_up_body, bps=bps, nseq=nseq, tail0=tail0)
    col = lambda half: (lambda j, i: (0, half * nj + j))
    seq = lambda half: (lambda j, i: (i // bps, 0, half * nj + j))
    tail = jax.ShapeDtypeStruct((nb_batch, HALO, FFN_PAD), F32)
    return pl.pallas_call(
        body, grid=(nj, t // tm),
        in_specs=[pl.BlockSpec((tm, d), lambda j, i: (i, 0)),
                  pl.BlockSpec((None, d, tn), lambda j, i: (layer, 0, j)),
                  pl.BlockSpec((None, d, tn), lambda j, i: (layer, 0, nj + j)),
                  pl.BlockSpec((FFN_CONV, tn), col(0)), pl.BlockSpec((FFN_CONV, tn), col(1)),
                  pl.BlockSpec((nseq, HALO, tn), seq(0)), pl.BlockSpec((nseq, HALO, tn), seq(1))],
        out_specs=[pl.BlockSpec((tm, tn), lambda j, i: (i, j)),
                   pl.BlockSpec((nseq, HALO, tn), lambda j, i: (i // bps, 0, j)),
                   pl.BlockSpec((nseq, HALO, tn), lambda j, i: (i // bps, 0, j))],
        out_shape=[jax.ShapeDtypeStruct((t, FFN_PAD), BF16), tail, tail],
        scratch_shapes=[pltpu.VMEM((HALO + tm, tn), F32), pltpu.VMEM((HALO + tm, tn), F32)],
        compiler_params=_cp(("parallel", "arbitrary")),
        name="ffn_up_act")(h, w_up, w_up, conv_w, conv_w, hist, hist)


def _merge_body(oa_ref, ob_ref, oc_ref, w_ref, g0_ref, g1_ref, g2_ref, o_ref):
    acc = jax.nn.sigmoid(g0_ref[...]) * _dot(oa_ref[...], w_ref[0])
    acc = acc + jax.nn.sigmoid(g1_ref[...]) * _dot(ob_ref[...], w_ref[1])
    acc = acc + jax.nn.sigmoid(g2_ref[...]) * _dot(oc_ref[...], w_ref[2])
    o_ref[...] = acc.astype(o_ref.dtype)


def merge_branches(oa, ob, oc, wb, layer, z, tm, tn):
    t = oa.shape[0]
    d = wb.shape[3]
    tok = pl.BlockSpec((tm, 1024), lambda i, j: (i, 0))
    gate = lambda nbr: pl.BlockSpec((tm, tn), lambda i, j: (i, (Z_BR + nbr * d) // tn + j))
    return pl.pallas_call(
        _merge_body, grid=(t // tm, d // tn),
        in_specs=[tok, tok, tok, pl.BlockSpec((None, 3, 1024, tn), lambda i, j: (layer, 0, 0, j)),
                  gate(0), gate(1), gate(2)],
        out_specs=pl.BlockSpec((tm, tn), lambda i, j: (i, j)),
        out_shape=jax.ShapeDtypeStruct((t, d), BF16),
        compiler_params=_cp(("parallel", "parallel")), name="merge_branches")(oa, ob, oc, wb, z, z, z)


def _pad_halves(a, f):
    pad = ((0, 0),) * (a.ndim - 1) + ((0, FFN_PAD - f),)
    return jnp.concatenate([jnp.pad(a[..., :f], pad), jnp.pad(a[..., f:], pad)], axis=-1)


def _stacked_weights(w_in, w_up, w_down, w_branch, w_out):
    offs = np.cumsum([0, 1024, 1536, 24, 512, 512, 1024, 4, 4, 1024, 3072, 8, 8, 1024, 6144])
    lead = w_in.shape[:2]
    seg = lambda i: w_in[:, :, offs[i]:offs[i + 1]]
    kv = seg(1).reshape(lead + (3, 2, NSA_G * DH))
    small = jnp.concatenate([seg(2), seg(6), seg(7), seg(10), seg(11)], axis=-1)
    small = jnp.pad(small, ((0, 0), (0, 0), (0, Z_COLS - Z_SM - small.shape[-1])))
    gap = jnp.zeros(lead + (Z_KV - Z_MQ - ML_H * ML_DK,), w_in.dtype)
    w_all = jnp.concatenate([seg(0), seg(5), seg(8), seg(12), seg(9), seg(13), seg(3), gap,
                             kv[:, :, :, 0].reshape(lead + (KV_W // 2,)), kv[:, :, :, 1].reshape(lead + (KV_W // 2,)),
                             seg(4), small], axis=-1)
    assert w_all.shape[-1] == Z_COLS
    f = w_down.shape[1]
    return dict(w_all=w_all.astype(BF16), w_up=_pad_halves(w_up, f).astype(BF16),
                w_down=jnp.pad(w_down, ((0, 0), (0, FFN_PAD - f), (0, 0))).astype(BF16),
                w_branch=w_branch.astype(BF16), w_out=w_out.astype(BF16))


def _layer_weights(p):
    w1 = p['w_cmp1']
    wcat = jnp.concatenate([w1[:, :CMP_STRIDE].reshape(2, CMP_STRIDE * DH, DH),
                            w1[:, CMP_STRIDE:].reshape(2, CMP_STRIDE * DH, DH)], axis=2).astype(BF16)

    def lane_row(pairs):
        row = jnp.zeros((LANE,), F32)
        for off, v in pairs:
            row = row.at[off:off + v.shape[0]].set(v.astype(F32))
        return row.reshape(1, LANE)

    return dict(
        ffn_conv_w=_pad_halves(p['ffn_conv_w'], p['ffn_conv_w'].shape[-1] // 2), wcat=wcat,
        pe8=jnp.pad(p['cmp_pe'].reshape(2, 1, CMP_BLOCK * DH), ((0, 0), (0, 7), (0, 0))).astype(BF16),
        w1flat=w1.reshape(2, CMP_BLOCK * DH, DH).astype(BF16), w2=p['w_cmp2'].astype(BF16),
        ml_bias=lane_row([(SM_MI, p['mlstm_b_i']), (SM_MF, p['mlstm_b_f'])]),
        gd_dt=lane_row([(SM_GA, p['gdn_dt_bias'])]),
        gd_nega=lane_row([(SM_GA, -jnp.exp(p['gdn_a_log'].astype(F32)))]))


def _hist_rows(state):
    return jnp.pad(state, ((0, 0), (HALO - state.shape[1], 0), (0, 0)))


def _group_layer(x, grp, st, p, w, ws, layer):
    nb, nq, nqp, pos0 = grp['nb'], grp['nq'], grp['nqp'], grp['pos0']
    t = nb * nqp
    big = nqp >= 512
    tm = 1024 if big else t
    z = norm_matmul(x, p['norm1_g'], ws['w_all'], layer, tm, 1024 if big else 2048, name="in_proj")

    tb = 256 if big else nqp
    q_rot, rows, kvs, win_new = nsa_prep(z, grp['rope'], p['q_norm_g'], p['k_norm_g'], nqp, tb)
    pps = KC // PAGE
    if pos0 == 0:
        nblk_c = nqp // CMP_STRIDE
        ncp = nblk_c
        fs = compress_pages(rows.reshape(nb * nqp // PAGE, PAGE_SUB, LANE), w['wcat'], nb, nqp // KC, pps)
    else:
        n_pages = st['table'].shape[1]
        nblk_c = (n_pages + 1) * (PAGE // CMP_STRIDE)
        ncp = -(-nblk_c // LANE) * LANE
        live = (jnp.arange(nqp * ROW_SUB) < nq * ROW_SUB)[None, :, None]
        newpage = jnp.pad(jnp.where(live, rows.reshape(nb, nqp * ROW_SUB, LANE), 0.0),
                          ((0, 0), (0, PAGE_SUB - nqp * ROW_SUB), (0, 0)))
        fs, kvs = compress_pages(st['cache'], w['wcat'], nb, ncp * CMP_STRIDE // KC, pps, table=st['table'],
                                 layer=layer, newpage=newpage)
        kvs = kvs.reshape(-1, 512)
    kvc = compress_finish(fs, w['pe8'], w['w1flat'], w['w2'])
    ns = nblk_c * CMP_STRIDE // SLC_BLOCK
    nsp = -(-ns // LANE) * LANE
    t_all = pos0 + nq
    nsel = min(N_SELECT, -(-t_all // SLC_BLOCK))
    qb = 128 if big else nqp
    o_cmp, sel = cmp_select(q_rot, kvc, nb, nqp, qb, pos0, ns, nsp, nsel)
    o_slc = slc_attention(q_rot, sel, kvs, nb, nqp, qb, pos0, nsp, KC if big else 2 * KC)
    win3 = win_new.reshape(nb, nqp, 512)
    win_all = jnp.concatenate([st['win'], win3], axis=1)
    o_nsa = win_combine(q_rot, win_all, o_cmp, o_slc, z, nb, nqp, qb, pos0)
    win_state = jnp.concatenate([st['win'], win3[:, :nq]], axis=1)[:, -WINDOW:]

    o_ml, m_c, m_n, m_m = mlstm_scan(z, w['ml_bias'], p['mlstm_norm_g'], st['m_c'], st['m_n'], st['m_m'],
                                     nb, nqp, nq)

    tbc = 512 if big else nqp
    qkv = gdn_prep(z, _hist_rows(st['g_conv']), p['gdn_conv_w'], nb, nqp, tbc)
    o_gd, g_s = gdn_scan(qkv, z, w['gd_dt'], w['gd_nega'], p['gdn_norm_g'], st['g_s'], nb, nqp, nq)
    z3 = z.reshape(nb, nqp, Z_COLS)
    g_conv_new = z3[:, nq - (GDN_CONV - 1):nq, Z_GQKV:Z_GQKV + 3 * GD_H * GD_D]

    mrg = merge_branches(o_nsa, o_ml, o_gd, ws['w_branch'], layer, z, tm, 512)
    x1, h2 = matmul_residual_norm(mrg, ws['w_out'], layer, min(tm, 512), x, p['norm2_g'], name="out_proj")

    f = st['f_conv'].shape[-1] // 2
    act, tail_a, tail_b = ffn_up_act(h2, ws['w_up'], layer, w['ffn_conv_w'], _hist_rows(_pad_halves(st['f_conv'], f)),
                                     nb, nq, nqp, tm, 512)
    x2 = matmul_residual(act, ws['w_down'], layer, tm, 512, x1, name="down_proj")
    rows_blk = min(tm, nqp)
    r0 = (nq - (FFN_CONV - 1)) % rows_blk - ((nq - 1) % rows_blk) // HALO * HALO
    f_conv_new = jnp.concatenate([tail_a[:, r0:r0 + FFN_CONV - 1, :f], tail_b[:, r0:r0 + FFN_CONV - 1, :f]], axis=-1)

    new_rows = rows.reshape(nb, nqp, 4, NSA_G, DH)[:, :nq]
    states = (new_rows, win_state.reshape(nb, WINDOW, 2, NSA_G, DH), m_c, m_n.reshape(nb, ML_H, ML_DK),
              m_m.reshape(nb, ML_H), g_conv_new, g_s, f_conv_new)
    return x2, states


def kernel(x_prompt, x_sample, cache_nsa_kv, page_table, cache_nsa_win, state_mlstm_c, state_mlstm_n,
           state_mlstm_m, state_gdn_conv, state_gdn_s, state_ffn_conv, norm1_g, w_in, q_norm_g, k_norm_g,
           w_cmp1, cmp_pe, w_cmp2, mlstm_b_i, mlstm_b_f, mlstm_norm_g, gdn_conv_w, gdn_a_log, gdn_dt_bias,
           gdn_norm_g, w_branch, w_out, norm2_g, w_up, ffn_conv_w, w_down):
    depth = w_in.shape[0]
    bp, sp, d = x_prompt.shape
    bs, ss, _ = x_sample.shape
    n_pages = page_table.shape[1]
    past = n_pages * PAGE
    ssp = -(-ss // 16) * 16
    assert sp % KC == 0 and ss >= GDN_CONV - 1 and cache_nsa_win.shape[2] == WINDOW and n_pages % (KC // PAGE) == 0
    grp_p = dict(nb=bp, nq=sp, nqp=sp, pos0=0, rope=rope_tables(0, sp))
    grp_s = dict(nb=bs, nq=ss, nqp=ssp, pos0=past, rope=rope_tables(past, ssp))
    cache = cache_nsa_kv.reshape(depth, cache_nsa_kv.shape[1], PAGE_SUB, LANE)
    xp = x_prompt.reshape(bp * sp, d)
    xs = jnp.pad(x_sample, ((0, 0), (0, ssp - ss), (0, 0))).reshape(bs * ssp, d)
    ws = _stacked_weights(w_in, w_up, w_down, w_branch, w_out)
    st_p, st_s = [], []
    for l in range(depth):
        p = dict(norm1_g=norm1_g[l], q_norm_g=q_norm_g[l], k_norm_g=k_norm_g[l],
                 w_cmp1=w_cmp1[l], cmp_pe=cmp_pe[l], w_cmp2=w_cmp2[l], mlstm_b_i=mlstm_b_i[l],
                 mlstm_b_f=mlstm_b_f[l], mlstm_norm_g=mlstm_norm_g[l], gdn_conv_w=gdn_conv_w[l],
                 gdn_a_log=gdn_a_log[l], gdn_dt_bias=gdn_dt_bias[l], gdn_norm_g=gdn_norm_g[l],
                 norm2_g=norm2_g[l], ffn_conv_w=ffn_conv_w[l])
        w = _layer_weights(p)
        zero_p = dict(win=jnp.zeros((bp, WINDOW, 512), F32),
                      m_c=jnp.zeros((bp, ML_H, ML_DK, ML_DV), F32), m_n=jnp.zeros((bp, ML_H, 1, ML_DK), F32),
                      m_m=jnp.zeros((bp, ML_H, 1, 1), F32),
                      g_conv=jnp.zeros((bp, GDN_CONV - 1, 3 * GD_H * GD_D), F32),
                      g_s=jnp.zeros((bp, GD_H, GD_D, GD_D), F32),
                      f_conv=jnp.zeros((bp, FFN_CONV - 1, state_ffn_conv.shape[-1]), F32))
        xp, s_p = _group_layer(xp, grp_p, zero_p, p, w, ws, l)
        st = dict(cache=cache, table=page_table, win=cache_nsa_win[l].reshape(bs, WINDOW, 512),
                  m_c=state_mlstm_c[l], m_n=state_mlstm_n[l].reshape(bs, ML_H, 1, ML_DK),
                  m_m=state_mlstm_m[l].reshape(bs, ML_H, 1, 1), g_conv=state_gdn_conv[l],
                  g_s=state_gdn_s[l], f_conv=state_ffn_conv[l])
        xs, s_s = _group_layer(xs, grp_s, st, p, w, ws, l)
        st_p.append(s_p)
        st_s.append(s_s)
    outs = [xp.reshape(bp, sp, d), xs.reshape(bs, ssp, d)[:, :ss]]
    for i in range(8):
        outs.append(jnp.stack([s[i] for s in st_p], axis=0))
        outs.append(jnp.stack([s[i] for s in st_s], axis=0))
    return tuple(outs)
```
